```python
import math
import jax, jax.numpy as jnp
from jax import lax
import numpy as np

D_MODEL = 2048
BATCH = 2
SEQ = 16384
DEPTH = 1
DEC_BATCH = 16
DEC_SEQ = 64
PAST_LEN = 2048

CHUNK = 64
D_MIX = D_MODEL
D_MLSTM = D_MIX // 2
M_HEADS = 4
M_HEAD_DIM = D_MLSTM // M_HEADS
CONV_W = 4
D_S5 = D_MIX - D_MLSTM
S5_GROUP = 16
S5_GROUPS = D_S5 // S5_GROUP
S5_STATE = 64
N_EXPERT_GROUPS = 4
EXPERTS_PER_GROUP = 4
N_EXPERTS = N_EXPERT_GROUPS * EXPERTS_PER_GROUP
TOP_K_INNER = 2
D_FF_EXPERT = 512
ALPHA = (2 * DEPTH) ** 0.25
BETA = (8 * DEPTH) ** -0.25
LN_EPS = 1e-5
NEG = -1e30
PROJ_COLS = 4 * D_MLSTM + 2 * M_HEADS + D_S5

kernel_name = "hymba_mlstm_s5_hmoe_stream_step"

F32 = jnp.float32


def layer_norm(x, g, b):
    xf = x.astype(F32)
    mu = xf.mean(-1, keepdims=True)
    var = jnp.square(xf - mu).mean(-1, keepdims=True)
    return ((xf - mu) * lax.rsqrt(var + LN_EPS) * g.astype(F32) + b.astype(F32)).astype(x.dtype)


def to_chunks(x, axis, L):
    T = x.shape[axis]
    x = x.reshape(x.shape[:axis] + (T // L, L) + x.shape[axis + 1:])
    return jnp.moveaxis(x, axis, 0)


def from_chunks(y, axis):
    y = jnp.moveaxis(y, 0, axis)
    return y.reshape(y.shape[:axis] + (y.shape[axis] * y.shape[axis + 1],) + y.shape[axis + 2:])


def causal_conv(x, buf, w, b):
    T = x.shape[1]
    xp = jnp.concatenate([buf.astype(x.dtype), x], axis=1)
    y = b
    for j in range(CONV_W):
        y = y + xp[:, j:j + T] * w[j]
    return y, xp[:, -(CONV_W - 1):]


def mlstm_chunk(carry, inp):
    C, n, m = carry
    q, k, v, ig, lf = inp
    L = q.shape[2]
    b = lax.cumsum(lf, axis=2)
    causal = jnp.tril(jnp.ones((L, L), dtype=bool))
    dmat = jnp.where(causal, b[..., :, None] - b[..., None, :] + ig[..., None, :], NEG)
    inter = b + m[..., None]
    m_t = jnp.maximum(inter, dmat.max(-1))
    w_intra = jnp.exp(dmat - m_t[..., None])
    w_inter = jnp.exp(inter - m_t)
    s = jnp.einsum('bhtd,bhsd->bhts', q, k) * w_intra
    num = w_inter[..., None] * jnp.einsum('bhtk,bhkv->bhtv', q, C) + jnp.einsum('bhts,bhsv->bhtv', s, v)
    den = w_inter * jnp.einsum('bhtk,bhk->bht', q, n) + s.sum(-1)
    h = num / jnp.maximum(jnp.abs(den), jnp.exp(-m_t))[..., None]
    m_new = m_t[..., -1]
    w_s = jnp.exp(b[..., -1:] - b + ig - m_new[..., None])
    decay = jnp.exp(b[..., -1] + m - m_new)
    C_new = decay[..., None, None] * C + jnp.einsum('bhs,bhsk,bhsv->bhkv', w_s, k, v)
    n_new = decay[..., None] * n + jnp.einsum('bhs,bhsk->bhk', w_s, k)
    return (C_new, n_new, m_new), h


def mlstm_mixer(qk_pre, v, og, ig, fg, conv_buf, C0, n0, m0, conv_w, conv_b, b_gates, norm_w):
    Bn, T, _ = qk_pre.shape
    qk, conv_new = causal_conv(qk_pre, conv_buf, conv_w, conv_b)
    qk = jax.nn.silu(qk.astype(F32))

    def heads(t):
        return t.reshape(Bn, T, M_HEADS, M_HEAD_DIM).transpose(0, 2, 1, 3)

    q = heads(qk[..., :D_MLSTM])
    k = heads(qk[..., D_MLSTM:]) * (M_HEAD_DIM ** -0.5)
    vh = heads(v.astype(F32))
    i_pre = (ig.astype(F32) + b_gates[:M_HEADS].astype(F32)).transpose(0, 2, 1)
    logf = jax.nn.log_sigmoid(fg.astype(F32) + b_gates[M_HEADS:].astype(F32)).transpose(0, 2, 1)
    L = min(T, CHUNK)
    xs = (to_chunks(q, 2, L), to_chunks(k, 2, L), to_chunks(vh, 2, L),
          to_chunks(i_pre, 2, L), to_chunks(logf, 2, L))
    (C, n, m), h = lax.scan(mlstm_chunk, (C0.astype(F32), n0.astype(F32), m0.astype(F32)), xs)
    h = from_chunks(h, 2)
    mu = h.mean(-1, keepdims=True)
    var = jnp.square(h - mu).mean(-1, keepdims=True)
    h = (h - mu) * lax.rsqrt(var + LN_EPS)
    h = h.transpose(0, 2, 1, 3).reshape(Bn, T, D_MLSTM) * norm_w.astype(F32)
    out = jax.nn.sigmoid(og.astype(F32)) * h
    return out, conv_new, C, n, m


def complex_affine_combine(e1, e2):
    a1r, a1i, b1r, b1i = e1
    a2r, a2i, b2r, b2i = e2
    return (a2r * a1r - a2i * a1i, a2r * a1i + a2i * a1r,
            a2r * b1r - a2i * b1i + b2r, a2r * b1i + a2i * b1r + b2i)


def s5_mixer(u, s_re0, s_im0, lam_re, lam_im, log_dt, b_re, b_im, c_re, c_im, d_skip, w_glu, b_glu):
    Bn, T, _ = u.shape
    uf = u.astype(F32).reshape(Bn, T, S5_GROUPS, S5_GROUP)
    lam_re = lam_re.astype(F32)
    lam_im = lam_im.astype(F32)
    dt = jnp.exp(log_dt.astype(F32))[:, None]
    mag = jnp.exp(lam_re * dt)
    a_re = mag * jnp.cos(lam_im * dt)
    a_im = mag * jnp.sin(lam_im * dt)
    e_re = a_re - 1.0
    e_im = a_im
    lam_sq = lam_re * lam_re + lam_im * lam_im
    coef_re = (e_re * lam_re + e_im * lam_im) / lam_sq
    coef_im = (e_im * lam_re - e_re * lam_im) / lam_sq
    b_re = b_re.astype(F32)
    b_im = b_im.astype(F32)
    bb_re = coef_re[..., None] * b_re - coef_im[..., None] * b_im
    bb_im = coef_re[..., None] * b_im + coef_im[..., None] * b_re
    c_re = c_re.astype(F32)
    c_im = c_im.astype(F32)
    dg = d_skip.astype(F32).reshape(S5_GROUPS, S5_GROUP)

    def step(carry, u_c):
        s_re, s_im = carry
        x_re = jnp.einsum('blgc,gpc->blgp', u_c, bb_re)
        x_im = jnp.einsum('blgc,gpc->blgp', u_c, bb_im)
        ar = jnp.broadcast_to(a_re, x_re.shape)
        ai = jnp.broadcast_to(a_im, x_re.shape)
        A_re, A_im, X_re, X_im = lax.associative_scan(complex_affine_combine, (ar, ai, x_re, x_im), axis=1)
        h_re = X_re + A_re * s_re[:, None] - A_im * s_im[:, None]
        h_im = X_im + A_re * s_im[:, None] + A_im * s_re[:, None]
        y = (jnp.einsum('blgp,gcp->blgc', h_re, c_re) - jnp.einsum('blgp,gcp->blgc', h_im, c_im)
             + dg * u_c)
        return (h_re[:, -1], h_im[:, -1]), y

    L = min(T, CHUNK)
    (s_re, s_im), y = lax.scan(step, (s_re0.astype(F32), s_im0.astype(F32)), to_chunks(uf, 1, L))
    y = from_chunks(y, 1).reshape(Bn, T, D_S5)
    g = jax.nn.gelu(y)
    out = g * jax.nn.sigmoid(g @ w_glu.astype(F32) + b_glu.astype(F32))
    return out, s_re, s_im


def hier_moe(x, w_r1, b_r1, w_r2, b_r2, w_gate, w_up, w_down):
    shp = x.shape
    xt = x.reshape(-1, D_MODEL)
    N = xt.shape[0]
    l1 = (xt @ w_r1).astype(F32) + b_r1.astype(F32)
    p1 = jax.nn.softmax(l1, axis=-1)
    _, g_idx = lax.top_k(l1, 1)
    p_g = jnp.take_along_axis(p1, g_idx, axis=1)
    l2 = jnp.einsum('nd,gde->nge', xt, w_r2).astype(F32) + b_r2.astype(F32)
    l2_sel = jnp.take_along_axis(l2, g_idx[:, :, None], axis=1)[:, 0]
    top_v, top_i = lax.top_k(l2_sel, TOP_K_INNER)
    w2 = jax.nn.softmax(top_v, axis=-1)
    inner = jnp.sum(jax.nn.one_hot(top_i, EXPERTS_PER_GROUP, dtype=F32) * w2[..., None], axis=1)
    gates = (jax.nn.one_hot(g_idx[:, 0], N_EXPERT_GROUPS, dtype=F32)[:, :, None]
             * inner[:, None, :] * p_g[:, :, None]).reshape(N, N_EXPERTS)
    out = jnp.zeros((N, D_MODEL), F32)
    for e in range(N_EXPERTS):
        h = jax.nn.silu(xt @ w_gate[e]) * (xt @ w_up[e])
        out = out + gates[:, e:e + 1] * (h @ w_down[e]).astype(F32)
    return out.astype(x.dtype).reshape(shp)


def layer(x, conv_buf, C0, n0, m0, s_re0, s_im0, lp):
    (w_in, b_gates, conv_w, conv_b, mlstm_norm_w, lam_re, lam_im, log_dt, b_re, b_im, c_re, c_im,
     d_skip, w_glu, b_glu, w_out, ln1_g, ln1_b, w_r1, b_r1, w_r2, b_r2, w_gate, w_up, w_down,
     ln2_g, ln2_b) = lp
    p = x @ w_in
    o0 = 2 * D_MLSTM
    qk_pre = p[..., :o0]
    v = p[..., o0:o0 + D_MLSTM]
    og = p[..., o0 + D_MLSTM:o0 + 2 * D_MLSTM]
    o1 = o0 + 2 * D_MLSTM
    ig = p[..., o1:o1 + M_HEADS]
    fg = p[..., o1 + M_HEADS:o1 + 2 * M_HEADS]
    u = p[..., o1 + 2 * M_HEADS:]
    h_m, conv_new, C, n, m = mlstm_mixer(qk_pre, v, og, ig, fg, conv_buf, C0, n0, m0,
                                         conv_w, conv_b, b_gates, mlstm_norm_w)
    h_s, s_re, s_im = s5_mixer(u, s_re0, s_im0, lam_re, lam_im, log_dt, b_re, b_im, c_re, c_im,
                               d_skip, w_glu, b_glu)
    mix = jnp.concatenate([h_m, h_s], axis=-1).astype(x.dtype) @ w_out
    x1 = layer_norm(ALPHA * x + mix, ln1_g, ln1_b)
    x2 = layer_norm(ALPHA * x1 + hier_moe(x1, w_r1, b_r1, w_r2, b_r2, w_gate, w_up, w_down), ln2_g, ln2_b)
    return x2, (C, n, m, conv_new, s_re, s_im)


def setup_inputs(seed: int = 0) -> dict:
    key = jax.random.key(seed)
    ks = iter(jax.random.split(key, 48))

    def nrm(shape, s):
        return jax.random.normal(next(ks), shape, F32) * s

    x_prompt = nrm((BATCH, SEQ, D_MODEL), 1.0)
    x_sample = nrm((DEC_BATCH, DEC_SEQ, D_MODEL), 1.0)
    state_mlstm_C = nrm((DEPTH, DEC_BATCH, M_HEADS, M_HEAD_DIM, M_HEAD_DIM), 0.05)
    state_mlstm_n = nrm((DEPTH, DEC_BATCH, M_HEADS, M_HEAD_DIM), 0.05)
    state_mlstm_m = nrm((DEPTH, DEC_BATCH, M_HEADS), 0.5)
    state_conv = nrm((DEPTH, DEC_BATCH, CONV_W - 1, 2 * D_MLSTM), 1.0)
    state_s5_re = nrm((DEPTH, DEC_BATCH, S5_GROUPS, S5_STATE), 0.5)
    state_s5_im = nrm((DEPTH, DEC_BATCH, S5_GROUPS, S5_STATE), 0.5)
    col_scale = jnp.concatenate([jnp.ones((2 * D_MLSTM,), F32), jnp.full((D_MLSTM,), BETA, F32),
                                 jnp.ones((D_MLSTM + 2 * M_HEADS,), F32), jnp.full((D_S5,), BETA, F32)])
    w_in = nrm((DEPTH, D_MODEL, PROJ_COLS), D_MODEL ** -0.5) * col_scale
    b_gates = jnp.concatenate([nrm((DEPTH, M_HEADS), 0.1),
                               jnp.linspace(3.0, 6.0, M_HEADS, dtype=F32)[None] + nrm((DEPTH, M_HEADS), 0.1)], axis=-1)
    conv_w = nrm((DEPTH, CONV_W, 2 * D_MLSTM), 0.5)
    conv_b = nrm((DEPTH, 2 * D_MLSTM), 0.02)
    mlstm_norm_w = 1.0 + nrm((DEPTH, D_MLSTM), 0.02)
    s5_lam_re = -0.5 + nrm((DEPTH, S5_GROUPS, S5_STATE), 0.01)
    s5_lam_im = math.pi * jnp.arange(S5_STATE, dtype=F32) + nrm((DEPTH, S5_GROUPS, S5_STATE), 0.01)
    s5_log_dt = jax.random.uniform(next(ks), (DEPTH, S5_GROUPS), F32, math.log(1e-3), math.log(1e-1))
    s5_b_re = nrm((DEPTH, S5_GROUPS, S5_STATE, S5_GROUP), (2 * S5_GROUP) ** -0.5)
    s5_b_im = nrm((DEPTH, S5_GROUPS, S5_STATE, S5_GROUP), (2 * S5_GROUP) ** -0.5)
    s5_c_re = nrm((DEPTH, S5_GROUPS, S5_GROUP, S5_STATE), (2 * S5_STATE) ** -0.5)
    s5_c_im = nrm((DEPTH, S5_GROUPS, S5_GROUP, S5_STATE), (2 * S5_STATE) ** -0.5)
    s5_d = nrm((DEPTH, D_S5), 0.5)
    w_glu = nrm((DEPTH, D_S5, D_S5), D_S5 ** -0.5)
    b_glu = nrm((DEPTH, D_S5), 0.02)
    w_out = nrm((DEPTH, D_MIX, D_MODEL), D_MIX ** -0.5 * BETA)
    ln1_g = 1.0 + nrm((DEPTH, D_MODEL), 0.02)
    ln1_b = nrm((DEPTH, D_MODEL), 0.02)
    w_r1 = nrm((DEPTH, D_MODEL, N_EXPERT_GROUPS), D_MODEL ** -0.5)
    b_r1 = nrm((DEPTH, N_EXPERT_GROUPS), 0.01)
    w_r2 = nrm((DEPTH, N_EXPERT_GROUPS, D_MODEL, EXPERTS_PER_GROUP), D_MODEL ** -0.5)
    b_r2 = nrm((DEPTH, N_EXPERT_GROUPS, EXPERTS_PER_GROUP), 0.01)
    w_gate = nrm((DEPTH, N_EXPERTS, D_MODEL, D_FF_EXPERT), D_MODEL ** -0.5 * BETA)
    w_up = nrm((DEPTH, N_EXPERTS, D_MODEL, D_FF_EXPERT), D_MODEL ** -0.5 * BETA)
    w_down = nrm((DEPTH, N_EXPERTS, D_FF_EXPERT, D_MODEL), D_FF_EXPERT ** -0.5 * BETA)
    ln2_g = 1.0 + nrm((DEPTH, D_MODEL), 0.02)
    ln2_b = nrm((DEPTH, D_MODEL), 0.02)
    return {"x_prompt": x_prompt, "x_sample": x_sample,
            "state_mlstm_C": state_mlstm_C, "state_mlstm_n": state_mlstm_n, "state_mlstm_m": state_mlstm_m,
            "state_conv": state_conv, "state_s5_re": state_s5_re, "state_s5_im": state_s5_im,
            "w_in": w_in, "b_gates": b_gates, "conv_w": conv_w, "conv_b": conv_b,
            "mlstm_norm_w": mlstm_norm_w, "s5_lam_re": s5_lam_re, "s5_lam_im": s5_lam_im,
            "s5_log_dt": s5_log_dt, "s5_b_re": s5_b_re, "s5_b_im": s5_b_im, "s5_c_re": s5_c_re,
            "s5_c_im": s5_c_im, "s5_d": s5_d, "w_glu": w_glu, "b_glu": b_glu, "w_out": w_out,
            "ln1_g": ln1_g, "ln1_b": ln1_b, "w_r1": w_r1, "b_r1": b_r1, "w_r2": w_r2, "b_r2": b_r2,
            "w_gate": w_gate, "w_up": w_up, "w_down": w_down, "ln2_g": ln2_g, "ln2_b": ln2_b}


def reference(x_prompt, x_sample, state_mlstm_C, state_mlstm_n, state_mlstm_m, state_conv,
              state_s5_re, state_s5_im, w_in, b_gates, conv_w, conv_b, mlstm_norm_w,
              s5_lam_re, s5_lam_im, s5_log_dt, s5_b_re, s5_b_im, s5_c_re, s5_c_im, s5_d,
              w_glu, b_glu, w_out, ln1_g, ln1_b, w_r1, b_r1, w_r2, b_r2, w_gate, w_up, w_down,
              ln2_g, ln2_b):
    yp = x_prompt
    ys = x_sample
    st_p = []
    st_s = []
    for l in range(DEPTH):
        lp = (w_in[l], b_gates[l], conv_w[l], conv_b[l], mlstm_norm_w[l], s5_lam_re[l], s5_lam_im[l],
              s5_log_dt[l], s5_b_re[l], s5_b_im[l], s5_c_re[l], s5_c_im[l], s5_d[l], w_glu[l], b_glu[l],
              w_out[l], ln1_g[l], ln1_b[l], w_r1[l], b_r1[l], w_r2[l], b_r2[l], w_gate[l], w_up[l],
              w_down[l], ln2_g[l], ln2_b[l])
        zc = jnp.zeros((BATCH, CONV_W - 1, 2 * D_MLSTM), x_prompt.dtype)
        zC = jnp.zeros((BATCH, M_HEADS, M_HEAD_DIM, M_HEAD_DIM), F32)
        zn = jnp.zeros((BATCH, M_HEADS, M_HEAD_DIM), F32)
        zm = jnp.zeros((BATCH, M_HEADS), F32)
        zs = jnp.zeros((BATCH, S5_GROUPS, S5_STATE), F32)
        yp, sp = layer(yp, zc, zC, zn, zm, zs, zs, lp)
        ys, ss = layer(ys, state_conv[l], state_mlstm_C[l], state_mlstm_n[l], state_mlstm_m[l],
                       state_s5_re[l], state_s5_im[l], lp)
        st_p.append(sp)
        st_s.append(ss)
    C_p = jnp.stack([s[0] for s in st_p])
    n_p = jnp.stack([s[1] for s in st_p])
    m_p = jnp.stack([s[2] for s in st_p])
    conv_p = jnp.stack([s[3] for s in st_p])
    s5re_p = jnp.stack([s[4] for s in st_p])
    s5im_p = jnp.stack([s[5] for s in st_p])
    C_s = jnp.stack([s[0] for s in st_s])
    n_s = jnp.stack([s[1] for s in st_s])
    m_s = jnp.stack([s[2] for s in st_s])
    conv_s = jnp.stack([s[3] for s in st_s])
    s5re_s = jnp.stack([s[4] for s in st_s])
    s5im_s = jnp.stack([s[5] for s in st_s])
    return (yp, ys, C_p, n_p, m_p, conv_p, s5re_p, s5im_p, C_s, n_s, m_s, conv_s, s5re_s, s5im_s)
```

```python
import functools
import math

import jax
import jax.numpy as jnp
import numpy as np
from jax import lax
from jax.experimental import pallas as pl
from jax.experimental.pallas import tpu as pltpu

F32 = jnp.float32
BF16 = jnp.bfloat16

LANES = 128
SUBLANES = 8
VMEM_LIMIT = 56 * 1024 * 1024

M_HEADS = 4
CONV_W = 4
S5_GROUP = 16
S5_STATE = 64
N_EXPERT_GROUPS = 4
EXPERTS_PER_GROUP = 4
N_PAIRS = 6
N_BINS = N_EXPERT_GROUPS * N_PAIRS
LN_EPS = 1e-5
NEG = -1e30

ROW_TILE = 256
MOE_TILE = 256
N_STREAMS = SUBLANES
SCAN_LANES = 512


def _dot(a, b):
    return jnp.dot(a, b, preferred_element_type=F32)


def _dot_nt(a, b):
    return lax.dot_general(a, b, (((1,), (1,)), ((), ())), preferred_element_type=F32)


def _resident(shape):
    nd = len(shape)
    return pl.BlockSpec(shape, lambda *_: (0,) * nd, pipeline_mode=pl.Buffered(1))


def _params(*sem):
    return pltpu.CompilerParams(dimension_semantics=sem, vmem_limit_bytes=VMEM_LIMIT)


def _stream_perm(block, seg):
    n_seg = block // seg
    r = np.arange(block)
    src = (r % n_seg) * seg + r // n_seg
    p = np.zeros((block, block), np.float32)
    p[r, src] = 1.0
    reps = ROW_TILE // block
    return np.kron(np.eye(reps, dtype=np.float32), p)


def _inproj_kernel(x_ref, perm_ref, wqk_ref, wv_ref, wo_ref, wu_ref, wg_ref,
                   qk_ref, v_ref, og_ref, u_ref, g_ref):
    xb = x_ref[...].astype(BF16)
    qk_ref[...] = _dot(xb, wqk_ref[...])
    v_ref[...] = _dot(xb, wv_ref[...]).astype(BF16)
    og_ref[...] = _dot(xb, wo_ref[...])
    g_ref[...] = _dot(xb, wg_ref[...])
    xp = _dot(perm_ref[...], xb).astype(BF16)
    u_ref[...] = _dot(xp, wu_ref[...])


def _inproj(x2, perm, wqk, wv, wo, wu, wg):
    n, d = x2.shape
    dm = wv.shape[1]
    ds5 = wu.shape[1]
    rows = lambda w: pl.BlockSpec((ROW_TILE, w), lambda i: (i, 0))
    return pl.pallas_call(
        _inproj_kernel,
        grid=(n // ROW_TILE,),
        in_specs=[rows(d), _resident(perm.shape), _resident(wqk.shape), _resident(wv.shape),
                  _resident(wo.shape), _resident(wu.shape), _resident(wg.shape)],
        out_specs=[rows(2 * dm), rows(dm), rows(dm), rows(ds5), rows(LANES)],
        out_shape=[jax.ShapeDtypeStruct((n, 2 * dm), F32), jax.ShapeDtypeStruct((n, dm), BF16),
                   jax.ShapeDtypeStruct((n, dm), F32), jax.ShapeDtypeStruct((n, ds5), F32),
                   jax.ShapeDtypeStruct((n, LANES), F32)],
        compiler_params=_params("arbitrary"),
        name="inproj",
    )(x2, perm, wqk, wv, wo, wu, wg)


def _cumsum_rows(x):
    n = x.shape[0]
    row = lax.broadcasted_iota(jnp.int32, x.shape, 0)
    s = 1
    while s < n:
        x = x + jnp.where(row >= s, pltpu.roll(x, s, 0), 0.0)
        s *= 2
    return x


def _transpose_gate_cols(x):
    sel = (lax.broadcasted_iota(jnp.int32, (SUBLANES, LANES), 0)
           == lax.broadcasted_iota(jnp.int32, (SUBLANES, LANES), 1)).astype(BF16)
    hi = x.astype(BF16)
    r1 = x - hi.astype(F32)
    mid = r1.astype(BF16)
    lo = (r1 - mid.astype(F32)).astype(BF16)
    return _dot_nt(sel, hi) + _dot_nt(sel, mid) + _dot_nt(sel, lo)


def _mlstm_kernel(qk_ref, v_ref, og_ref, g_ref, c0_ref, n0_ref, m0_ref, tail0_ref,
                  cw_ref, cb_ref, bg_ref, nw_ref,
                  h_ref, c_out, n_out, m_out,
                  ext_ref, c_ref, n_ref, m_ref, *, chunk, head_dim):
    ci = pl.program_id(1)
    dm = M_HEADS * head_dim
    L = chunk

    @pl.when(ci == 0)
    def _():
        c_ref[...] = c0_ref[0]
        n_ref[...] = n0_ref[0]
        m_ref[...] = m0_ref[0]
        ext_ref[0:SUBLANES, :] = tail0_ref[0]

    ext_ref[SUBLANES:SUBLANES + L, :] = qk_ref[...]
    acc = cb_ref[...]
    for j in range(CONV_W):
        acc = acc + ext_ref[pl.ds(SUBLANES - (CONV_W - 1) + j, L), :] * cw_ref[j:j + 1, :]
    tail = ext_ref[L:L + SUBLANES, :]
    ext_ref[0:SUBLANES, :] = tail
    qk = acc * jax.nn.sigmoid(acc)

    gates = g_ref[...] + bg_ref[...]
    fpre = pltpu.roll(gates, LANES - M_HEADS, 1)
    logf = jnp.minimum(fpre, 0.0) - jnp.log1p(jnp.exp(-jnp.abs(fpre)))
    bcum = _cumsum_rows(logf)
    rrow = _transpose_gate_cols(gates - bcum)
    tri = (lax.broadcasted_iota(jnp.int32, (L, L), 0) >= lax.broadcasted_iota(jnp.int32, (L, L), 1))

    for h in range(M_HEADS):
        lo, hi = h * head_dim, (h + 1) * head_dim
        q = qk[:, lo:hi]
        k = qk[:, dm + lo:dm + hi] * (head_dim ** -0.5)
        v = v_ref[:, lo:hi]
        qb = q.astype(BF16)
        kb = k.astype(BF16)
        b_col = bcum[:, h:h + 1]
        ig_col = gates[:, h:h + 1]
        m_prev = m_ref[h:h + 1, 0:1]
        c_prev = c_ref[h]
        n_prev = n_ref[h:h + 1, :]

        dmat = jnp.where(tri, b_col + rrow[h:h + 1, :], NEG)
        inter = b_col + m_prev
        m_t = jnp.maximum(inter, jnp.max(dmat, axis=-1, keepdims=True))
        w_intra = jnp.exp(dmat - m_t)
        w_inter = jnp.exp(inter - m_t)
        s = _dot_nt(qb, kb) * w_intra
        num = w_inter * _dot(qb, c_prev.astype(BF16)) + _dot(s.astype(BF16), v)
        den = w_inter * jnp.sum(q * n_prev, axis=-1, keepdims=True) + jnp.sum(s, axis=-1, keepdims=True)
        hh = num / jnp.maximum(jnp.abs(den), jnp.exp(-m_t))

        m_new = m_t[L - 1:L, :]
        b_last = b_col[L - 1:L, :]
        w_s = jnp.exp(b_last - b_col + ig_col - m_new)
        decay = jnp.exp(b_last + m_prev - m_new)
        kw = k * w_s
        c_ref[h] = decay * c_prev + _dot(kw.T.astype(BF16), v)
        n_ref[h:h + 1, :] = decay * n_prev + jnp.sum(kw, axis=0, keepdims=True)
        m_ref[h:h + 1, :] = jnp.broadcast_to(m_new, (1, LANES))

        mu = jnp.mean(hh, axis=-1, keepdims=True)
        hc = hh - mu
        var = jnp.mean(hc * hc, axis=-1, keepdims=True)
        hn = hc * lax.rsqrt(var + LN_EPS) * nw_ref[:, lo:hi]
        h_ref[:, lo:hi] = (jax.nn.sigmoid(og_ref[:, lo:hi]) * hn).astype(BF16)

    @pl.when(ci == pl.num_programs(1) - 1)
    def _():
        c_out[0] = c_ref[...]
        n_out[0] = n_ref[...]
        m_out[0] = m_ref[...]


def _mlstm(qk, v, og, g, c0, n0p, m0p, tail0, conv_w, conv_b, bg, norm_w, *, batch, seq, chunk):
    n, dm = v.shape
    head_dim = dm // M_HEADS
    nc = seq // chunk
    rows = lambda w: pl.BlockSpec((chunk, w), lambda b, c: (b * nc + c, 0))
    per_b = lambda *s: pl.BlockSpec((1,) + s, lambda b, c: (b,) + (0,) * len(s))
    const = lambda a: pl.BlockSpec(a.shape, lambda b, c: (0,) * a.ndim)
    return pl.pallas_call(
        functools.partial(_mlstm_kernel, chunk=chunk, head_dim=head_dim),
        grid=(batch, nc),
        in_specs=[rows(2 * dm), rows(dm), rows(dm), rows(LANES),
                  per_b(M_HEADS, head_dim, head_dim), per_b(SUBLANES, head_dim), per_b(SUBLANES, LANES),
                  per_b(SUBLANES, 2 * dm),
                  const(conv_w), const(conv_b), const(bg), const(norm_w)],
        out_specs=[rows(dm), per_b(M_HEADS, head_dim, head_dim), per_b(SUBLANES, head_dim),
                   per_b(SUBLANES, LANES)],
        out_shape=[jax.ShapeDtypeStruct((n, dm), BF16),
                   jax.ShapeDtypeStruct((batch, M_HEADS, head_dim, head_dim), F32),
                   jax.ShapeDtypeStruct((batch, SUBLANES, head_dim), F32),
                   jax.ShapeDtypeStruct((batch, SUBLANES, LANES), F32)],
        scratch_shapes=[pltpu.VMEM((chunk + SUBLANES, 2 * dm), F32),
                        pltpu.VMEM((M_HEADS, head_dim, head_dim), F32),
                        pltpu.VMEM((SUBLANES, head_dim), F32),
                        pltpu.VMEM((SUBLANES, LANES), F32)],
        compiler_params=_params("arbitrary", "arbitrary"),
        name="mlstm",
    )(qk, v, og, g, c0, n0p, m0p, tail0, conv_w, conv_b, bg, norm_w)


def _s5_kernel(u_ref, sre0_ref, sim0_ref, wb_ref, are_ref, aim_ref, asre_ref, asim_ref,
               wcr_ref, wci_ref, dskip_ref, wglu_ref, bglu_ref, permt_ref,
               hs_ref, sre_out, sim_out,
               xr_ref, xi_ref, hr_ref, hi_ref, *, seg):
    bi = pl.program_id(1)
    n_state = xr_ref.shape[1]
    n_tiles = wb_ref.shape[0]

    @pl.when(bi == 0)
    def _():
        hr_ref[...] = sre0_ref[0]
        hi_ref[...] = sim0_ref[0]

    ub = u_ref[...].astype(BF16)
    tiles_per_slab = LANES // (2 * S5_GROUP)
    for j in range(n_tiles):
        cs = (j // tiles_per_slab) * LANES
        xj = _dot(ub[:, cs:cs + LANES], wb_ref[j])
        xr_ref[:, j * LANES:(j + 1) * LANES] = xj[:, :LANES]
        xi_ref[:, j * LANES:(j + 1) * LANES] = xj[:, LANES:]

    row = lax.broadcasted_iota(jnp.int32, (N_STREAMS, SCAN_LANES), 0)
    for sl in range(n_state // SCAN_LANES):
        cols = slice(sl * SCAN_LANES, (sl + 1) * SCAN_LANES)
        ar = jnp.broadcast_to(are_ref[:, cols], (N_STREAMS, SCAN_LANES))
        ai = jnp.broadcast_to(aim_ref[:, cols], (N_STREAMS, SCAN_LANES))

        def scan_step(i, carry):
            sr, si = carry
            r0 = pl.multiple_of(i * N_STREAMS, N_STREAMS)
            nr = ar * sr - ai * si + xr_ref[pl.ds(r0, N_STREAMS), cols]
            ni = ar * si + ai * sr + xi_ref[pl.ds(r0, N_STREAMS), cols]
            xr_ref[pl.ds(r0, N_STREAMS), cols] = nr
            xi_ref[pl.ds(r0, N_STREAMS), cols] = ni
            return nr, ni

        zero = jnp.zeros((N_STREAMS, SCAN_LANES), F32)
        er, ei = lax.fori_loop(0, seg, scan_step, (zero, zero))

        asr = asre_ref[:, cols]
        asi = asim_ref[:, cols]
        cr = hr_ref[:, cols]
        ci = hi_ref[:, cols]
        cmr = jnp.zeros((N_STREAMS, SCAN_LANES), F32)
        cmi = jnp.zeros((N_STREAMS, SCAN_LANES), F32)
        for k in range(N_STREAMS):
            cmr = jnp.where(row == k, cr, cmr)
            cmi = jnp.where(row == k, ci, cmi)
            nr = asr * cr - asi * ci + er[k:k + 1, :]
            ni = asr * ci + asi * cr + ei[k:k + 1, :]
            cr, ci = nr, ni
        hr_ref[:, cols] = cr
        hi_ref[:, cols] = ci

        def fix_step(i, carry):
            dr, di = carry
            r0 = pl.multiple_of(i * N_STREAMS, N_STREAMS)
            nr = ar * dr - ai * di
            ni = ar * di + ai * dr
            xr_ref[pl.ds(r0, N_STREAMS), cols] += nr
            xi_ref[pl.ds(r0, N_STREAMS), cols] += ni
            return nr, ni

        lax.fori_loop(0, seg, fix_step, (cmr, cmi))

    n_out_tiles = wcr_ref.shape[0]
    kw = n_state // n_out_tiles
    ys = []
    for qt in range(n_out_tiles):
        hr = xr_ref[:, qt * kw:(qt + 1) * kw].astype(BF16)
        hi = xi_ref[:, qt * kw:(qt + 1) * kw].astype(BF16)
        ys.append(_dot(hr, wcr_ref[qt]) + _dot(hi, wci_ref[qt]))
    y = jnp.concatenate(ys, axis=-1) + dskip_ref[...] * u_ref[...]
    gl = 0.5 * y * (1.0 + jnp.tanh(math.sqrt(2.0 / math.pi) * (y + 0.044715 * (y * y * y))))
    z = _dot(gl.astype(BF16), wglu_ref[...]) + bglu_ref[...]
    out = gl * jax.nn.sigmoid(z)
    hs_ref[...] = _dot(permt_ref[...], out.astype(BF16)).astype(BF16)

    @pl.when(bi == pl.num_programs(1) - 1)
    def _():
        sre_out[0] = hr_ref[...]
        sim_out[0] = hi_ref[...]


def _s5(u, sre0, sim0, wb, a_re, a_im, as_re, as_im, wcr, wci, dskip, wglu, bglu, permt,
        *, batch, seq, block):
    n, ds5 = u.shape
    n_state = a_re.shape[1]
    nb = seq // block
    rows = lambda w: pl.BlockSpec((block, w), lambda b, c: (b * nb + c, 0))
    per_b = pl.BlockSpec((1, 1, n_state), lambda b, c: (b, 0, 0))
    const = lambda a: pl.BlockSpec(a.shape, lambda b, c: (0,) * a.ndim)
    return pl.pallas_call(
        functools.partial(_s5_kernel, seg=block // N_STREAMS),
        grid=(batch, nb),
        in_specs=[rows(ds5), per_b, per_b, const(wb), const(a_re), const(a_im), const(as_re),
                  const(as_im), const(wcr), const(wci), const(dskip), const(wglu), const(bglu),
                  const(permt)],
        out_specs=[rows(ds5), per_b, per_b],
        out_shape=[jax.ShapeDtypeStruct((n, ds5), BF16),
                   jax.ShapeDtypeStruct((batch, 1, n_state), F32),
                   jax.ShapeDtypeStruct((batch, 1, n_state), F32)],
        scratch_shapes=[pltpu.VMEM((block, n_state), F32), pltpu.VMEM((block, n_state), F32),
                        pltpu.VMEM((1, n_state), F32), pltpu.VMEM((1, n_state), F32)],
        compiler_params=_params("arbitrary", "arbitrary"),
        name="s5",
    )(u, sre0, sim0, wb, a_re, a_im, as_re, as_im, wcr, wci, dskip, wglu, bglu, permt)


def _layer_norm(x, g, b):
    mu = jnp.mean(x, axis=-1, keepdims=True)
    xc = x - mu
    var = jnp.mean(xc * xc, axis=-1, keepdims=True)
    return xc * lax.rsqrt(var + LN_EPS) * g + b


def _first_max(cols):
    best = cols[0]
    for c in cols[1:]:
        best = jnp.maximum(best, c)
    flags = []
    taken = None
    for c in cols:
        hit = c == best
        if taken is not None:
            hit = jnp.logical_and(hit, jnp.logical_not(taken))
            taken = jnp.logical_or(taken, hit)
        else:
            taken = hit
        flags.append(hit)
    return best, flags


def _outproj_kernel(x_ref, hm_ref, hs_ref, woa_ref, wob_ref, g1_ref, b1_ref, wrh_ref, wrl_ref, br_ref,
                    x1_ref, route_ref, *, alpha):
    mix = _dot(hm_ref[...], woa_ref[...]) + _dot(hs_ref[...], wob_ref[...])
    x1 = _layer_norm(alpha * x_ref[...] + mix, g1_ref[...], b1_ref[...])
    x1_ref[...] = x1

    xh = x1.astype(BF16)
    xl = (x1 - xh.astype(F32)).astype(BF16)
    logits = (_dot(xh, wrh_ref[...]) + _dot(xl, wrh_ref[...]) + _dot(xh, wrl_ref[...])) + br_ref[...]

    l1 = [logits[:, g:g + 1] for g in range(N_EXPERT_GROUPS)]
    m1, gsel = _first_max(l1)
    denom = l1[0] * 0.0
    for c in l1:
        denom = denom + jnp.exp(c - m1)
    p_g = 1.0 / denom
    gid = l1[0] * 0.0
    l2 = []
    for e in range(EXPERTS_PER_GROUP):
        acc = l1[0] * 0.0
        for g in range(N_EXPERT_GROUPS):
            c0 = N_EXPERT_GROUPS + g * EXPERTS_PER_GROUP + e
            acc = acc + jnp.where(gsel[g], logits[:, c0:c0 + 1], 0.0)
        l2.append(acc)
    for g in range(N_EXPERT_GROUPS):
        gid = gid + jnp.where(gsel[g], float(g), 0.0)

    v1, f1 = _first_max(l2)
    masked = [jnp.where(f1[e], -jnp.inf, l2[e]) for e in range(EXPERTS_PER_GROUP)]
    v2, f2 = _first_max(masked)
    e21 = jnp.exp(v2 - v1)
    w_first = 1.0 / (1.0 + e21)
    w_second = e21 / (1.0 + e21)
    pair_id = l1[0] * 0.0
    gate_a = l1[0] * 0.0
    gate_b = l1[0] * 0.0
    pid = 0
    for a in range(EXPERTS_PER_GROUP):
        for b in range(a + 1, EXPERTS_PER_GROUP):
            ab = jnp.logical_and(f1[a], f2[b])
            ba = jnp.logical_and(f1[b], f2[a])
            pair_id = pair_id + jnp.where(jnp.logical_or(ab, ba), float(pid), 0.0)
            gate_a = gate_a + jnp.where(ab, w_first, 0.0) + jnp.where(ba, w_second, 0.0)
            gate_b = gate_b + jnp.where(ab, w_second, 0.0) + jnp.where(ba, w_first, 0.0)
            pid += 1
    bin_id = gid * float(N_PAIRS) + pair_id
    lane = lax.broadcasted_iota(jnp.int32, route_ref.shape, 1)
    route_ref[...] = jnp.where(lane == 0, bin_id,
                               jnp.where(lane == 1, p_g * gate_a, jnp.where(lane == 2, p_g * gate_b, 0.0)))


def _outproj(x2, hm, hs, woa, wob, g1, b1, wrh, wrl, br, *, alpha):
    n, d = x2.shape
    dm = hm.shape[1]
    rows = lambda w: pl.BlockSpec((ROW_TILE, w), lambda i: (i, 0))
    return pl.pallas_call(
        functools.partial(_outproj_kernel, alpha=alpha),
        grid=(n // ROW_TILE,),
        in_specs=[rows(d), rows(dm), rows(hs.shape[1]), _resident(woa.shape), _resident(wob.shape),
                  _resident(g1.shape), _resident(b1.shape), _resident(wrh.shape), _resident(wrl.shape),
                  _resident(br.shape)],
        out_specs=[rows(d), rows(LANES)],
        out_shape=[jax.ShapeDtypeStruct((n, d), F32), jax.ShapeDtypeStruct((n, LANES), F32)],
        compiler_params=_params("arbitrary"),
        name="outproj",
    )(x2, hm, hs, woa, wob, g1, b1, wrh, wrl, br)


def _moe_kernel(ea_ref, eb_ref, nv_ref, npr_ref,
                idx_ref, idxn_ref, gates_ref, wga_ref, wua_ref, wda_ref, wgb_ref, wub_ref, wdb_ref,
                g2_ref, b2_ref, x1p_hbm, x1s_hbm,
                yp_hbm, ys_hbm,
                xbuf, obuf, gsem, ssem, *, alpha, n_prompt):
    t = pl.program_id(0)
    nt = pl.num_programs(0)
    slot = t % 2

    def gather_copy(tok, r, s, from_prompt):
        src = x1p_hbm.at[pl.ds(tok, 1)] if from_prompt else x1s_hbm.at[pl.ds(tok - n_prompt, 1)]
        return pltpu.make_async_copy(src, xbuf.at[s, pl.ds(r, 1)], gsem.at[s])

    def scatter_copy(tok, r, s, to_prompt):
        dst = yp_hbm.at[pl.ds(tok, 1)] if to_prompt else ys_hbm.at[pl.ds(tok - n_prompt, 1)]
        return pltpu.make_async_copy(obuf.at[s, pl.ds(r, 1)], dst, ssem.at[s])

    def start_gather(tile, s, ids):
        def body_p(r, c):
            gather_copy(ids[0, 0, r], r, s, True).start()
            return c

        def body_s(r, c):
            gather_copy(ids[0, 0, r], r, s, False).start()
            return c

        lax.fori_loop(0, npr_ref[tile], body_p, 0)
        lax.fori_loop(npr_ref[tile], nv_ref[tile], body_s, 0)

    def wait_rows(copy, count):
        def body(r, c):
            copy.wait()
            return c

        lax.fori_loop(0, count, body, 0)

    @pl.when(t == 0)
    def _():
        xbuf[...] = jnp.zeros_like(xbuf)
        start_gather(0, 0, idx_ref)

    @pl.when(t + 1 < nt)
    def _():
        start_gather(t + 1, 1 - slot, idxn_ref)

    wait_rows(gather_copy(0, 0, slot, True), nv_ref[t])

    @pl.when(t >= 2)
    def _():
        wait_rows(scatter_copy(0, 0, slot, True), nv_ref[t - 2])

    @pl.when(nv_ref[t] > 0)
    def _():
        x1 = xbuf[slot]
        xb = x1.astype(BF16)
        ha = _dot(xb, wga_ref[0])
        ha = (ha * jax.nn.sigmoid(ha)) * _dot(xb, wua_ref[0])
        ya = _dot(ha.astype(BF16), wda_ref[0])
        hb = _dot(xb, wgb_ref[0])
        hb = (hb * jax.nn.sigmoid(hb)) * _dot(xb, wub_ref[0])
        yb = _dot(hb.astype(BF16), wdb_ref[0])
        moe = gates_ref[:, 0:1] * ya + gates_ref[:, 1:2] * yb
        obuf[slot] = _layer_norm(alpha * x1 + moe, g2_ref[...], b2_ref[...])

    def scatter_p(r, c):
        scatter_copy(idx_ref[0, 0, r], r, slot, True).start()
        return c

    def scatter_s(r, c):
        scatter_copy(idx_ref[0, 0, r], r, slot, False).start()
        return c

    lax.fori_loop(0, npr_ref[t], scatter_p, 0)
    lax.fori_loop(npr_ref[t], nv_ref[t], scatter_s, 0)

    @pl.when(t == nt - 1)
    def _():
        wait_rows(scatter_copy(0, 0, slot, True), nv_ref[t])

        @pl.when(t >= 1)
        def _():
            wait_rows(scatter_copy(0, 0, 1 - slot, True), nv_ref[t - 1])


def _moe(x1p, x1s, route_p, route_s, wg, wu, wd, g2, b2, *, alpha):
    n_p, d = x1p.shape
    n_s = x1s.shape[0]
    n = n_p + n_s
    tm = MOE_TILE
    nt = n // tm + N_BINS if n % tm == 0 else (n + N_BINS * (tm - 1)) // tm + 1
    route = jnp.concatenate([route_p[:, :SUBLANES], route_s[:, :SUBLANES]], axis=0)
    bins = route[:, 0].astype(jnp.int32)

    order = jnp.argsort(bins, stable=True).astype(jnp.int32)
    onehot = bins[:, None] == jnp.arange(N_BINS, dtype=jnp.int32)[None, :]
    cnt = jnp.sum(onehot, axis=0, dtype=jnp.int32)
    cnt_p = jnp.sum(onehot[:n_p], axis=0, dtype=jnp.int32)
    off = jnp.cumsum(cnt) - cnt
    tiles_b = (cnt + tm - 1) // tm
    tile_end = jnp.cumsum(tiles_b)
    tile_start = tile_end - tiles_b
    tid = jnp.arange(nt, dtype=jnp.int32)
    used = tid < tile_end[-1]
    tbin = jnp.minimum(jnp.searchsorted(tile_end, tid, side="right").astype(jnp.int32), N_BINS - 1)
    last_bin = jnp.max(jnp.where(cnt > 0, jnp.arange(N_BINS, dtype=jnp.int32), 0))
    tbin = jnp.where(used, tbin, last_bin)
    kk = tid - tile_start[tbin]
    nvalid = jnp.where(used, jnp.clip(cnt[tbin] - kk * tm, 0, tm), 0).astype(jnp.int32)
    nprompt = jnp.where(used, jnp.clip(cnt_p[tbin] - kk * tm, 0, nvalid), 0).astype(jnp.int32)
    pos = off[tbin][:, None] + kk[:, None] * tm + jnp.arange(tm, dtype=jnp.int32)[None, :]
    idx = order[jnp.clip(pos, 0, n - 1)]
    gates = route[idx.reshape(-1), 1:3]
    pair_a = jnp.array([0, 0, 0, 1, 1, 2], jnp.int32)
    pair_b = jnp.array([1, 2, 3, 2, 3, 3], jnp.int32)
    ea = (tbin // N_PAIRS) * EXPERTS_PER_GROUP + pair_a[tbin % N_PAIRS]
    eb = (tbin // N_PAIRS) * EXPERTS_PER_GROUP + pair_b[tbin % N_PAIRS]
    idx3 = idx.reshape(nt, 1, tm)

    dff = wg.shape[2]
    wspec_in = lambda sel: pl.BlockSpec((1, d, dff), lambda t, ea, eb, nv, npr: (sel(ea, eb)[t], 0, 0))
    wspec_out = lambda sel: pl.BlockSpec((1, dff, d), lambda t, ea, eb, nv, npr: (sel(ea, eb)[t], 0, 0))
    first = lambda a, b: a
    second = lambda a, b: b
    const2 = lambda a: pl.BlockSpec(a.shape, lambda t, *_: (0,) * a.ndim)
    smem_rows = lambda fn: pl.BlockSpec((1, 1, tm), fn, memory_space=pltpu.SMEM)
    grid_spec = pltpu.PrefetchScalarGridSpec(
        num_scalar_prefetch=4,
        grid=(nt,),
        in_specs=[smem_rows(lambda t, *_: (t, 0, 0)),
                  smem_rows(lambda t, *_: (jnp.minimum(t + 1, nt - 1), 0, 0)),
                  pl.BlockSpec((tm, 2), lambda t, *_: (t, 0)),
                  wspec_in(first), wspec_in(first), wspec_out(first),
                  wspec_in(second), wspec_in(second), wspec_out(second),
                  const2(g2), const2(b2),
                  pl.BlockSpec(memory_space=pl.ANY), pl.BlockSpec(memory_space=pl.ANY)],
        out_specs=[pl.BlockSpec(memory_space=pl.ANY), pl.BlockSpec(memory_space=pl.ANY)],
        scratch_shapes=[pltpu.VMEM((2, tm, d), F32), pltpu.VMEM((2, tm, d), F32),
                        pltpu.SemaphoreType.DMA((2,)), pltpu.SemaphoreType.DMA((2,))],
    )
    return pl.pallas_call(
        functools.partial(_moe_kernel, alpha=alpha, n_prompt=n_p),
        grid_spec=grid_spec,
        out_shape=[jax.ShapeDtypeStruct((n_p, d), F32), jax.ShapeDtypeStruct((n_s, d), F32)],
        compiler_params=_params("arbitrary"),
        name="moe",
    )(ea, eb, nvalid, nprompt, idx3, idx3, gates, wg, wu, wd, wg, wu, wd, g2, b2, x1p, x1s)


def _s5_tables(lam_re, lam_im, log_dt, b_re, b_im, c_re, c_im, seg_lens):
    n_groups, n_state = lam_re.shape
    dt = jnp.exp(log_dt.astype(F32))[:, None]
    lam_re = lam_re.astype(F32)
    lam_im = lam_im.astype(F32)
    mag = jnp.exp(lam_re * dt)
    a_re = mag * jnp.cos(lam_im * dt)
    a_im = mag * jnp.sin(lam_im * dt)
    e_re = a_re - 1.0
    e_im = a_im
    lam_sq = lam_re * lam_re + lam_im * lam_im
    coef_re = (e_re * lam_re + e_im * lam_im) / lam_sq
    coef_im = (e_im * lam_re - e_re * lam_im) / lam_sq
    b_re = b_re.astype(F32)
    b_im = b_im.astype(F32)
    bb_re = coef_re[..., None] * b_re - coef_im[..., None] * b_im
    bb_im = coef_re[..., None] * b_im + coef_im[..., None] * b_re

    gpt = LANES // n_state
    n_tiles = n_groups // gpt
    gps = LANES // S5_GROUP
    g_in_slab = (jnp.arange(n_groups) % gps)
    rows_onehot = jax.nn.one_hot(g_in_slab, gps, dtype=F32)
    w_re = jnp.einsum("gs,gpc->gscp", rows_onehot, bb_re).reshape(n_groups, LANES, n_state)
    w_im = jnp.einsum("gs,gpc->gscp", rows_onehot, bb_im).reshape(n_groups, LANES, n_state)
    w_re = w_re.reshape(n_tiles, gpt, LANES, n_state).transpose(0, 2, 1, 3).reshape(n_tiles, LANES, LANES)
    w_im = w_im.reshape(n_tiles, gpt, LANES, n_state).transpose(0, 2, 1, 3).reshape(n_tiles, LANES, LANES)
    wb = jnp.concatenate([w_re, w_im], axis=-1).astype(BF16)

    n_flat = n_groups * n_state
    out_w = 2 * LANES
    g_per_out = out_w // S5_GROUP
    n_out_tiles = n_groups // g_per_out
    oh = jax.nn.one_hot(jnp.arange(n_groups) % g_per_out, g_per_out, dtype=F32)
    wcr = jnp.einsum("gs,gcp->gpsc", oh, c_re.astype(F32)).reshape(n_out_tiles, g_per_out * n_state, out_w)
    wci = jnp.einsum("gs,gcp->gpsc", oh, -c_im.astype(F32)).reshape(n_out_tiles, g_per_out * n_state, out_w)

    def powers(k):
        return ((mag ** k) * jnp.cos(lam_im * dt * k)).reshape(1, n_flat), \
               ((mag ** k) * jnp.sin(lam_im * dt * k)).reshape(1, n_flat)

    return (wb, a_re.reshape(1, n_flat), a_im.reshape(1, n_flat), [powers(float(k)) for k in seg_lens],
            wcr.astype(BF16), wci.astype(BF16))


def _layer(x_p, x_s, st, lp):
    (w_in, b_gates, conv_w, conv_b, norm_w, lam_re, lam_im, log_dt, b_re, b_im, c_re, c_im,
     d_skip, w_glu, b_glu, w_out, ln1_g, ln1_b, w_r1, b_r1, w_r2, b_r2, w_gate, w_up, w_down,
     ln2_g, ln2_b, alpha) = lp
    state_c, state_n, state_m, state_conv, state_re, state_im = st
    bp, tp, d = x_p.shape
    bs, ts, _ = x_s.shape
    dm = norm_w.shape[0]
    head_dim = dm // M_HEADS
    ds5 = d_skip.shape[0]
    n_groups, n_state = lam_re.shape
    n_flat = n_groups * n_state

    o0 = 2 * dm
    o1 = o0 + 2 * dm
    wqk = w_in[:, :o0].astype(BF16)
    wv = w_in[:, o0:o0 + dm].astype(BF16)
    wo = w_in[:, o0 + dm:o1].astype(BF16)
    wg = jnp.pad(w_in[:, o1:o1 + 2 * M_HEADS], ((0, 0), (0, LANES - 2 * M_HEADS))).astype(BF16)
    wu = w_in[:, o1 + 2 * M_HEADS:].astype(BF16)
    bg = jnp.pad(b_gates.astype(F32), (0, LANES - 2 * M_HEADS)).reshape(1, LANES)
    woa = w_out[:dm].astype(BF16)
    wob = w_out[dm:].astype(BF16)
    wr = jnp.concatenate([w_r1, jnp.transpose(w_r2, (1, 0, 2)).reshape(d, -1)], axis=1).astype(F32)
    wr = jnp.pad(wr, ((0, 0), (0, LANES - wr.shape[1])))
    wrh = wr.astype(BF16)
    wrl = (wr - wrh.astype(F32)).astype(BF16)
    br = jnp.pad(jnp.concatenate([b_r1, b_r2.reshape(-1)]).astype(F32), (0, LANES - N_EXPERT_GROUPS
                                                                      - N_EXPERT_GROUPS * EXPERTS_PER_GROUP))
    br = br.reshape(1, LANES)

    blk_p = min(tp, ROW_TILE)
    blk_s = min(ts, ROW_TILE)
    seg_p = blk_p // N_STREAMS
    seg_s = blk_s // N_STREAMS
    wb, a_re, a_im, (as_p, as_s), wcr, wci = _s5_tables(lam_re, lam_im, log_dt, b_re, b_im, c_re, c_im,
                                                         (seg_p, seg_s))

    outs = []
    for x, blk, seg, a_seg, zero_state in ((x_p, blk_p, seg_p, as_p, True), (x_s, blk_s, seg_s, as_s, False)):
        b, t, _ = x.shape
        x2 = x.reshape(b * t, d)
        perm_np = _stream_perm(blk, seg)
        perm = jnp.asarray(perm_np, BF16)
        permt = jnp.asarray(perm_np[:blk, :blk].T, BF16)
        qk, v, og, u, g = _inproj(x2, perm, wqk, wv, wo, wu, wg)

        if zero_state:
            c0 = jnp.zeros((b, M_HEADS, head_dim, head_dim), F32)
            n0 = jnp.zeros((b, M_HEADS, head_dim), F32)
            m0 = jnp.zeros((b, M_HEADS), F32)
            conv0 = jnp.zeros((b, CONV_W - 1, 2 * dm), F32)
            re0 = jnp.zeros((b, n_groups, n_state), F32)
            im0 = jnp.zeros((b, n_groups, n_state), F32)
        else:
            c0, n0, m0, conv0, re0, im0 = (state_c.astype(F32), state_n.astype(F32), state_m.astype(F32),
                                           state_conv.astype(F32), state_re.astype(F32), state_im.astype(F32))
        n0p = jnp.pad(n0, ((0, 0), (0, SUBLANES - M_HEADS), (0, 0)))
        m0p = jnp.broadcast_to(jnp.pad(m0, ((0, 0), (0, SUBLANES - M_HEADS)))[:, :, None], (b, SUBLANES, LANES))
        tail0 = jnp.pad(conv0, ((0, 0), (SUBLANES - (CONV_W - 1), 0), (0, 0)))
        chunk = min(t, LANES)
        hm, c_new, n_new, m_new = _mlstm(qk, v, og, g, c0, n0p, m0p, tail0, conv_w.astype(F32),
                                         conv_b.astype(F32).reshape(1, -1), bg, norm_w.astype(F32).reshape(1, -1),
                                         batch=b, seq=t, chunk=chunk)
        hs, re_new, im_new = _s5(u, re0.reshape(b, 1, n_flat), im0.reshape(b, 1, n_flat), wb, a_re, a_im,
                                 a_seg[0], a_seg[1], wcr, wci, d_skip.astype(F32).reshape(1, -1),
                                 w_glu.astype(BF16), b_glu.astype(F32).reshape(1, -1), permt,
                                 batch=b, seq=t, block=blk)
        x1, route = _outproj(x2, hm, hs, woa, wob, ln1_g.astype(F32).reshape(1, -1),
                             ln1_b.astype(F32).reshape(1, -1), wrh, wrl, br, alpha=alpha)
        conv_new = jnp.concatenate([conv0, qk.reshape(b, t, 2 * dm)], axis=1)[:, -(CONV_W - 1):]
        states = (c_new, n_new[:, :M_HEADS], m_new[:, :M_HEADS, 0], conv_new,
                  re_new.reshape(b, n_groups, n_state), im_new.reshape(b, n_groups, n_state))
        outs.append((x1, route, states))

    (x1p, route_p, st_p), (x1s, route_s, st_s) = outs
    yp, ys = _moe(x1p, x1s, route_p, route_s, w_gate.astype(BF16), w_up.astype(BF16), w_down.astype(BF16),
                  ln2_g.astype(F32).reshape(1, -1), ln2_b.astype(F32).reshape(1, -1), alpha=alpha)
    return yp.reshape(bp, tp, d), ys.reshape(bs, ts, d), st_p, st_s


def kernel(x_prompt, x_sample, state_mlstm_C, state_mlstm_n, state_mlstm_m, state_conv, state_s5_re, state_s5_im, w_in, b_gates, conv_w, conv_b, mlstm_norm_w, s5_lam_re, s5_lam_im, s5_log_dt, s5_b_re, s5_b_im, s5_c_re, s5_c_im, s5_d, w_glu, b_glu, w_out, ln1_g, ln1_b, w_r1, b_r1, w_r2, b_r2, w_gate, w_up, w_down, ln2_g, ln2_b):
    depth = w_in.shape[0]
    alpha = (2 * depth) ** 0.25
    yp, ys = x_prompt, x_sample
    sts_p, sts_s = [], []
    for l in range(depth):
        lp = (w_in[l], b_gates[l], conv_w[l], conv_b[l], mlstm_norm_w[l], s5_lam_re[l], s5_lam_im[l],
              s5_log_dt[l], s5_b_re[l], s5_b_im[l], s5_c_re[l], s5_c_im[l], s5_d[l], w_glu[l], b_glu[l],
              w_out[l], ln1_g[l], ln1_b[l], w_r1[l], b_r1[l], w_r2[l], b_r2[l], w_gate[l], w_up[l],
              w_down[l], ln2_g[l], ln2_b[l], alpha)
        st = (state_mlstm_C[l], state_mlstm_n[l], state_mlstm_m[l], state_conv[l], state_s5_re[l],
              state_s5_im[l])
        yp, ys, sp, ss = _layer(yp, ys, st, lp)
        sts_p.append(sp)
        sts_s.append(ss)
    stack = lambda sts, i: jnp.stack([s[i] for s in sts])
    return (yp, ys) + tuple(stack(sts_p, i) for i in range(6)) + tuple(stack(sts_s, i) for i in range(6))
```

```python
import functools
import math

import jax
import jax.numpy as jnp
import numpy as np
from jax import lax
from jax.experimental import pallas as pl
from jax.experimental.pallas import tpu as pltpu

F32 = jnp.float32
BF16 = jnp.bfloat16

LANES = 128
SUBLANES = 8
VMEM_LIMIT = 56 * 1024 * 1024

M_HEADS = 4
CONV_W = 4
S5_GROUP = 16
S5_STATE = 64
N_EXPERT_GROUPS = 4
EXPERTS_PER_GROUP = 4
N_PAIRS = 6
N_BINS = N_EXPERT_GROUPS * N_PAIRS
LN_EPS = 1e-5
NEG = -1e30

ROW_TILE = 256
MOE_TILE = 256
ISSUE_UNROLL = 8
N_STREAMS = SUBLANES
SCAN_LANES = 512


def _dot(a, b):
    return jnp.dot(a, b, preferred_element_type=F32)


def _dot_nt(a, b):
    return lax.dot_general(a, b, (((1,), (1,)), ((), ())), preferred_element_type=F32)


def _resident(shape):
    nd = len(shape)
    return pl.BlockSpec(shape, lambda *_: (0,) * nd, pipeline_mode=pl.Buffered(1))


def _params(*sem):
    return pltpu.CompilerParams(dimension_semantics=sem, vmem_limit_bytes=VMEM_LIMIT)


def _stream_perm(block, seg):
    n_seg = block // seg
    r = np.arange(block)
    src = (r % n_seg) * seg + r // n_seg
    p = np.zeros((block, block), np.float32)
    p[r, src] = 1.0
    reps = ROW_TILE // block
    return np.kron(np.eye(reps, dtype=np.float32), p)


def _inproj_kernel(x_ref, perm_ref, wqk_ref, wv_ref, wo_ref, wu_ref, wg_ref,
                   qk_ref, v_ref, og_ref, u_ref, g_ref):
    xb = x_ref[...].astype(BF16)
    qk_ref[...] = _dot(xb, wqk_ref[...])
    v_ref[...] = _dot(xb, wv_ref[...]).astype(BF16)
    og_ref[...] = _dot(xb, wo_ref[...])
    g_ref[...] = _dot(xb, wg_ref[...])
    xp = _dot(perm_ref[...], xb).astype(BF16)
    u_ref[...] = _dot(xp, wu_ref[...])


def _inproj(x2, perm, wqk, wv, wo, wu, wg):
    n, d = x2.shape
    dm = wv.shape[1]
    ds5 = wu.shape[1]
    rows = lambda w: pl.BlockSpec((ROW_TILE, w), lambda i: (i, 0))
    return pl.pallas_call(
        _inproj_kernel,
        grid=(n // ROW_TILE,),
        in_specs=[rows(d), _resident(perm.shape), _resident(wqk.shape), _resident(wv.shape),
                  _resident(wo.shape), _resident(wu.shape), _resident(wg.shape)],
        out_specs=[rows(2 * dm), rows(dm), rows(dm), rows(ds5), rows(LANES)],
        out_shape=[jax.ShapeDtypeStruct((n, 2 * dm), F32), jax.ShapeDtypeStruct((n, dm), BF16),
                   jax.ShapeDtypeStruct((n, dm), F32), jax.ShapeDtypeStruct((n, ds5), F32),
                   jax.ShapeDtypeStruct((n, LANES), F32)],
        compiler_params=_params("arbitrary"),
        name="inproj",
    )(x2, perm, wqk, wv, wo, wu, wg)


def _cumsum_rows(x):
    n = x.shape[0]
    row = lax.broadcasted_iota(jnp.int32, x.shape, 0)
    s = 1
    while s < n:
        x = x + jnp.where(row >= s, pltpu.roll(x, s, 0), 0.0)
        s *= 2
    return x


def _transpose_gate_cols(x):
    sel = (lax.broadcasted_iota(jnp.int32, (SUBLANES, LANES), 0)
           == lax.broadcasted_iota(jnp.int32, (SUBLANES, LANES), 1)).astype(BF16)
    hi = x.astype(BF16)
    r1 = x - hi.astype(F32)
    mid = r1.astype(BF16)
    lo = (r1 - mid.astype(F32)).astype(BF16)
    return _dot_nt(sel, hi) + _dot_nt(sel, mid) + _dot_nt(sel, lo)


def _mlstm_kernel(qk_ref, v_ref, og_ref, g_ref, c0_ref, n0_ref, m0_ref, tail0_ref,
                  cw_ref, cb_ref, bg_ref, nw_ref,
                  h_ref, c_out, n_out, m_out,
                  ext_ref, c_ref, n_ref, m_ref, *, chunk, head_dim):
    ci = pl.program_id(1)
    dm = M_HEADS * head_dim
    L = chunk

    @pl.when(ci == 0)
    def _():
        c_ref[...] = c0_ref[0]
        n_ref[...] = n0_ref[0]
        m_ref[...] = m0_ref[0]
        ext_ref[0:SUBLANES, :] = tail0_ref[0]

    ext_ref[SUBLANES:SUBLANES + L, :] = qk_ref[...]
    acc = cb_ref[...]
    for j in range(CONV_W):
        acc = acc + ext_ref[pl.ds(SUBLANES - (CONV_W - 1) + j, L), :] * cw_ref[j:j + 1, :]
    tail = ext_ref[L:L + SUBLANES, :]
    ext_ref[0:SUBLANES, :] = tail
    qk = acc * jax.nn.sigmoid(acc)

    gates = g_ref[...] + bg_ref[...]
    fpre = pltpu.roll(gates, LANES - M_HEADS, 1)
    logf = jnp.minimum(fpre, 0.0) - jnp.log1p(jnp.exp(-jnp.abs(fpre)))
    bcum = _cumsum_rows(logf)
    rrow = _transpose_gate_cols(gates - bcum)
    tri = (lax.broadcasted_iota(jnp.int32, (L, L), 0) >= lax.broadcasted_iota(jnp.int32, (L, L), 1))

    for h in range(M_HEADS):
        lo, hi = h * head_dim, (h + 1) * head_dim
        q = qk[:, lo:hi]
        k = qk[:, dm + lo:dm + hi] * (head_dim ** -0.5)
        v = v_ref[:, lo:hi]
        qb = q.astype(BF16)
        kb = k.astype(BF16)
        b_col = bcum[:, h:h + 1]
        ig_col = gates[:, h:h + 1]
        m_prev = m_ref[h:h + 1, 0:1]
        c_prev = c_ref[h]
        n_prev = n_ref[h:h + 1, :]

        dmat = jnp.where(tri, b_col + rrow[h:h + 1, :], NEG)
        inter = b_col + m_prev
        m_t = jnp.maximum(inter, jnp.max(dmat, axis=-1, keepdims=True))
        w_intra = jnp.exp(dmat - m_t)
        w_inter = jnp.exp(inter - m_t)
        s = _dot_nt(qb, kb) * w_intra
        num = w_inter * _dot(qb, c_prev.astype(BF16)) + _dot(s.astype(BF16), v)
        den = w_inter * jnp.sum(q * n_prev, axis=-1, keepdims=True) + jnp.sum(s, axis=-1, keepdims=True)
        hh = num / jnp.maximum(jnp.abs(den), jnp.exp(-m_t))

        m_new = m_t[L - 1:L, :]
        b_last = b_col[L - 1:L, :]
        w_s = jnp.exp(b_last - b_col + ig_col - m_new)
        decay = jnp.exp(b_last + m_prev - m_new)
        kw = k * w_s
        c_ref[h] = decay * c_prev + _dot(kw.T.astype(BF16), v)
        n_ref[h:h + 1, :] = decay * n_prev + jnp.sum(kw, axis=0, keepdims=True)
        m_ref[h:h + 1, :] = jnp.broadcast_to(m_new, (1, LANES))

        mu = jnp.mean(hh, axis=-1, keepdims=True)
        hc = hh - mu
        var = jnp.mean(hc * hc, axis=-1, keepdims=True)
        hn = hc * lax.rsqrt(var + LN_EPS) * nw_ref[:, lo:hi]
        h_ref[:, lo:hi] = (jax.nn.sigmoid(og_ref[:, lo:hi]) * hn).astype(BF16)

    @pl.when(ci == pl.num_programs(1) - 1)
    def _():
        c_out[0] = c_ref[...]
        n_out[0] = n_ref[...]
        m_out[0] = m_ref[...]


def _mlstm(qk, v, og, g, c0, n0p, m0p, tail0, conv_w, conv_b, bg, norm_w, *, batch, seq, chunk):
    n, dm = v.shape
    head_dim = dm // M_HEADS
    nc = seq // chunk
    rows = lambda w: pl.BlockSpec((chunk, w), lambda b, c: (b * nc + c, 0))
    per_b = lambda *s: pl.BlockSpec((1,) + s, lambda b, c: (b,) + (0,) * len(s))
    const = lambda a: pl.BlockSpec(a.shape, lambda b, c: (0,) * a.ndim)
    return pl.pallas_call(
        functools.partial(_mlstm_kernel, chunk=chunk, head_dim=head_dim),
        grid=(batch, nc),
        in_specs=[rows(2 * dm), rows(dm), rows(dm), rows(LANES),
                  per_b(M_HEADS, head_dim, head_dim), per_b(SUBLANES, head_dim), per_b(SUBLANES, LANES),
                  per_b(SUBLANES, 2 * dm),
                  const(conv_w), const(conv_b), const(bg), const(norm_w)],
        out_specs=[rows(dm), per_b(M_HEADS, head_dim, head_dim), per_b(SUBLANES, head_dim),
                   per_b(SUBLANES, LANES)],
        out_shape=[jax.ShapeDtypeStruct((n, dm), BF16),
                   jax.ShapeDtypeStruct((batch, M_HEADS, head_dim, head_dim), F32),
                   jax.ShapeDtypeStruct((batch, SUBLANES, head_dim), F32),
                   jax.ShapeDtypeStruct((batch, SUBLANES, LANES), F32)],
        scratch_shapes=[pltpu.VMEM((chunk + SUBLANES, 2 * dm), F32),
                        pltpu.VMEM((M_HEADS, head_dim, head_dim), F32),
                        pltpu.VMEM((SUBLANES, head_dim), F32),
                        pltpu.VMEM((SUBLANES, LANES), F32)],
        compiler_params=_params("arbitrary", "arbitrary"),
        name="mlstm",
    )(qk, v, og, g, c0, n0p, m0p, tail0, conv_w, conv_b, bg, norm_w)


def _s5_kernel(u_ref, sre0_ref, sim0_ref, wb_ref, are_ref, aim_ref, asre_ref, asim_ref,
               wcr_ref, wci_ref, dskip_ref, wglu_ref, bglu_ref, permt_ref,
               hs_ref, sre_out, sim_out,
               xr_ref, xi_ref, hr_ref, hi_ref, *, seg):
    bi = pl.program_id(1)
    n_state = xr_ref.shape[1]
    n_tiles = wb_ref.shape[0]

    @pl.when(bi == 0)
    def _():
        hr_ref[...] = sre0_ref[0]
        hi_ref[...] = sim0_ref[0]

    ub = u_ref[...].astype(BF16)
    tiles_per_slab = LANES // (2 * S5_GROUP)
    row = lax.broadcasted_iota(jnp.int32, (N_STREAMS, SCAN_LANES), 0)
    for sl in range(n_state // SCAN_LANES):
        cols = slice(sl * SCAN_LANES, (sl + 1) * SCAN_LANES)
        for j in range(sl * (SCAN_LANES // LANES), (sl + 1) * (SCAN_LANES // LANES)):
            cs = (j // tiles_per_slab) * LANES
            xj = _dot(ub[:, cs:cs + LANES], wb_ref[j])
            xr_ref[:, j * LANES:(j + 1) * LANES] = xj[:, :LANES]
            xi_ref[:, j * LANES:(j + 1) * LANES] = xj[:, LANES:]
        ar = jnp.broadcast_to(are_ref[:, cols], (N_STREAMS, SCAN_LANES))
        ai = jnp.broadcast_to(aim_ref[:, cols], (N_STREAMS, SCAN_LANES))

        er = xr_ref[0:N_STREAMS, cols]
        ei = xi_ref[0:N_STREAMS, cols]
        for i in range(1, seg):
            rows = slice(i * N_STREAMS, (i + 1) * N_STREAMS)
            nr = ar * er - ai * ei + xr_ref[rows, cols]
            ni = ar * ei + ai * er + xi_ref[rows, cols]
            xr_ref[rows, cols] = nr
            xi_ref[rows, cols] = ni
            er, ei = nr, ni

        asr = asre_ref[:, cols]
        asi = asim_ref[:, cols]
        cr = hr_ref[:, cols]
        ci = hi_ref[:, cols]
        cmr = jnp.zeros((N_STREAMS, SCAN_LANES), F32)
        cmi = jnp.zeros((N_STREAMS, SCAN_LANES), F32)
        for k in range(N_STREAMS):
            cmr = jnp.where(row == k, cr, cmr)
            cmi = jnp.where(row == k, ci, cmi)
            nr = asr * cr - asi * ci + er[k:k + 1, :]
            ni = asr * ci + asi * cr + ei[k:k + 1, :]
            cr, ci = nr, ni
        hr_ref[:, cols] = cr
        hi_ref[:, cols] = ci

        dr, di = cmr, cmi
        for i in range(seg):
            rows = slice(i * N_STREAMS, (i + 1) * N_STREAMS)
            dr, di = ar * dr - ai * di, ar * di + ai * dr
            xr_ref[rows, cols] += dr
            xi_ref[rows, cols] += di

    n_out_tiles = wcr_ref.shape[0]
    kw = n_state // n_out_tiles
    ys = []
    for qt in range(n_out_tiles):
        hr = xr_ref[:, qt * kw:(qt + 1) * kw].astype(BF16)
        hi = xi_ref[:, qt * kw:(qt + 1) * kw].astype(BF16)
        ys.append(_dot(hr, wcr_ref[qt]) + _dot(hi, wci_ref[qt]))
    y = jnp.concatenate(ys, axis=-1) + dskip_ref[...] * u_ref[...]
    gl = 0.5 * y * (1.0 + jnp.tanh(math.sqrt(2.0 / math.pi) * (y + 0.044715 * (y * y * y))))
    z = _dot(gl.astype(BF16), wglu_ref[...]) + bglu_ref[...]
    out = gl * jax.nn.sigmoid(z)
    hs_ref[...] = _dot(permt_ref[...], out.astype(BF16)).astype(BF16)

    @pl.when(bi == pl.num_programs(1) - 1)
    def _():
        sre_out[0] = hr_ref[...]
        sim_out[0] = hi_ref[...]


def _s5(u, sre0, sim0, wb, a_re, a_im, as_re, as_im, wcr, wci, dskip, wglu, bglu, permt,
        *, batch, seq, block):
    n, ds5 = u.shape
    n_state = a_re.shape[1]
    nb = seq // block
    rows = lambda w: pl.BlockSpec((block, w), lambda b, c: (b * nb + c, 0))
    per_b = pl.BlockSpec((1, 1, n_state), lambda b, c: (b, 0, 0))
    const = lambda a: pl.BlockSpec(a.shape, lambda b, c: (0,) * a.ndim)
    return pl.pallas_call(
        functools.partial(_s5_kernel, seg=block // N_STREAMS),
        grid=(batch, nb),
        in_specs=[rows(ds5), per_b, per_b, const(wb), const(a_re), const(a_im), const(as_re),
                  const(as_im), const(wcr), const(wci), const(dskip), const(wglu), const(bglu),
                  const(permt)],
        out_specs=[rows(ds5), per_b, per_b],
        out_shape=[jax.ShapeDtypeStruct((n, ds5), BF16),
                   jax.ShapeDtypeStruct((batch, 1, n_state), F32),
                   jax.ShapeDtypeStruct((batch, 1, n_state), F32)],
        scratch_shapes=[pltpu.VMEM((block, n_state), F32), pltpu.VMEM((block, n_state), F32),
                        pltpu.VMEM((1, n_state), F32), pltpu.VMEM((1, n_state), F32)],
        compiler_params=_params("arbitrary", "arbitrary"),
        name="s5",
    )(u, sre0, sim0, wb, a_re, a_im, as_re, as_im, wcr, wci, dskip, wglu, bglu, permt)


def _layer_norm(x, g, b):
    mu = jnp.mean(x, axis=-1, keepdims=True)
    xc = x - mu
    var = jnp.mean(xc * xc, axis=-1, keepdims=True)
    return xc * lax.rsqrt(var + LN_EPS) * g + b


def _first_max(cols):
    best = cols[0]
    for c in cols[1:]:
        best = jnp.maximum(best, c)
    flags = []
    taken = None
    for c in cols:
        hit = c == best
        if taken is not None:
            hit = jnp.logical_and(hit, jnp.logical_not(taken))
            taken = jnp.logical_or(taken, hit)
        else:
            taken = hit
        flags.append(hit)
    return best, flags


def _outproj_kernel(x_ref, hm_ref, hs_ref, woa_ref, wob_ref, g1_ref, b1_ref, wrh_ref, wrl_ref, br_ref,
                    x1_ref, route_ref, *, alpha):
    d = x_ref.shape[1]
    mix = _dot(hm_ref[...], woa_ref[...]) + _dot(hs_ref[...], wob_ref[...])
    x1 = _layer_norm(alpha * x_ref[...] + mix, g1_ref[...], b1_ref[...])
    x1_ref[:, 0:d] = x1

    xh = x1.astype(BF16)
    xl = (x1 - xh.astype(F32)).astype(BF16)
    logits = (_dot(xh, wrh_ref[...]) + _dot(xl, wrh_ref[...]) + _dot(xh, wrl_ref[...])).T + br_ref[...]

    l1 = [logits[g:g + 1, :] for g in range(N_EXPERT_GROUPS)]
    m1, gsel = _first_max(l1)
    denom = l1[0] * 0.0
    for c in l1:
        denom = denom + jnp.exp(c - m1)
    p_g = 1.0 / denom
    gid = l1[0] * 0.0
    l2 = []
    for e in range(EXPERTS_PER_GROUP):
        acc = l1[0] * 0.0
        for g in range(N_EXPERT_GROUPS):
            c0 = N_EXPERT_GROUPS + g * EXPERTS_PER_GROUP + e
            acc = acc + jnp.where(gsel[g], logits[c0:c0 + 1, :], 0.0)
        l2.append(acc)
    for g in range(N_EXPERT_GROUPS):
        gid = gid + jnp.where(gsel[g], float(g), 0.0)

    v1, f1 = _first_max(l2)
    masked = [jnp.where(f1[e], -jnp.inf, l2[e]) for e in range(EXPERTS_PER_GROUP)]
    v2, f2 = _first_max(masked)
    e21 = jnp.exp(v2 - v1)
    w_first = 1.0 / (1.0 + e21)
    w_second = e21 / (1.0 + e21)
    pair_id = l1[0] * 0.0
    gate_a = l1[0] * 0.0
    gate_b = l1[0] * 0.0
    pid = 0
    for a in range(EXPERTS_PER_GROUP):
        for b in range(a + 1, EXPERTS_PER_GROUP):
            ab = jnp.logical_and(f1[a], f2[b])
            ba = jnp.logical_and(f1[b], f2[a])
            pair_id = pair_id + jnp.where(jnp.logical_or(ab, ba), float(pid), 0.0)
            gate_a = gate_a + jnp.where(ab, w_first, 0.0) + jnp.where(ba, w_second, 0.0)
            gate_b = gate_b + jnp.where(ab, w_second, 0.0) + jnp.where(ba, w_first, 0.0)
            pid += 1
    bin_id = gid * float(N_PAIRS) + pair_id
    row = lax.broadcasted_iota(jnp.int32, (LANES, x_ref.shape[0]), 0)
    route_t = jnp.where(row == 0, bin_id,
                        jnp.where(row == 1, p_g * gate_a, jnp.where(row == 2, p_g * gate_b, 0.0)))
    route_ref[...] = route_t[0:SUBLANES, :]
    x1_ref[:, d:d + LANES] = route_t.T


def _outproj(x2, hm, hs, woa, wob, g1, b1, wrh, wrl, br_col, *, alpha):
    n, d = x2.shape
    dm = hm.shape[1]
    rows = lambda w: pl.BlockSpec((ROW_TILE, w), lambda i: (i, 0))
    return pl.pallas_call(
        functools.partial(_outproj_kernel, alpha=alpha),
        grid=(n // ROW_TILE,),
        in_specs=[rows(d), rows(dm), rows(hs.shape[1]), _resident(woa.shape), _resident(wob.shape),
                  _resident(g1.shape), _resident(b1.shape), _resident(wrh.shape), _resident(wrl.shape),
                  _resident(br_col.shape)],
        out_specs=[rows(d + LANES), pl.BlockSpec((SUBLANES, ROW_TILE), lambda i: (0, i))],
        out_shape=[jax.ShapeDtypeStruct((n, d + LANES), F32), jax.ShapeDtypeStruct((SUBLANES, n), F32)],
        compiler_params=_params("arbitrary"),
        name="outproj",
    )(x2, hm, hs, woa, wob, g1, b1, wrh, wrl, br_col)


def _moe_kernel(ea_ref, eb_ref, nv_ref, npr_ref,
                idx_ref, idxn_ref, wga_ref, wua_ref, wda_ref, wgb_ref, wub_ref, wdb_ref,
                g2_ref, b2_ref, x1p_hbm, x1s_hbm,
                yp_hbm, ys_hbm,
                xbuf, obuf, gsem, ssem, *, alpha, n_prompt):
    t = pl.program_id(0)
    nt = pl.num_programs(0)
    slot = t % 2

    def gather_copy(tok, r, s, from_prompt):
        src = x1p_hbm.at[pl.ds(tok, 1)] if from_prompt else x1s_hbm.at[pl.ds(tok - n_prompt, 1)]
        return pltpu.make_async_copy(src, xbuf.at[s, pl.ds(r, 1)], gsem.at[s])

    def scatter_copy(tok, r, s, to_prompt):
        dst = yp_hbm.at[pl.ds(tok, 1)] if to_prompt else ys_hbm.at[pl.ds(tok - n_prompt, 1)]
        return pltpu.make_async_copy(obuf.at[s, pl.ds(r, 1)], dst, ssem.at[s])

    def for_rows(lo, hi, fn):
        n_groups = (hi - lo) // ISSUE_UNROLL

        def group(gi, c):
            base = lo + gi * ISSUE_UNROLL
            for j in range(ISSUE_UNROLL):
                fn(base + j)
            return c

        def single(r, c):
            fn(r)
            return c

        lax.fori_loop(0, n_groups, group, 0)
        lax.fori_loop(lo + n_groups * ISSUE_UNROLL, hi, single, 0)

    def start_gather(tile, s, ids):
        for_rows(0, npr_ref[tile], lambda r: gather_copy(ids[0, 0, r], r, s, True).start())
        for_rows(npr_ref[tile], nv_ref[tile], lambda r: gather_copy(ids[0, 0, r], r, s, False).start())

    def wait_rows(src, dst, sem, count):
        bit = xbuf.shape[1]
        while bit >= 1:
            @pl.when((count & bit) != 0)
            def _(bit=bit):
                pltpu.make_async_copy(src.at[pl.ds(0, bit)], dst.at[pl.ds(0, bit)], sem).wait()
            bit //= 2

    @pl.when(t == 0)
    def _():
        xbuf[...] = jnp.zeros_like(xbuf)
        start_gather(0, 0, idx_ref)

    @pl.when(t + 1 < nt)
    def _():
        start_gather(t + 1, 1 - slot, idxn_ref)

    wait_rows(x1p_hbm, xbuf.at[slot], gsem.at[slot], nv_ref[t])

    @pl.when(t >= 2)
    def _():
        wait_rows(obuf.at[slot], yp_hbm, ssem.at[slot], nv_ref[t - 2])

    @pl.when(nv_ref[t] > 0)
    def _():
        d = obuf.shape[2]
        x1 = xbuf[slot, :, 0:d]
        gate_a = xbuf[slot, :, d + 1:d + 2]
        gate_b = xbuf[slot, :, d + 2:d + 3]
        xb = x1.astype(BF16)
        ha = _dot(xb, wga_ref[0])
        ha = (ha * jax.nn.sigmoid(ha)) * _dot(xb, wua_ref[0])
        ya = _dot(ha.astype(BF16), wda_ref[0])
        hb = _dot(xb, wgb_ref[0])
        hb = (hb * jax.nn.sigmoid(hb)) * _dot(xb, wub_ref[0])
        yb = _dot(hb.astype(BF16), wdb_ref[0])
        moe = gate_a * ya + gate_b * yb
        obuf[slot] = _layer_norm(alpha * x1 + moe, g2_ref[...], b2_ref[...])

    for_rows(0, npr_ref[t], lambda r: scatter_copy(idx_ref[0, 0, r], r, slot, True).start())
    for_rows(npr_ref[t], nv_ref[t], lambda r: scatter_copy(idx_ref[0, 0, r], r, slot, False).start())

    @pl.when(t == nt - 1)
    def _():
        wait_rows(obuf.at[slot], yp_hbm, ssem.at[slot], nv_ref[t])

        @pl.when(t >= 1)
        def _():
            wait_rows(obuf.at[1 - slot], yp_hbm, ssem.at[1 - slot], nv_ref[t - 1])


def _moe(x1p, x1s, route_p, route_s, wg, wu, wd, g2, b2, *, alpha):
    n_p = x1p.shape[0]
    d = x1p.shape[1] - LANES
    n_s = x1s.shape[0]
    n = n_p + n_s
    tm = MOE_TILE
    nt = n // tm + N_BINS if n % tm == 0 else (n + N_BINS * (tm - 1)) // tm + 1
    bins = jnp.concatenate([route_p[0], route_s[0]], axis=0).astype(jnp.int32)

    onehot = (bins[:, None] == jnp.arange(N_BINS, dtype=jnp.int32)[None, :]).astype(jnp.int32)
    csum = jnp.cumsum(onehot, axis=0)
    rank = jnp.sum(csum * onehot, axis=1) - 1
    cnt = csum[n - 1]
    cnt_p = csum[n_p - 1]
    tiles_b = (cnt + tm - 1) // tm
    tile_end = jnp.cumsum(tiles_b)
    tile_start = tile_end - tiles_b
    slot_of_token = jnp.sum(onehot * tile_start[None, :], axis=1) * tm + rank
    idx = jnp.zeros((nt * tm,), jnp.int32).at[slot_of_token].set(
        jnp.arange(n, dtype=jnp.int32), unique_indices=True).reshape(nt, tm)
    tid = jnp.arange(nt, dtype=jnp.int32)
    used = tid < tile_end[N_BINS - 1]
    tbin = jnp.minimum(jnp.sum((tid[:, None] >= tile_end[None, :]).astype(jnp.int32), axis=1), N_BINS - 1)
    last_bin = jnp.max(jnp.where(cnt > 0, jnp.arange(N_BINS, dtype=jnp.int32), 0))
    tbin = jnp.where(used, tbin, last_bin)
    tile_onehot = (tbin[:, None] == jnp.arange(N_BINS, dtype=jnp.int32)[None, :]).astype(jnp.int32)
    lookup = lambda table: jnp.sum(tile_onehot * table[None, :], axis=1)
    kk = tid - lookup(tile_start)
    nvalid = jnp.where(used, jnp.clip(lookup(cnt) - kk * tm, 0, tm), 0).astype(jnp.int32)
    nprompt = jnp.where(used, jnp.clip(lookup(cnt_p) - kk * tm, 0, nvalid), 0).astype(jnp.int32)
    pairs = [(a, b) for a in range(EXPERTS_PER_GROUP) for b in range(a + 1, EXPERTS_PER_GROUP)]
    bin_group = np.arange(N_BINS) // N_PAIRS
    ea = lookup(jnp.asarray(bin_group * EXPERTS_PER_GROUP + np.array([p[0] for p in pairs] * N_EXPERT_GROUPS),
                            jnp.int32))
    eb = lookup(jnp.asarray(bin_group * EXPERTS_PER_GROUP + np.array([p[1] for p in pairs] * N_EXPERT_GROUPS),
                            jnp.int32))
    idx3 = idx.reshape(nt, 1, tm)

    dff = wg.shape[2]
    wspec_in = lambda sel: pl.BlockSpec((1, d, dff), lambda t, ea, eb, nv, npr: (sel(ea, eb)[t], 0, 0))
    wspec_out = lambda sel: pl.BlockSpec((1, dff, d), lambda t, ea, eb, nv, npr: (sel(ea, eb)[t], 0, 0))
    first = lambda a, b: a
    second = lambda a, b: b
    const2 = lambda a: pl.BlockSpec(a.shape, lambda t, *_: (0,) * a.ndim)
    smem_rows = lambda fn: pl.BlockSpec((1, 1, tm), fn, memory_space=pltpu.SMEM)
    grid_spec = pltpu.PrefetchScalarGridSpec(
        num_scalar_prefetch=4,
        grid=(nt,),
        in_specs=[smem_rows(lambda t, *_: (t, 0, 0)),
                  smem_rows(lambda t, *_: (jnp.minimum(t + 1, nt - 1), 0, 0)),
                  wspec_in(first), wspec_in(first), wspec_out(first),
                  wspec_in(second), wspec_in(second), wspec_out(second),
                  const2(g2), const2(b2),
                  pl.BlockSpec(memory_space=pl.ANY), pl.BlockSpec(memory_space=pl.ANY)],
        out_specs=[pl.BlockSpec(memory_space=pl.ANY), pl.BlockSpec(memory_space=pl.ANY)],
        scratch_shapes=[pltpu.VMEM((2, tm, d + LANES), F32), pltpu.VMEM((2, tm, d), F32),
                        pltpu.SemaphoreType.DMA((2,)), pltpu.SemaphoreType.DMA((2,))],
    )
    return pl.pallas_call(
        functools.partial(_moe_kernel, alpha=alpha, n_prompt=n_p),
        grid_spec=grid_spec,
        out_shape=[jax.ShapeDtypeStruct((n_p, d), F32), jax.ShapeDtypeStruct((n_s, d), F32)],
        compiler_params=_params("arbitrary"),
        name="moe",
    )(ea, eb, nvalid, nprompt, idx3, idx3, wg, wu, wd, wg, wu, wd, g2, b2, x1p, x1s)


def _s5_tables(lam_re, lam_im, log_dt, b_re, b_im, c_re, c_im, seg_lens):
    n_groups, n_state = lam_re.shape
    dt = jnp.exp(log_dt.astype(F32))[:, None]
    lam_re = lam_re.astype(F32)
    lam_im = lam_im.astype(F32)
    mag = jnp.exp(lam_re * dt)
    a_re = mag * jnp.cos(lam_im * dt)
    a_im = mag * jnp.sin(lam_im * dt)
    e_re = a_re - 1.0
    e_im = a_im
    lam_sq = lam_re * lam_re + lam_im * lam_im
    coef_re = (e_re * lam_re + e_im * lam_im) / lam_sq
    coef_im = (e_im * lam_re - e_re * lam_im) / lam_sq
    b_re = b_re.astype(F32)
    b_im = b_im.astype(F32)
    bb_re = coef_re[..., None] * b_re - coef_im[..., None] * b_im
    bb_im = coef_re[..., None] * b_im + coef_im[..., None] * b_re

    gpt = LANES // n_state
    n_tiles = n_groups // gpt
    gps = LANES // S5_GROUP
    g_in_slab = (jnp.arange(n_groups) % gps)
    rows_onehot = jax.nn.one_hot(g_in_slab, gps, dtype=F32)
    w_re = jnp.einsum("gs,gpc->gscp", rows_onehot, bb_re).reshape(n_groups, LANES, n_state)
    w_im = jnp.einsum("gs,gpc->gscp", rows_onehot, bb_im).reshape(n_groups, LANES, n_state)
    w_re = w_re.reshape(n_tiles, gpt, LANES, n_state).transpose(0, 2, 1, 3).reshape(n_tiles, LANES, LANES)
    w_im = w_im.reshape(n_tiles, gpt, LANES, n_state).transpose(0, 2, 1, 3).reshape(n_tiles, LANES, LANES)
    wb = jnp.concatenate([w_re, w_im], axis=-1).astype(BF16)

    n_flat = n_groups * n_state
    out_w = 2 * LANES
    g_per_out = out_w // S5_GROUP
    n_out_tiles = n_groups // g_per_out
    oh = jax.nn.one_hot(jnp.arange(n_groups) % g_per_out, g_per_out, dtype=F32)
    wcr = jnp.einsum("gs,gcp->gpsc", oh, c_re.astype(F32)).reshape(n_out_tiles, g_per_out * n_state, out_w)
    wci = jnp.einsum("gs,gcp->gpsc", oh, -c_im.astype(F32)).reshape(n_out_tiles, g_per_out * n_state, out_w)

    def powers(k):
        return ((mag ** k) * jnp.cos(lam_im * dt * k)).reshape(1, n_flat), \
               ((mag ** k) * jnp.sin(lam_im * dt * k)).reshape(1, n_flat)

    return (wb, a_re.reshape(1, n_flat), a_im.reshape(1, n_flat), [powers(float(k)) for k in seg_lens],
            wcr.astype(BF16), wci.astype(BF16))


def _layer(x_p, x_s, st, lp):
    (w_in, b_gates, conv_w, conv_b, norm_w, lam_re, lam_im, log_dt, b_re, b_im, c_re, c_im,
     d_skip, w_glu, b_glu, w_out, ln1_g, ln1_b, w_r1, b_r1, w_r2, b_r2, w_gate, w_up, w_down,
     ln2_g, ln2_b, alpha) = lp
    state_c, state_n, state_m, state_conv, state_re, state_im = st
    bp, tp, d = x_p.shape
    bs, ts, _ = x_s.shape
    dm = norm_w.shape[0]
    head_dim = dm // M_HEADS
    ds5 = d_skip.shape[0]
    n_groups, n_state = lam_re.shape
    n_flat = n_groups * n_state

    o0 = 2 * dm
    o1 = o0 + 2 * dm
    wqk = w_in[:, :o0].astype(BF16)
    wv = w_in[:, o0:o0 + dm].astype(BF16)
    wo = w_in[:, o0 + dm:o1].astype(BF16)
    wg = jnp.pad(w_in[:, o1:o1 + 2 * M_HEADS], ((0, 0), (0, LANES - 2 * M_HEADS))).astype(BF16)
    wu = w_in[:, o1 + 2 * M_HEADS:].astype(BF16)
    bg = jnp.pad(b_gates.astype(F32), (0, LANES - 2 * M_HEADS)).reshape(1, LANES)
    woa = w_out[:dm].astype(BF16)
    wob = w_out[dm:].astype(BF16)
    wr = jnp.concatenate([w_r1, jnp.transpose(w_r2, (1, 0, 2)).reshape(d, -1)], axis=1).astype(F32)
    n_logits = wr.shape[1]
    wr = jnp.pad(wr, ((0, 0), (0, LANES - n_logits)))
    wrh = wr.astype(BF16)
    wrl = (wr - wrh.astype(F32)).astype(BF16)
    br = jnp.pad(jnp.concatenate([b_r1, b_r2.reshape(-1)]).astype(F32), (0, LANES - n_logits))
    br = br.reshape(LANES, 1)

    blk_p = min(tp, ROW_TILE)
    blk_s = min(ts, ROW_TILE)
    seg_p = blk_p // N_STREAMS
    seg_s = blk_s // N_STREAMS
    wb, a_re, a_im, (as_p, as_s), wcr, wci = _s5_tables(lam_re, lam_im, log_dt, b_re, b_im, c_re, c_im,
                                                         (seg_p, seg_s))

    outs = []
    for x, blk, seg, a_seg, zero_state in ((x_p, blk_p, seg_p, as_p, True), (x_s, blk_s, seg_s, as_s, False)):
        b, t, _ = x.shape
        x2 = x.reshape(b * t, d)
        perm_np = _stream_perm(blk, seg)
        perm = jnp.asarray(perm_np, BF16)
        permt = jnp.asarray(perm_np[:blk, :blk].T, BF16)
        qk, v, og, u, g = _inproj(x2, perm, wqk, wv, wo, wu, wg)

        if zero_state:
            c0 = jnp.zeros((b, M_HEADS, head_dim, head_dim), F32)
            n0 = jnp.zeros((b, M_HEADS, head_dim), F32)
            m0 = jnp.zeros((b, M_HEADS), F32)
            conv0 = jnp.zeros((b, CONV_W - 1, 2 * dm), F32)
            re0 = jnp.zeros((b, n_groups, n_state), F32)
            im0 = jnp.zeros((b, n_groups, n_state), F32)
        else:
            c0, n0, m0, conv0, re0, im0 = (state_c.astype(F32), state_n.astype(F32), state_m.astype(F32),
                                           state_conv.astype(F32), state_re.astype(F32), state_im.astype(F32))
        n0p = jnp.pad(n0, ((0, 0), (0, SUBLANES - M_HEADS), (0, 0)))
        m0p = jnp.broadcast_to(jnp.pad(m0, ((0, 0), (0, SUBLANES - M_HEADS)))[:, :, None], (b, SUBLANES, LANES))
        tail0 = jnp.pad(conv0, ((0, 0), (SUBLANES - (CONV_W - 1), 0), (0, 0)))
        chunk = min(t, LANES)
        hm, c_new, n_new, m_new = _mlstm(qk, v, og, g, c0, n0p, m0p, tail0, conv_w.astype(F32),
                                         conv_b.astype(F32).reshape(1, -1), bg, norm_w.astype(F32).reshape(1, -1),
                                         batch=b, seq=t, chunk=chunk)
        hs, re_new, im_new = _s5(u, re0.reshape(b, 1, n_flat), im0.reshape(b, 1, n_flat), wb, a_re, a_im,
                                 a_seg[0], a_seg[1], wcr, wci, d_skip.astype(F32).reshape(1, -1),
                                 w_glu.astype(BF16), b_glu.astype(F32).reshape(1, -1), permt,
                                 batch=b, seq=t, block=blk)
        x1, route = _outproj(x2, hm, hs, woa, wob, ln1_g.astype(F32).reshape(1, -1),
                             ln1_b.astype(F32).reshape(1, -1), wrh, wrl, br, alpha=alpha)
        conv_new = jnp.concatenate([conv0, qk.reshape(b, t, 2 * dm)], axis=1)[:, -(CONV_W - 1):]
        states = (c_new, n_new[:, :M_HEADS], m_new[:, :M_HEADS, 0], conv_new,
                  re_new.reshape(b, n_groups, n_state), im_new.reshape(b, n_groups, n_state))
        outs.append((x1, route, states))

    (x1p, route_p, st_p), (x1s, route_s, st_s) = outs
    yp, ys = _moe(x1p, x1s, route_p, route_s, w_gate.astype(BF16), w_up.astype(BF16), w_down.astype(BF16),
                  ln2_g.astype(F32).reshape(1, -1), ln2_b.astype(F32).reshape(1, -1), alpha=alpha)
    return yp.reshape(bp, tp, d), ys.reshape(bs, ts, d), st_p, st_s


def kernel(x_prompt, x_sample, state_mlstm_C, state_mlstm_n, state_mlstm_m, state_conv, state_s5_re, state_s5_im, w_in, b_gates, conv_w, conv_b, mlstm_norm_w, s5_lam_re, s5_lam_im, s5_log_dt, s5_b_re, s5_b_im, s5_c_re, s5_c_im, s5_d, w_glu, b_glu, w_out, ln1_g, ln1_b, w_r1, b_r1, w_r2, b_r2, w_gate, w_up, w_down, ln2_g, ln2_b):
    depth = w_in.shape[0]
    alpha = (2 * depth) ** 0.25
    yp, ys = x_prompt, x_sample
    sts_p, sts_s = [], []
    for l in range(depth):
        lp = (w_in[l], b_gates[l], conv_w[l], conv_b[l], mlstm_norm_w[l], s5_lam_re[l], s5_lam_im[l],
              s5_log_dt[l], s5_b_re[l], s5_b_im[l], s5_c_re[l], s5_c_im[l], s5_d[l], w_glu[l], b_glu[l],
              w_out[l], ln1_g[l], ln1_b[l], w_r1[l], b_r1[l], w_r2[l], b_r2[l], w_gate[l], w_up[l],
              w_down[l], ln2_g[l], ln2_b[l], alpha)
        st = (state_mlstm_C[l], state_mlstm_n[l], state_mlstm_m[l], state_conv[l], state_s5_re[l],
              state_s5_im[l])
        yp, ys, sp, ss = _layer(yp, ys, st, lp)
        sts_p.append(sp)
        sts_s.append(ss)
    stack = lambda sts, i: jnp.stack([s[i] for s in sts])
    return (yp, ys) + tuple(stack(sts_p, i) for i in range(6)) + tuple(stack(sts_s, i) for i in range(6))
```

```python
import functools
import math

import jax
import jax.numpy as jnp
import numpy as np
from jax import lax
from jax.experimental import pallas as pl
from jax.experimental.pallas import tpu as pltpu

F32 = jnp.float32
BF16 = jnp.bfloat16

LANES = 128
SUBLANES = 8
VMEM_LIMIT = 56 * 1024 * 1024

M_HEADS = 4
CONV_W = 4
S5_GROUP = 16
S5_STATE = 64
N_EXPERT_GROUPS = 4
EXPERTS_PER_GROUP = 4
N_PAIRS = 6
N_BINS = N_EXPERT_GROUPS * N_PAIRS
LN_EPS = 1e-5
NEG = -1e30

ROW_TILE = 256
MOE_TILE = 256
ISSUE_UNROLL = 8
N_STREAMS = SUBLANES
SCAN_LANES = 512


def _dot(a, b):
    return jnp.dot(a, b, preferred_element_type=F32)


def _dot_nt(a, b):
    return lax.dot_general(a, b, (((1,), (1,)), ((), ())), preferred_element_type=F32)


def _resident(shape):
    nd = len(shape)
    return pl.BlockSpec(shape, lambda *_: (0,) * nd, pipeline_mode=pl.Buffered(1))


def _params(*sem):
    return pltpu.CompilerParams(dimension_semantics=sem, vmem_limit_bytes=VMEM_LIMIT)


def _stream_perm(block, seg):
    n_seg = block // seg
    r = np.arange(block)
    src = (r % n_seg) * seg + r // n_seg
    p = np.zeros((block, block), np.float32)
    p[r, src] = 1.0
    reps = ROW_TILE // block
    return np.kron(np.eye(reps, dtype=np.float32), p)


def _inproj_kernel(x_ref, perm_ref, wqk_ref, wv_ref, wo_ref, wu_ref, wg_ref,
                   qk_ref, v_ref, og_ref, u_ref, g_ref):
    xb = x_ref[...].astype(BF16)
    qk_ref[...] = _dot(xb, wqk_ref[...])
    v_ref[...] = _dot(xb, wv_ref[...]).astype(BF16)
    og_ref[...] = _dot(xb, wo_ref[...])
    g_ref[...] = _dot(xb, wg_ref[...])
    xp = _dot(perm_ref[...], xb).astype(BF16)
    u_ref[...] = _dot(xp, wu_ref[...])


def _inproj(x2, perm, wqk, wv, wo, wu, wg):
    n, d = x2.shape
    dm = wv.shape[1]
    ds5 = wu.shape[1]
    rows = lambda w: pl.BlockSpec((ROW_TILE, w), lambda i: (i, 0))
    return pl.pallas_call(
        _inproj_kernel,
        grid=(n // ROW_TILE,),
        in_specs=[rows(d), _resident(perm.shape), _resident(wqk.shape), _resident(wv.shape),
                  _resident(wo.shape), _resident(wu.shape), _resident(wg.shape)],
        out_specs=[rows(2 * dm), rows(dm), rows(dm), rows(ds5), rows(LANES)],
        out_shape=[jax.ShapeDtypeStruct((n, 2 * dm), F32), jax.ShapeDtypeStruct((n, dm), BF16),
                   jax.ShapeDtypeStruct((n, dm), F32), jax.ShapeDtypeStruct((n, ds5), F32),
                   jax.ShapeDtypeStruct((n, LANES), F32)],
        compiler_params=_params("arbitrary"),
        name="inproj",
    )(x2, perm, wqk, wv, wo, wu, wg)


def _cumsum_rows(x):
    n = x.shape[0]
    row = lax.broadcasted_iota(jnp.int32, x.shape, 0)
    s = 1
    while s < n:
        x = x + jnp.where(row >= s, pltpu.roll(x, s, 0), 0.0)
        s *= 2
    return x


def _transpose_gate_cols(x):
    sel = (lax.broadcasted_iota(jnp.int32, (SUBLANES, LANES), 0)
           == lax.broadcasted_iota(jnp.int32, (SUBLANES, LANES), 1)).astype(BF16)
    hi = x.astype(BF16)
    r1 = x - hi.astype(F32)
    mid = r1.astype(BF16)
    lo = (r1 - mid.astype(F32)).astype(BF16)
    return _dot_nt(sel, hi) + _dot_nt(sel, mid) + _dot_nt(sel, lo)


def _mlstm_kernel(qk_ref, v_ref, og_ref, g_ref, c0_ref, n0_ref, m0_ref, tail0_ref,
                  cw_ref, cb_ref, bg_ref, nw_ref,
                  h_ref, c_out, n_out, m_out,
                  ext_ref, c_ref, n_ref, m_ref, *, chunk, head_dim):
    ci = pl.program_id(1)
    dm = M_HEADS * head_dim
    L = chunk

    @pl.when(ci == 0)
    def _():
        c_ref[...] = c0_ref[0]
        n_ref[...] = n0_ref[0]
        m_ref[...] = m0_ref[0]
        ext_ref[0:SUBLANES, :] = tail0_ref[0]

    ext_ref[SUBLANES:SUBLANES + L, :] = qk_ref[...]
    acc = cb_ref[...]
    for j in range(CONV_W):
        acc = acc + ext_ref[pl.ds(SUBLANES - (CONV_W - 1) + j, L), :] * cw_ref[j:j + 1, :]
    tail = ext_ref[L:L + SUBLANES, :]
    ext_ref[0:SUBLANES, :] = tail
    qk = acc * jax.nn.sigmoid(acc)

    gates = g_ref[...] + bg_ref[...]
    fpre = pltpu.roll(gates, LANES - M_HEADS, 1)
    logf = jnp.minimum(fpre, 0.0) - jnp.log1p(jnp.exp(-jnp.abs(fpre)))
    bcum = _cumsum_rows(logf)
    rrow = _transpose_gate_cols(gates - bcum)
    tri = (lax.broadcasted_iota(jnp.int32, (L, L), 0) >= lax.broadcasted_iota(jnp.int32, (L, L), 1))

    for h in range(M_HEADS):
        lo, hi = h * head_dim, (h + 1) * head_dim
        q = qk[:, lo:hi]
        k = qk[:, dm + lo:dm + hi] * (head_dim ** -0.5)
        v = v_ref[:, lo:hi]
        qb = q.astype(BF16)
        kb = k.astype(BF16)
        b_col = bcum[:, h:h + 1]
        ig_col = gates[:, h:h + 1]
        m_prev = m_ref[h:h + 1, 0:1]
        c_prev = c_ref[h]
        n_prev = n_ref[h:h + 1, :]

        dmat = jnp.where(tri, b_col + rrow[h:h + 1, :], NEG)
        inter = b_col + m_prev
        m_t = jnp.maximum(inter, jnp.max(dmat, axis=-1, keepdims=True))
        w_intra = jnp.exp(dmat - m_t)
        w_inter = jnp.exp(inter - m_t)
        s = _dot_nt(qb, kb) * w_intra
        num = w_inter * _dot(qb, c_prev.astype(BF16)) + _dot(s.astype(BF16), v)
        den = w_inter * jnp.sum(q * n_prev, axis=-1, keepdims=True) + jnp.sum(s, axis=-1, keepdims=True)
        hh = num / jnp.maximum(jnp.abs(den), jnp.exp(-m_t))

        m_new = m_t[L - 1:L, :]
        b_last = b_col[L - 1:L, :]
        w_s = jnp.exp(b_last - b_col + ig_col - m_new)
        decay = jnp.exp(b_last + m_prev - m_new)
        kw = k * w_s
        c_ref[h] = decay * c_prev + _dot(kw.T.astype(BF16), v)
        n_ref[h:h + 1, :] = decay * n_prev + jnp.sum(kw, axis=0, keepdims=True)
        m_ref[h:h + 1, :] = jnp.broadcast_to(m_new, (1, LANES))

        mu = jnp.mean(hh, axis=-1, keepdims=True)
        hc = hh - mu
        var = jnp.mean(hc * hc, axis=-1, keepdims=True)
        hn = hc * lax.rsqrt(var + LN_EPS) * nw_ref[:, lo:hi]
        h_ref[:, lo:hi] = (jax.nn.sigmoid(og_ref[:, lo:hi]) * hn).astype(BF16)

    @pl.when(ci == pl.num_programs(1) - 1)
    def _():
        c_out[0] = c_ref[...]
        n_out[0] = n_ref[...]
        m_out[0] = m_ref[...]


def _mlstm(qk, v, og, g, c0, n0p, m0p, tail0, conv_w, conv_b, bg, norm_w, *, batch, seq, chunk):
    n, dm = v.shape
    head_dim = dm // M_HEADS
    nc = seq // chunk
    rows = lambda w: pl.BlockSpec((chunk, w), lambda b, c: (b * nc + c, 0))
    per_b = lambda *s: pl.BlockSpec((1,) + s, lambda b, c: (b,) + (0,) * len(s))
    const = lambda a: pl.BlockSpec(a.shape, lambda b, c: (0,) * a.ndim)
    return pl.pallas_call(
        functools.partial(_mlstm_kernel, chunk=chunk, head_dim=head_dim),
        grid=(batch, nc),
        in_specs=[rows(2 * dm), rows(dm), rows(dm), rows(LANES),
                  per_b(M_HEADS, head_dim, head_dim), per_b(SUBLANES, head_dim), per_b(SUBLANES, LANES),
                  per_b(SUBLANES, 2 * dm),
                  const(conv_w), const(conv_b), const(bg), const(norm_w)],
        out_specs=[rows(dm), per_b(M_HEADS, head_dim, head_dim), per_b(SUBLANES, head_dim),
                   per_b(SUBLANES, LANES)],
        out_shape=[jax.ShapeDtypeStruct((n, dm), BF16),
                   jax.ShapeDtypeStruct((batch, M_HEADS, head_dim, head_dim), F32),
                   jax.ShapeDtypeStruct((batch, SUBLANES, head_dim), F32),
                   jax.ShapeDtypeStruct((batch, SUBLANES, LANES), F32)],
        scratch_shapes=[pltpu.VMEM((chunk + SUBLANES, 2 * dm), F32),
                        pltpu.VMEM((M_HEADS, head_dim, head_dim), F32),
                        pltpu.VMEM((SUBLANES, head_dim), F32),
                        pltpu.VMEM((SUBLANES, LANES), F32)],
        compiler_params=_params("arbitrary", "arbitrary"),
        name="mlstm",
    )(qk, v, og, g, c0, n0p, m0p, tail0, conv_w, conv_b, bg, norm_w)


def _s5_kernel(u_ref, sre0_ref, sim0_ref, wb_ref, are_ref, aim_ref, asre_ref, asim_ref,
               wcr_ref, wci_ref, dskip_ref, wglu_ref, bglu_ref, permt_ref,
               hs_ref, sre_out, sim_out,
               xr_ref, xi_ref, hr_ref, hi_ref, *, seg):
    bi = pl.program_id(1)
    n_state = xr_ref.shape[1]
    n_tiles = wb_ref.shape[0]

    @pl.when(bi == 0)
    def _():
        hr_ref[...] = sre0_ref[0]
        hi_ref[...] = sim0_ref[0]

    ub = u_ref[...].astype(BF16)
    tiles_per_slab = LANES // (2 * S5_GROUP)
    row = lax.broadcasted_iota(jnp.int32, (N_STREAMS, SCAN_LANES), 0)
    for sl in range(n_state // SCAN_LANES):
        cols = slice(sl * SCAN_LANES, (sl + 1) * SCAN_LANES)
        for j in range(sl * (SCAN_LANES // LANES), (sl + 1) * (SCAN_LANES // LANES)):
            cs = (j // tiles_per_slab) * LANES
            xj = _dot(ub[:, cs:cs + LANES], wb_ref[j])
            xr_ref[:, j * LANES:(j + 1) * LANES] = xj[:, :LANES]
            xi_ref[:, j * LANES:(j + 1) * LANES] = xj[:, LANES:]
        ar = jnp.broadcast_to(are_ref[:, cols], (N_STREAMS, SCAN_LANES))
        ai = jnp.broadcast_to(aim_ref[:, cols], (N_STREAMS, SCAN_LANES))

        er = xr_ref[0:N_STREAMS, cols]
        ei = xi_ref[0:N_STREAMS, cols]
        for i in range(1, seg):
            rows = slice(i * N_STREAMS, (i + 1) * N_STREAMS)
            nr = ar * er - ai * ei + xr_ref[rows, cols]
            ni = ar * ei + ai * er + xi_ref[rows, cols]
            xr_ref[rows, cols] = nr
            xi_ref[rows, cols] = ni
            er, ei = nr, ni

        asr = asre_ref[:, cols]
        asi = asim_ref[:, cols]
        cr = hr_ref[:, cols]
        ci = hi_ref[:, cols]
        cmr = jnp.zeros((N_STREAMS, SCAN_LANES), F32)
        cmi = jnp.zeros((N_STREAMS, SCAN_LANES), F32)
        for k in range(N_STREAMS):
            cmr = jnp.where(row == k, cr, cmr)
            cmi = jnp.where(row == k, ci, cmi)
            nr = asr * cr - asi * ci + er[k:k + 1, :]
            ni = asr * ci + asi * cr + ei[k:k + 1, :]
            cr, ci = nr, ni
        hr_ref[:, cols] = cr
        hi_ref[:, cols] = ci

        dr, di = cmr, cmi
        for i in range(seg):
            rows = slice(i * N_STREAMS, (i + 1) * N_STREAMS)
            dr, di = ar * dr - ai * di, ar * di + ai * dr
            xr_ref[rows, cols] += dr
            xi_ref[rows, cols] += di

    n_out_tiles = wcr_ref.shape[0]
    kw = n_state // n_out_tiles
    ys = []
    for qt in range(n_out_tiles):
        hr = xr_ref[:, qt * kw:(qt + 1) * kw].astype(BF16)
        hi = xi_ref[:, qt * kw:(qt + 1) * kw].astype(BF16)
        ys.append(_dot(hr, wcr_ref[qt]) + _dot(hi, wci_ref[qt]))
    y = jnp.concatenate(ys, axis=-1) + dskip_ref[...] * u_ref[...]
    gl = 0.5 * y * (1.0 + jnp.tanh(math.sqrt(2.0 / math.pi) * (y + 0.044715 * (y * y * y))))
    z = _dot(gl.astype(BF16), wglu_ref[...]) + bglu_ref[...]
    out = gl * jax.nn.sigmoid(z)
    hs_ref[...] = _dot(permt_ref[...], out.astype(BF16)).astype(BF16)

    @pl.when(bi == pl.num_programs(1) - 1)
    def _():
        sre_out[0] = hr_ref[...]
        sim_out[0] = hi_ref[...]


def _s5(u, sre0, sim0, wb, a_re, a_im, as_re, as_im, wcr, wci, dskip, wglu, bglu, permt,
        *, batch, seq, block):
    n, ds5 = u.shape
    n_state = a_re.shape[1]
    nb = seq // block
    rows = lambda w: pl.BlockSpec((block, w), lambda b, c: (b * nb + c, 0))
    per_b = pl.BlockSpec((1, 1, n_state), lambda b, c: (b, 0, 0))
    const = lambda a: pl.BlockSpec(a.shape, lambda b, c: (0,) * a.ndim)
    return pl.pallas_call(
        functools.partial(_s5_kernel, seg=block // N_STREAMS),
        grid=(batch, nb),
        in_specs=[rows(ds5), per_b, per_b, const(wb), const(a_re), const(a_im), const(as_re),
                  const(as_im), const(wcr), const(wci), const(dskip), const(wglu), const(bglu),
                  const(permt)],
        out_specs=[rows(ds5), per_b, per_b],
        out_shape=[jax.ShapeDtypeStruct((n, ds5), BF16),
                   jax.ShapeDtypeStruct((batch, 1, n_state), F32),
                   jax.ShapeDtypeStruct((batch, 1, n_state), F32)],
        scratch_shapes=[pltpu.VMEM((block, n_state), F32), pltpu.VMEM((block, n_state), F32),
                        pltpu.VMEM((1, n_state), F32), pltpu.VMEM((1, n_state), F32)],
        compiler_params=_params("arbitrary", "arbitrary"),
        name="s5",
    )(u, sre0, sim0, wb, a_re, a_im, as_re, as_im, wcr, wci, dskip, wglu, bglu, permt)


def _layer_norm(x, g, b):
    mu = jnp.mean(x, axis=-1, keepdims=True)
    xc = x - mu
    var = jnp.mean(xc * xc, axis=-1, keepdims=True)
    return xc * lax.rsqrt(var + LN_EPS) * g + b


def _first_max(cols):
    best = cols[0]
    for c in cols[1:]:
        best = jnp.maximum(best, c)
    flags = []
    taken = None
    for c in cols:
        hit = c == best
        if taken is not None:
            hit = jnp.logical_and(hit, jnp.logical_not(taken))
            taken = jnp.logical_or(taken, hit)
        else:
            taken = hit
        flags.append(hit)
    return best, flags


def _outproj_kernel(*refs, alpha, n_valid_blocks):
    x1_ref, route_ref = refs[-2:]
    i = pl.program_id(0)

    @pl.when(i < n_valid_blocks)
    def _():
        _outproj_tile(*refs, alpha=alpha)

    @pl.when(i >= n_valid_blocks)
    def _():
        x1_ref[...] = jnp.zeros_like(x1_ref)
        route_ref[...] = jnp.zeros_like(route_ref)


def _outproj_tile(x_ref, hm_ref, hs_ref, woa_ref, wob_ref, g1_ref, b1_ref, wrh_ref, wrl_ref, br_ref,
                  *rest, alpha):
    x1_ref, route_ref = rest[-2:]
    d = x_ref.shape[1]
    mix = _dot(hm_ref[...], woa_ref[...]) + _dot(hs_ref[...], wob_ref[...])
    x1 = _layer_norm(alpha * x_ref[...] + mix, g1_ref[...], b1_ref[...])
    x1_ref[:, 0:d] = x1

    xh = x1.astype(BF16)
    xl = (x1 - xh.astype(F32)).astype(BF16)
    logits = (_dot(xh, wrh_ref[...]) + _dot(xl, wrh_ref[...]) + _dot(xh, wrl_ref[...])).T + br_ref[...]

    l1 = [logits[g:g + 1, :] for g in range(N_EXPERT_GROUPS)]
    m1, gsel = _first_max(l1)
    denom = l1[0] * 0.0
    for c in l1:
        denom = denom + jnp.exp(c - m1)
    p_g = 1.0 / denom
    gid = l1[0] * 0.0
    l2 = []
    for e in range(EXPERTS_PER_GROUP):
        acc = l1[0] * 0.0
        for g in range(N_EXPERT_GROUPS):
            c0 = N_EXPERT_GROUPS + g * EXPERTS_PER_GROUP + e
            acc = acc + jnp.where(gsel[g], logits[c0:c0 + 1, :], 0.0)
        l2.append(acc)
    for g in range(N_EXPERT_GROUPS):
        gid = gid + jnp.where(gsel[g], float(g), 0.0)

    v1, f1 = _first_max(l2)
    masked = [jnp.where(f1[e], -jnp.inf, l2[e]) for e in range(EXPERTS_PER_GROUP)]
    v2, f2 = _first_max(masked)
    e21 = jnp.exp(v2 - v1)
    w_first = 1.0 / (1.0 + e21)
    w_second = e21 / (1.0 + e21)
    pair_id = l1[0] * 0.0
    gate_a = l1[0] * 0.0
    gate_b = l1[0] * 0.0
    pid = 0
    for a in range(EXPERTS_PER_GROUP):
        for b in range(a + 1, EXPERTS_PER_GROUP):
            ab = jnp.logical_and(f1[a], f2[b])
            ba = jnp.logical_and(f1[b], f2[a])
            pair_id = pair_id + jnp.where(jnp.logical_or(ab, ba), float(pid), 0.0)
            gate_a = gate_a + jnp.where(ab, w_first, 0.0) + jnp.where(ba, w_second, 0.0)
            gate_b = gate_b + jnp.where(ab, w_second, 0.0) + jnp.where(ba, w_first, 0.0)
            pid += 1
    bin_id = gid * float(N_PAIRS) + pair_id
    row = lax.broadcasted_iota(jnp.int32, (LANES, x_ref.shape[0]), 0)
    route_t = jnp.where(row == 0, bin_id,
                        jnp.where(row == 1, p_g * gate_a, jnp.where(row == 2, p_g * gate_b, 0.0)))
    route_ref[...] = route_t[0:SUBLANES, :]
    x1_ref[:, d:d + LANES] = route_t.T


def _outproj(x2, hm, hs, woa, wob, g1, b1, wrh, wrl, br_col, shared, *, alpha, n_total, row_offset):
    n, d = x2.shape
    dm = hm.shape[1]
    blk0 = row_offset // ROW_TILE
    n_blocks = n // ROW_TILE
    grid = n_blocks if shared is not None else n_total // ROW_TILE - blk0
    rows = lambda w: pl.BlockSpec((ROW_TILE, w), lambda i: (jnp.minimum(i, n_blocks - 1), 0))
    kern = functools.partial(_outproj_kernel, alpha=alpha, n_valid_blocks=n_blocks)
    in_specs = [rows(d), rows(dm), rows(hs.shape[1]), _resident(woa.shape), _resident(wob.shape),
                _resident(g1.shape), _resident(b1.shape), _resident(wrh.shape), _resident(wrl.shape),
                _resident(br_col.shape)]
    args = [x2, hm, hs, woa, wob, g1, b1, wrh, wrl, br_col]
    aliases = {}
    if shared is not None:
        in_specs += [pl.BlockSpec(memory_space=pl.ANY), pl.BlockSpec(memory_space=pl.ANY)]
        aliases = {len(args): 0, len(args) + 1: 1}
        args = args + list(shared)
    return pl.pallas_call(
        kern,
        grid=(grid,),
        in_specs=in_specs,
        out_specs=[pl.BlockSpec((ROW_TILE, d + LANES), lambda i: (i + blk0, 0)),
                   pl.BlockSpec((SUBLANES, ROW_TILE), lambda i: (0, i + blk0))],
        out_shape=[jax.ShapeDtypeStruct((n_total, d + LANES), F32),
                   jax.ShapeDtypeStruct((SUBLANES, n_total), F32)],
        input_output_aliases=aliases,
        compiler_params=_params("arbitrary"),
        name="outproj",
    )(*args)


def _moe_kernel(ea_ref, eb_ref, nv_ref, npr_ref,
                idx_ref, idxn_ref, idxp_ref, wga_ref, wua_ref, wda_ref, wgb_ref, wub_ref, wdb_ref,
                g2_ref, b2_ref, x1_hbm,
                yp_hbm, ys_hbm,
                xbuf, obuf, xb_ref, gsem, ssem, *, alpha, n_prompt):
    t = pl.program_id(0)
    nt = pl.num_programs(0)
    slot = t % 2
    tm = xbuf.shape[1]
    d = obuf.shape[2]

    def gather_copy(tok, r, s):
        return pltpu.make_async_copy(x1_hbm.at[pl.ds(tok, 1)], xbuf.at[s, pl.ds(r, 1)], gsem.at[s])

    def scatter_copy(tok, r, s, to_prompt):
        dst = yp_hbm.at[pl.ds(tok, 1)] if to_prompt else ys_hbm.at[pl.ds(tok - n_prompt, 1)]
        return pltpu.make_async_copy(obuf.at[s, pl.ds(r, 1)], dst, ssem.at[s])

    def for_rows(lo, hi, fn):
        n_groups = (hi - lo) // ISSUE_UNROLL

        def group(gi, c):
            base = lo + gi * ISSUE_UNROLL
            for j in range(ISSUE_UNROLL):
                fn(base + j)
            return c

        def single(r, c):
            fn(r)
            return c

        lax.fori_loop(0, n_groups, group, 0)
        lax.fori_loop(lo + n_groups * ISSUE_UNROLL, hi, single, 0)

    def scatter_tile(ids, s, n_prompt_rows, n_rows):
        for_rows(0, n_prompt_rows, lambda r: scatter_copy(ids[0, 0, r], r, s, True).start())
        for_rows(n_prompt_rows, n_rows, lambda r: scatter_copy(ids[0, 0, r], r, s, False).start())

    def wait_rows(src, dst, sem, count):
        bit = xbuf.shape[1]
        while bit >= 1:
            @pl.when((count & bit) != 0)
            def _(bit=bit):
                pltpu.make_async_copy(src.at[pl.ds(0, bit)], dst.at[pl.ds(0, bit)], sem).wait()
            bit //= 2

    n_rows = nv_ref[t]
    mixed = npr_ref[t] != n_rows
    prev = jnp.maximum(t - 1, 0)
    n_prev = jnp.where(jnp.logical_and(t >= 1, npr_ref[prev] == nv_ref[prev]), nv_ref[prev], 0)

    def wait_gather(s):
        pltpu.make_async_copy(x1_hbm.at[pl.ds(0, tm)], xbuf.at[s], gsem.at[s]).wait()

    @pl.when(t == 0)
    def _():
        for_rows(0, tm, lambda r: gather_copy(idx_ref[0, 0, r], r, 0).start())

    @pl.when(jnp.logical_or(t == 0, nv_ref[prev] > 0))
    def _():
        wait_gather(slot)

    @pl.when(t >= 2)
    def _():
        wait_rows(obuf.at[slot], yp_hbm, ssem.at[slot], nv_ref[t - 2])

    @pl.when(n_rows > 0)
    def _():
        xb_ref[...] = xbuf[slot, :, 0:d].astype(BF16)
        n_stages = 8

        def issue_neighbours(stage):
            for r in range(stage * tm // n_stages, (stage + 1) * tm // n_stages):
                gather_copy(idxn_ref[0, 0, r], r, 1 - slot).start()

                @pl.when(r < n_prev)
                def _(r=r):
                    scatter_copy(idxp_ref[0, 0, r], r, 1 - slot, True).start()

        gate_a = xbuf[slot, :, d + 1:d + 2]
        gate_b = xbuf[slot, :, d + 2:d + 3]
        xb = xb_ref[...]
        issue_neighbours(0)
        ha = _dot(xb, wga_ref[0])
        issue_neighbours(1)
        ha = (ha * jax.nn.sigmoid(ha)) * _dot(xb, wua_ref[0])
        issue_neighbours(2)
        ya = _dot(ha.astype(BF16), wda_ref[0])
        issue_neighbours(3)
        hb = _dot(xb, wgb_ref[0])
        issue_neighbours(4)
        hb = (hb * jax.nn.sigmoid(hb)) * _dot(xb, wub_ref[0])
        issue_neighbours(5)
        yb = _dot(hb.astype(BF16), wdb_ref[0])
        issue_neighbours(6)
        moe = gate_a * ya + gate_b * yb
        issue_neighbours(7)
        obuf[slot] = _layer_norm(alpha * xbuf[slot, :, 0:d] + moe, g2_ref[...], b2_ref[...])

    @pl.when(n_rows == 0)
    def _():
        scatter_tile(idxp_ref, 1 - slot, n_prev, n_prev)

    now = jnp.logical_or(mixed, t == nt - 1)
    scatter_tile(idx_ref, slot, jnp.where(now, npr_ref[t], 0), jnp.where(now, n_rows, 0))

    @pl.when(t == nt - 1)
    def _():
        @pl.when(n_rows > 0)
        def _():
            wait_gather(1 - slot)

        wait_rows(obuf.at[slot], yp_hbm, ssem.at[slot], nv_ref[t])

        @pl.when(t >= 1)
        def _():
            wait_rows(obuf.at[1 - slot], yp_hbm, ssem.at[1 - slot], nv_ref[t - 1])


def _moe(x1_all, route_all, wg, wu, wd, g2, b2, *, alpha, n_prompt):
    n = x1_all.shape[0]
    d = x1_all.shape[1] - LANES
    n_p = n_prompt
    n_s = n - n_p
    tm = MOE_TILE
    nt = n // tm + N_BINS if n % tm == 0 else (n + N_BINS * (tm - 1)) // tm + 1
    bins = route_all[0].astype(jnp.int32)

    onehot = (bins[:, None] == jnp.arange(N_BINS, dtype=jnp.int32)[None, :]).astype(jnp.int32)
    csum = jnp.cumsum(onehot, axis=0)
    rank = jnp.sum(csum * onehot, axis=1) - 1
    cnt = csum[n - 1]
    cnt_p = csum[n_p - 1]
    tiles_b = (cnt + tm - 1) // tm
    tile_end = jnp.cumsum(tiles_b)
    tile_start = tile_end - tiles_b
    slot_of_token = jnp.sum(onehot * tile_start[None, :], axis=1) * tm + rank
    idx = jnp.zeros((nt * tm,), jnp.int32).at[slot_of_token].set(
        jnp.arange(n, dtype=jnp.int32), unique_indices=True).reshape(nt, tm)
    tid = jnp.arange(nt, dtype=jnp.int32)
    used = tid < tile_end[N_BINS - 1]
    tbin = jnp.minimum(jnp.sum((tid[:, None] >= tile_end[None, :]).astype(jnp.int32), axis=1), N_BINS - 1)
    last_bin = jnp.max(jnp.where(cnt > 0, jnp.arange(N_BINS, dtype=jnp.int32), 0))
    tbin = jnp.where(used, tbin, last_bin)
    tile_onehot = (tbin[:, None] == jnp.arange(N_BINS, dtype=jnp.int32)[None, :]).astype(jnp.int32)
    lookup = lambda table: jnp.sum(tile_onehot * table[None, :], axis=1)
    kk = tid - lookup(tile_start)
    nvalid = jnp.where(used, jnp.clip(lookup(cnt) - kk * tm, 0, tm), 0).astype(jnp.int32)
    nprompt = jnp.where(used, jnp.clip(lookup(cnt_p) - kk * tm, 0, nvalid), 0).astype(jnp.int32)
    pairs = [(a, b) for a in range(EXPERTS_PER_GROUP) for b in range(a + 1, EXPERTS_PER_GROUP)]
    bin_group = np.arange(N_BINS) // N_PAIRS
    ea = lookup(jnp.asarray(bin_group * EXPERTS_PER_GROUP + np.array([p[0] for p in pairs] * N_EXPERT_GROUPS),
                            jnp.int32))
    eb = lookup(jnp.asarray(bin_group * EXPERTS_PER_GROUP + np.array([p[1] for p in pairs] * N_EXPERT_GROUPS),
                            jnp.int32))
    idx3 = idx.reshape(nt, 1, tm)

    dff = wg.shape[2]
    wspec_in = lambda sel: pl.BlockSpec((1, d, dff), lambda t, ea, eb, nv, npr: (sel(ea, eb)[t], 0, 0))
    wspec_out = lambda sel: pl.BlockSpec((1, dff, d), lambda t, ea, eb, nv, npr: (sel(ea, eb)[t], 0, 0))
    first = lambda a, b: a
    second = lambda a, b: b
    const2 = lambda a: pl.BlockSpec(a.shape, lambda t, *_: (0,) * a.ndim)
    smem_rows = lambda fn: pl.BlockSpec((1, 1, tm), fn, memory_space=pltpu.SMEM)
    grid_spec = pltpu.PrefetchScalarGridSpec(
        num_scalar_prefetch=4,
        grid=(nt,),
        in_specs=[smem_rows(lambda t, *_: (t, 0, 0)),
                  smem_rows(lambda t, *_: (jnp.minimum(t + 1, nt - 1), 0, 0)),
                  smem_rows(lambda t, *_: (jnp.maximum(t - 1, 0), 0, 0)),
                  wspec_in(first), wspec_in(first), wspec_out(first),
                  wspec_in(second), wspec_in(second), wspec_out(second),
                  const2(g2), const2(b2),
                  pl.BlockSpec(memory_space=pl.ANY)],
        out_specs=[pl.BlockSpec(memory_space=pl.ANY), pl.BlockSpec(memory_space=pl.ANY)],
        scratch_shapes=[pltpu.VMEM((2, tm, d + LANES), F32), pltpu.VMEM((2, tm, d), F32),
                        pltpu.VMEM((tm, d), BF16),
                        pltpu.SemaphoreType.DMA((2,)), pltpu.SemaphoreType.DMA((2,))],
    )
    return pl.pallas_call(
        functools.partial(_moe_kernel, alpha=alpha, n_prompt=n_p),
        grid_spec=grid_spec,
        out_shape=[jax.ShapeDtypeStruct((n_p, d), F32), jax.ShapeDtypeStruct((n_s, d), F32)],
        compiler_params=_params("arbitrary"),
        name="moe",
    )(ea, eb, nvalid, nprompt, idx3, idx3, idx3, wg, wu, wd, wg, wu, wd, g2, b2, x1_all)


def _s5_tables(lam_re, lam_im, log_dt, b_re, b_im, c_re, c_im, seg_lens):
    n_groups, n_state = lam_re.shape
    dt = jnp.exp(log_dt.astype(F32))[:, None]
    lam_re = lam_re.astype(F32)
    lam_im = lam_im.astype(F32)
    mag = jnp.exp(lam_re * dt)
    a_re = mag * jnp.cos(lam_im * dt)
    a_im = mag * jnp.sin(lam_im * dt)
    e_re = a_re - 1.0
    e_im = a_im
    lam_sq = lam_re * lam_re + lam_im * lam_im
    coef_re = (e_re * lam_re + e_im * lam_im) / lam_sq
    coef_im = (e_im * lam_re - e_re * lam_im) / lam_sq
    b_re = b_re.astype(F32)
    b_im = b_im.astype(F32)
    bb_re = coef_re[..., None] * b_re - coef_im[..., None] * b_im
    bb_im = coef_re[..., None] * b_im + coef_im[..., None] * b_re

    gpt = LANES // n_state
    n_tiles = n_groups // gpt
    gps = LANES // S5_GROUP
    g_in_slab = (jnp.arange(n_groups) % gps)
    rows_onehot = jax.nn.one_hot(g_in_slab, gps, dtype=F32)
    w_re = jnp.einsum("gs,gpc->gscp", rows_onehot, bb_re).reshape(n_groups, LANES, n_state)
    w_im = jnp.einsum("gs,gpc->gscp", rows_onehot, bb_im).reshape(n_groups, LANES, n_state)
    w_re = w_re.reshape(n_tiles, gpt, LANES, n_state).transpose(0, 2, 1, 3).reshape(n_tiles, LANES, LANES)
    w_im = w_im.reshape(n_tiles, gpt, LANES, n_state).transpose(0, 2, 1, 3).reshape(n_tiles, LANES, LANES)
    wb = jnp.concatenate([w_re, w_im], axis=-1).astype(BF16)

    n_flat = n_groups * n_state
    out_w = 2 * LANES
    g_per_out = out_w // S5_GROUP
    n_out_tiles = n_groups // g_per_out
    oh = jax.nn.one_hot(jnp.arange(n_groups) % g_per_out, g_per_out, dtype=F32)
    wcr = jnp.einsum("gs,gcp->gpsc", oh, c_re.astype(F32)).reshape(n_out_tiles, g_per_out * n_state, out_w)
    wci = jnp.einsum("gs,gcp->gpsc", oh, -c_im.astype(F32)).reshape(n_out_tiles, g_per_out * n_state, out_w)

    def powers(k):
        return ((mag ** k) * jnp.cos(lam_im * dt * k)).reshape(1, n_flat), \
               ((mag ** k) * jnp.sin(lam_im * dt * k)).reshape(1, n_flat)

    return (wb, a_re.reshape(1, n_flat), a_im.reshape(1, n_flat), [powers(float(k)) for k in seg_lens],
            wcr.astype(BF16), wci.astype(BF16))


def _layer(x_p, x_s, st, lp):
    (w_in, b_gates, conv_w, conv_b, norm_w, lam_re, lam_im, log_dt, b_re, b_im, c_re, c_im,
     d_skip, w_glu, b_glu, w_out, ln1_g, ln1_b, w_r1, b_r1, w_r2, b_r2, w_gate, w_up, w_down,
     ln2_g, ln2_b, alpha) = lp
    state_c, state_n, state_m, state_conv, state_re, state_im = st
    bp, tp, d = x_p.shape
    bs, ts, _ = x_s.shape
    dm = norm_w.shape[0]
    head_dim = dm // M_HEADS
    ds5 = d_skip.shape[0]
    n_groups, n_state = lam_re.shape
    n_flat = n_groups * n_state

    o0 = 2 * dm
    o1 = o0 + 2 * dm
    wqk = w_in[:, :o0].astype(BF16)
    wv = w_in[:, o0:o0 + dm].astype(BF16)
    wo = w_in[:, o0 + dm:o1].astype(BF16)
    wg = jnp.pad(w_in[:, o1:o1 + 2 * M_HEADS], ((0, 0), (0, LANES - 2 * M_HEADS))).astype(BF16)
    wu = w_in[:, o1 + 2 * M_HEADS:].astype(BF16)
    bg = jnp.pad(b_gates.astype(F32), (0, LANES - 2 * M_HEADS)).reshape(1, LANES)
    woa = w_out[:dm].astype(BF16)
    wob = w_out[dm:].astype(BF16)
    wr = jnp.concatenate([w_r1, jnp.transpose(w_r2, (1, 0, 2)).reshape(d, -1)], axis=1).astype(F32)
    n_logits = wr.shape[1]
    wr = jnp.pad(wr, ((0, 0), (0, LANES - n_logits)))
    wrh = wr.astype(BF16)
    wrl = (wr - wrh.astype(F32)).astype(BF16)
    br = jnp.pad(jnp.concatenate([b_r1, b_r2.reshape(-1)]).astype(F32), (0, LANES - n_logits))
    br = br.reshape(LANES, 1)

    blk_p = min(tp, ROW_TILE)
    blk_s = min(ts, ROW_TILE)
    seg_p = blk_p // N_STREAMS
    seg_s = blk_s // N_STREAMS
    wb, a_re, a_im, (as_p, as_s), wcr, wci = _s5_tables(lam_re, lam_im, log_dt, b_re, b_im, c_re, c_im,
                                                         (seg_p, seg_s))

    outs = []
    shared = None
    row_offset = 0
    for x, blk, seg, a_seg, zero_state in ((x_p, blk_p, seg_p, as_p, True), (x_s, blk_s, seg_s, as_s, False)):
        b, t, _ = x.shape
        x2 = x.reshape(b * t, d)
        perm_np = _stream_perm(blk, seg)
        perm = jnp.asarray(perm_np, BF16)
        permt = jnp.asarray(perm_np[:blk, :blk].T, BF16)
        qk, v, og, u, g = _inproj(x2, perm, wqk, wv, wo, wu, wg)

        if zero_state:
            c0 = jnp.zeros((b, M_HEADS, head_dim, head_dim), F32)
            n0 = jnp.zeros((b, M_HEADS, head_dim), F32)
            m0 = jnp.zeros((b, M_HEADS), F32)
            conv0 = jnp.zeros((b, CONV_W - 1, 2 * dm), F32)
            re0 = jnp.zeros((b, n_groups, n_state), F32)
            im0 = jnp.zeros((b, n_groups, n_state), F32)
        else:
            c0, n0, m0, conv0, re0, im0 = (state_c.astype(F32), state_n.astype(F32), state_m.astype(F32),
                                           state_conv.astype(F32), state_re.astype(F32), state_im.astype(F32))
        n0p = jnp.pad(n0, ((0, 0), (0, SUBLANES - M_HEADS), (0, 0)))
        m0p = jnp.broadcast_to(jnp.pad(m0, ((0, 0), (0, SUBLANES - M_HEADS)))[:, :, None], (b, SUBLANES, LANES))
        tail0 = jnp.pad(conv0, ((0, 0), (SUBLANES - (CONV_W - 1), 0), (0, 0)))
        chunk = min(t, LANES)
        hm, c_new, n_new, m_new = _mlstm(qk, v, og, g, c0, n0p, m0p, tail0, conv_w.astype(F32),
                                         conv_b.astype(F32).reshape(1, -1), bg, norm_w.astype(F32).reshape(1, -1),
                                         batch=b, seq=t, chunk=chunk)
        hs, re_new, im_new = _s5(u, re0.reshape(b, 1, n_flat), im0.reshape(b, 1, n_flat), wb, a_re, a_im,
                                 a_seg[0], a_seg[1], wcr, wci, d_skip.astype(F32).reshape(1, -1),
                                 w_glu.astype(BF16), b_glu.astype(F32).reshape(1, -1), permt,
                                 batch=b, seq=t, block=blk)
        shared = _outproj(x2, hm, hs, woa, wob, ln1_g.astype(F32).reshape(1, -1),
                          ln1_b.astype(F32).reshape(1, -1), wrh, wrl, br, shared, alpha=alpha,
                          n_total=bp * tp + bs * ts, row_offset=row_offset)
        row_offset += b * t
        conv_new = jnp.concatenate([conv0, qk.reshape(b, t, 2 * dm)], axis=1)[:, -(CONV_W - 1):]
        states = (c_new, n_new[:, :M_HEADS], m_new[:, :M_HEADS, 0], conv_new,
                  re_new.reshape(b, n_groups, n_state), im_new.reshape(b, n_groups, n_state))
        outs.append(states)

    st_p, st_s = outs
    x1_all, route_all = shared
    yp, ys = _moe(x1_all, route_all, w_gate.astype(BF16), w_up.astype(BF16), w_down.astype(BF16),
                  ln2_g.astype(F32).reshape(1, -1), ln2_b.astype(F32).reshape(1, -1), alpha=alpha,
                  n_prompt=bp * tp)
    return yp.reshape(bp, tp, d), ys.reshape(bs, ts, d), st_p, st_s


def kernel(x_prompt, x_sample, state_mlstm_C, state_mlstm_n, state_mlstm_m, state_conv, state_s5_re, state_s5_im, w_in, b_gates, conv_w, conv_b, mlstm_norm_w, s5_lam_re, s5_lam_im, s5_log_dt, s5_b_re, s5_b_im, s5_c_re, s5_c_im, s5_d, w_glu, b_glu, w_out, ln1_g, ln1_b, w_r1, b_r1, w_r2, b_r2, w_gate, w_up, w_down, ln2_g, ln2_b):
    depth = w_in.shape[0]
    alpha = (2 * depth) ** 0.25
    yp, ys = x_prompt, x_sample
    sts_p, sts_s = [], []
    for l in range(depth):
        lp = (w_in[l], b_gates[l], conv_w[l], conv_b[l], mlstm_norm_w[l], s5_lam_re[l], s5_lam_im[l],
              s5_log_dt[l], s5_b_re[l], s5_b_im[l], s5_c_re[l], s5_c_im[l], s5_d[l], w_glu[l], b_glu[l],
              w_out[l], ln1_g[l], ln1_b[l], w_r1[l], b_r1[l], w_r2[l], b_r2[l], w_gate[l], w_up[l],
              w_down[l], ln2_g[l], ln2_b[l], alpha)
        st = (state_mlstm_C[l], state_mlstm_n[l], state_mlstm_m[l], state_conv[l], state_s5_re[l],
              state_s5_im[l])
        yp, ys, sp, ss = _layer(yp, ys, st, lp)
        sts_p.append(sp)
        sts_s.append(ss)
    stack = lambda sts, i: jnp.stack([s[i] for s in sts])
    return (yp, ys) + tuple(stack(sts_p, i) for i in range(6)) + tuple(stack(sts_s, i) for i in range(6))
```

```python
import functools
import math

import jax
import jax.numpy as jnp
import numpy as np
from jax import lax
from jax.experimental import pallas as pl
from jax.experimental.pallas import tpu as pltpu

F32 = jnp.float32
BF16 = jnp.bfloat16

LANES = 128
SUBLANES = 8
VMEM_LIMIT = 56 * 1024 * 1024

M_HEADS = 4
CONV_W = 4
S5_GROUP = 16
S5_STATE = 64
N_EXPERT_GROUPS = 4
EXPERTS_PER_GROUP = 4
N_PAIRS = 6
N_BINS = N_EXPERT_GROUPS * N_PAIRS
LN_EPS = 1e-5
NEG = -1e30

ROW_TILE = 256
MOE_TILE = 256
ISSUE_UNROLL = 8
N_STREAMS = SUBLANES
SCAN_LANES = 512


def _dot(a, b):
    return jnp.dot(a, b, preferred_element_type=F32)


def _dot_nt(a, b):
    return lax.dot_general(a, b, (((1,), (1,)), ((), ())), preferred_element_type=F32)


def _resident(shape):
    nd = len(shape)
    return pl.BlockSpec(shape, lambda *_: (0,) * nd, pipeline_mode=pl.Buffered(1))


def _params(*sem):
    return pltpu.CompilerParams(dimension_semantics=sem, vmem_limit_bytes=VMEM_LIMIT)


def _token_pitch(d):
    return -(-(d // LANES + 1) // SUBLANES) * SUBLANES


def _stream_perm(block, seg):
    n_seg = block // seg
    r = np.arange(block)
    src = (r % n_seg) * seg + r // n_seg
    p = np.zeros((block, block), np.float32)
    p[r, src] = 1.0
    reps = ROW_TILE // block
    return np.kron(np.eye(reps, dtype=np.float32), p)


def _inproj_kernel(x_ref, perm_ref, wqk_ref, wv_ref, wo_ref, wu_ref, wg_ref,
                   qk_ref, v_ref, og_ref, u_ref, g_ref):
    xb = x_ref[...].astype(BF16)
    qk_ref[...] = _dot(xb, wqk_ref[...])
    v_ref[...] = _dot(xb, wv_ref[...]).astype(BF16)
    og_ref[...] = _dot(xb, wo_ref[...])
    g_ref[...] = _dot(xb, wg_ref[...])
    xp = _dot(perm_ref[...], xb).astype(BF16)
    u_ref[...] = _dot(xp, wu_ref[...])


def _inproj(x2, perm, wqk, wv, wo, wu, wg):
    n, d = x2.shape
    dm = wv.shape[1]
    ds5 = wu.shape[1]
    rows = lambda w: pl.BlockSpec((ROW_TILE, w), lambda i: (i, 0))
    return pl.pallas_call(
        _inproj_kernel,
        grid=(n // ROW_TILE,),
        in_specs=[rows(d), _resident(perm.shape), _resident(wqk.shape), _resident(wv.shape),
                  _resident(wo.shape), _resident(wu.shape), _resident(wg.shape)],
        out_specs=[rows(2 * dm), rows(dm), rows(dm), rows(ds5), rows(LANES)],
        out_shape=[jax.ShapeDtypeStruct((n, 2 * dm), F32), jax.ShapeDtypeStruct((n, dm), BF16),
                   jax.ShapeDtypeStruct((n, dm), F32), jax.ShapeDtypeStruct((n, ds5), F32),
                   jax.ShapeDtypeStruct((n, LANES), F32)],
        compiler_params=_params("arbitrary"),
        name="inproj",
    )(x2, perm, wqk, wv, wo, wu, wg)


def _cumsum_rows(x):
    n = x.shape[0]
    row = lax.broadcasted_iota(jnp.int32, x.shape, 0)
    s = 1
    while s < n:
        x = x + jnp.where(row >= s, pltpu.roll(x, s, 0), 0.0)
        s *= 2
    return x


def _transpose_gate_cols(x):
    sel = (lax.broadcasted_iota(jnp.int32, (SUBLANES, LANES), 0)
           == lax.broadcasted_iota(jnp.int32, (SUBLANES, LANES), 1)).astype(BF16)
    hi = x.astype(BF16)
    r1 = x - hi.astype(F32)
    mid = r1.astype(BF16)
    lo = (r1 - mid.astype(F32)).astype(BF16)
    return _dot_nt(sel, hi) + _dot_nt(sel, mid) + _dot_nt(sel, lo)


def _mlstm_kernel(qk_ref, v_ref, og_ref, g_ref, c0_ref, n0_ref, m0_ref, tail0_ref,
                  cw_ref, cb_ref, bg_ref, nw_ref,
                  h_ref, c_out, n_out, m_out,
                  ext_ref, c_ref, n_ref, m_ref, *, chunk, head_dim):
    ci = pl.program_id(1)
    dm = M_HEADS * head_dim
    L = chunk

    @pl.when(ci == 0)
    def _():
        c_ref[...] = c0_ref[0]
        n_ref[...] = n0_ref[0]
        m_ref[...] = m0_ref[0]
        ext_ref[0:SUBLANES, :] = tail0_ref[0]

    ext_ref[SUBLANES:SUBLANES + L, :] = qk_ref[...]
    full = ext_ref[...]
    acc = cb_ref[...] + full[SUBLANES:, :] * cw_ref[CONV_W - 1:CONV_W, :]
    for back in range(1, CONV_W):
        tap = cw_ref[CONV_W - 1 - back:CONV_W - back, :]
        acc = acc + pltpu.roll(full, back, 0)[SUBLANES:, :] * tap
    tail = ext_ref[L:L + SUBLANES, :]
    ext_ref[0:SUBLANES, :] = tail
    qk = acc * jax.nn.sigmoid(acc)

    gates = g_ref[...] + bg_ref[...]
    fpre = pltpu.roll(gates, LANES - M_HEADS, 1)
    logf = jnp.minimum(fpre, 0.0) - jnp.log1p(jnp.exp(-jnp.abs(fpre)))
    bcum = _cumsum_rows(logf)
    rrow = _transpose_gate_cols(gates - bcum)
    tri = (lax.broadcasted_iota(jnp.int32, (L, L), 0) >= lax.broadcasted_iota(jnp.int32, (L, L), 1))

    for h in range(M_HEADS):
        lo, hi = h * head_dim, (h + 1) * head_dim
        q = qk[:, lo:hi]
        k = qk[:, dm + lo:dm + hi] * (head_dim ** -0.5)
        v = v_ref[:, lo:hi]
        qb = q.astype(BF16)
        kb = k.astype(BF16)
        b_col = bcum[:, h:h + 1]
        ig_col = gates[:, h:h + 1]
        m_prev = m_ref[h:h + 1, 0:1]
        c_prev = c_ref[h]
        n_prev = n_ref[h:h + 1, :]

        dmat = jnp.where(tri, b_col + rrow[h:h + 1, :], NEG)
        inter = b_col + m_prev
        m_t = jnp.maximum(inter, jnp.max(dmat, axis=-1, keepdims=True))
        w_intra = jnp.exp(dmat - m_t)
        w_inter = jnp.exp(inter - m_t)
        s = _dot_nt(qb, kb) * w_intra
        num = w_inter * _dot(qb, c_prev.astype(BF16)) + _dot(s.astype(BF16), v)
        den = w_inter * jnp.sum(q * n_prev, axis=-1, keepdims=True) + jnp.sum(s, axis=-1, keepdims=True)
        hh = num / jnp.maximum(jnp.abs(den), jnp.exp(-m_t))

        m_new = m_t[L - 1:L, :]
        b_last = b_col[L - 1:L, :]
        w_s = jnp.exp(b_last - b_col + ig_col - m_new)
        decay = jnp.exp(b_last + m_prev - m_new)
        kw = k * w_s
        c_ref[h] = decay * c_prev + _dot(kw.T.astype(BF16), v)
        n_ref[h:h + 1, :] = decay * n_prev + jnp.sum(kw, axis=0, keepdims=True)
        m_ref[h:h + 1, :] = jnp.broadcast_to(m_new, (1, LANES))

        mu = jnp.mean(hh, axis=-1, keepdims=True)
        hc = hh - mu
        var = jnp.mean(hc * hc, axis=-1, keepdims=True)
        hn = hc * lax.rsqrt(var + LN_EPS) * nw_ref[:, lo:hi]
        h_ref[:, lo:hi] = (jax.nn.sigmoid(og_ref[:, lo:hi]) * hn).astype(BF16)

    @pl.when(ci == pl.num_programs(1) - 1)
    def _():
        c_out[0] = c_ref[...]
        n_out[0] = n_ref[...]
        m_out[0] = m_ref[...]


def _mlstm(qk, v, og, g, c0, n0p, m0p, tail0, conv_w, conv_b, bg, norm_w, *, batch, seq, chunk):
    n, dm = v.shape
    head_dim = dm // M_HEADS
    nc = seq // chunk
    rows = lambda w: pl.BlockSpec((chunk, w), lambda b, c: (b * nc + c, 0))
    per_b = lambda *s: pl.BlockSpec((1,) + s, lambda b, c: (b,) + (0,) * len(s))
    const = lambda a: pl.BlockSpec(a.shape, lambda b, c: (0,) * a.ndim)
    return pl.pallas_call(
        functools.partial(_mlstm_kernel, chunk=chunk, head_dim=head_dim),
        grid=(batch, nc),
        in_specs=[rows(2 * dm), rows(dm), rows(dm), rows(LANES),
                  per_b(M_HEADS, head_dim, head_dim), per_b(SUBLANES, head_dim), per_b(SUBLANES, LANES),
                  per_b(SUBLANES, 2 * dm),
                  const(conv_w), const(conv_b), const(bg), const(norm_w)],
        out_specs=[rows(dm), per_b(M_HEADS, head_dim, head_dim), per_b(SUBLANES, head_dim),
                   per_b(SUBLANES, LANES)],
        out_shape=[jax.ShapeDtypeStruct((n, dm), BF16),
                   jax.ShapeDtypeStruct((batch, M_HEADS, head_dim, head_dim), F32),
                   jax.ShapeDtypeStruct((batch, SUBLANES, head_dim), F32),
                   jax.ShapeDtypeStruct((batch, SUBLANES, LANES), F32)],
        scratch_shapes=[pltpu.VMEM((chunk + SUBLANES, 2 * dm), F32),
                        pltpu.VMEM((M_HEADS, head_dim, head_dim), F32),
                        pltpu.VMEM((SUBLANES, head_dim), F32),
                        pltpu.VMEM((SUBLANES, LANES), F32)],
        compiler_params=_params("arbitrary", "arbitrary"),
        name="mlstm",
    )(qk, v, og, g, c0, n0p, m0p, tail0, conv_w, conv_b, bg, norm_w)


def _s5_kernel(u_ref, sre0_ref, sim0_ref, wb_ref, are_ref, aim_ref, asre_ref, asim_ref,
               wcr_ref, wci_ref, dskip_ref, wglu_ref, bglu_ref, permt_ref,
               hs_ref, sre_out, sim_out,
               xr_ref, xi_ref, hr_ref, hi_ref, *, seg):
    bi = pl.program_id(1)
    n_state = xr_ref.shape[1]
    n_tiles = wb_ref.shape[0]

    @pl.when(bi == 0)
    def _():
        hr_ref[...] = sre0_ref[0]
        hi_ref[...] = sim0_ref[0]

    ub = u_ref[...].astype(BF16)
    tiles_per_slab = LANES // (2 * S5_GROUP)
    row = lax.broadcasted_iota(jnp.int32, (N_STREAMS, SCAN_LANES), 0)
    for sl in range(n_state // SCAN_LANES):
        cols = slice(sl * SCAN_LANES, (sl + 1) * SCAN_LANES)
        for j in range(sl * (SCAN_LANES // LANES), (sl + 1) * (SCAN_LANES // LANES)):
            cs = (j // tiles_per_slab) * LANES
            xj = _dot(ub[:, cs:cs + LANES], wb_ref[j])
            xr_ref[:, j * LANES:(j + 1) * LANES] = xj[:, :LANES]
            xi_ref[:, j * LANES:(j + 1) * LANES] = xj[:, LANES:]
        ar = jnp.broadcast_to(are_ref[:, cols], (N_STREAMS, SCAN_LANES))
        ai = jnp.broadcast_to(aim_ref[:, cols], (N_STREAMS, SCAN_LANES))

        er = xr_ref[0:N_STREAMS, cols]
        ei = xi_ref[0:N_STREAMS, cols]
        for i in range(1, seg):
            rows = slice(i * N_STREAMS, (i + 1) * N_STREAMS)
            nr = ar * er - ai * ei + xr_ref[rows, cols]
            ni = ar * ei + ai * er + xi_ref[rows, cols]
            xr_ref[rows, cols] = nr
            xi_ref[rows, cols] = ni
            er, ei = nr, ni

        asr = asre_ref[:, cols]
        asi = asim_ref[:, cols]
        cr = hr_ref[:, cols]
        ci = hi_ref[:, cols]
        cmr = jnp.zeros((N_STREAMS, SCAN_LANES), F32)
        cmi = jnp.zeros((N_STREAMS, SCAN_LANES), F32)
        for k in range(N_STREAMS):
            cmr = jnp.where(row == k, cr, cmr)
            cmi = jnp.where(row == k, ci, cmi)
            nr = asr * cr - asi * ci + er[k:k + 1, :]
            ni = asr * ci + asi * cr + ei[k:k + 1, :]
            cr, ci = nr, ni
        hr_ref[:, cols] = cr
        hi_ref[:, cols] = ci

        dr, di = cmr, cmi
        for i in range(seg):
            rows = slice(i * N_STREAMS, (i + 1) * N_STREAMS)
            dr, di = ar * dr - ai * di, ar * di + ai * dr
            xr_ref[rows, cols] += dr
            xi_ref[rows, cols] += di

    n_out_tiles = wcr_ref.shape[0]
    kw = n_state // n_out_tiles
    ys = []
    for qt in range(n_out_tiles):
        hr = xr_ref[:, qt * kw:(qt + 1) * kw].astype(BF16)
        hi = xi_ref[:, qt * kw:(qt + 1) * kw].astype(BF16)
        ys.append(_dot(hr, wcr_ref[qt]) + _dot(hi, wci_ref[qt]))
    y = jnp.concatenate(ys, axis=-1) + dskip_ref[...] * u_ref[...]
    gl = 0.5 * y * (1.0 + jnp.tanh(math.sqrt(2.0 / math.pi) * (y + 0.044715 * (y * y * y))))
    z = _dot(gl.astype(BF16), wglu_ref[...]) + bglu_ref[...]
    out = gl * jax.nn.sigmoid(z)
    hs_ref[...] = _dot(permt_ref[...], out.astype(BF16)).astype(BF16)

    @pl.when(bi == pl.num_programs(1) - 1)
    def _():
        sre_out[0] = hr_ref[...]
        sim_out[0] = hi_ref[...]


def _s5(u, sre0, sim0, wb, a_re, a_im, as_re, as_im, wcr, wci, dskip, wglu, bglu, permt,
        *, batch, seq, block):
    n, ds5 = u.shape
    n_state = a_re.shape[1]
    nb = seq // block
    rows = lambda w: pl.BlockSpec((block, w), lambda b, c: (b * nb + c, 0))
    per_b = pl.BlockSpec((1, 1, n_state), lambda b, c: (b, 0, 0))
    const = lambda a: pl.BlockSpec(a.shape, lambda b, c: (0,) * a.ndim)
    return pl.pallas_call(
        functools.partial(_s5_kernel, seg=block // N_STREAMS),
        grid=(batch, nb),
        in_specs=[rows(ds5), per_b, per_b, const(wb), const(a_re), const(a_im), const(as_re),
                  const(as_im), const(wcr), const(wci), const(dskip), const(wglu), const(bglu),
                  const(permt)],
        out_specs=[rows(ds5), per_b, per_b],
        out_shape=[jax.ShapeDtypeStruct((n, ds5), BF16),
                   jax.ShapeDtypeStruct((batch, 1, n_state), F32),
                   jax.ShapeDtypeStruct((batch, 1, n_state), F32)],
        scratch_shapes=[pltpu.VMEM((block, n_state), F32), pltpu.VMEM((block, n_state), F32),
                        pltpu.VMEM((1, n_state), F32), pltpu.VMEM((1, n_state), F32)],
        compiler_params=_params("arbitrary", "arbitrary"),
        name="s5",
    )(u, sre0, sim0, wb, a_re, a_im, as_re, as_im, wcr, wci, dskip, wglu, bglu, permt)


def _layer_norm(x, g, b):
    mu = jnp.mean(x, axis=-1, keepdims=True)
    xc = x - mu
    var = jnp.mean(xc * xc, axis=-1, keepdims=True)
    return xc * lax.rsqrt(var + LN_EPS) * g + b


def _first_max(cols):
    best = cols[0]
    for c in cols[1:]:
        best = jnp.maximum(best, c)
    flags = []
    taken = None
    for c in cols:
        hit = c == best
        if taken is not None:
            hit = jnp.logical_and(hit, jnp.logical_not(taken))
            taken = jnp.logical_or(taken, hit)
        else:
            taken = hit
        flags.append(hit)
    return best, flags


def _outproj_kernel(*refs, alpha, n_valid_blocks):
    x1_ref, route_ref = refs[-2:]
    i = pl.program_id(0)

    @pl.when(i < n_valid_blocks)
    def _():
        _outproj_tile(*refs, alpha=alpha)

    @pl.when(i >= n_valid_blocks)
    def _():
        x1_ref[...] = jnp.zeros_like(x1_ref)
        route_ref[...] = jnp.zeros_like(route_ref)


def _outproj_tile(x_ref, hm_ref, hs_ref, woa_ref, wob_ref, g1_ref, b1_ref, wrh_ref, wrl_ref, br_ref,
                  *rest, alpha):
    x1_ref, route_ref = rest[-2:]
    tm, d = x_ref.shape
    pitch = x1_ref.shape[0] // tm

    def store_token_tile(j, v):
        for g in range(tm // SUBLANES):
            x1_ref[pl.ds(g * SUBLANES * pitch + j, SUBLANES, stride=pitch), :] = v[g * SUBLANES:(g + 1) * SUBLANES, :]

    mix = _dot(hm_ref[...], woa_ref[...]) + _dot(hs_ref[...], wob_ref[...])
    x1 = _layer_norm(alpha * x_ref[...] + mix, g1_ref[...], b1_ref[...])
    for j in range(d // LANES):
        store_token_tile(j, x1[:, j * LANES:(j + 1) * LANES])

    xh = x1.astype(BF16)
    xl = (x1 - xh.astype(F32)).astype(BF16)
    logits = (_dot(xh, wrh_ref[...]) + _dot(xl, wrh_ref[...]) + _dot(xh, wrl_ref[...])).T + br_ref[...]

    l1 = [logits[g:g + 1, :] for g in range(N_EXPERT_GROUPS)]
    m1, gsel = _first_max(l1)
    denom = l1[0] * 0.0
    for c in l1:
        denom = denom + jnp.exp(c - m1)
    p_g = 1.0 / denom
    gid = l1[0] * 0.0
    l2 = []
    for e in range(EXPERTS_PER_GROUP):
        acc = l1[0] * 0.0
        for g in range(N_EXPERT_GROUPS):
            c0 = N_EXPERT_GROUPS + g * EXPERTS_PER_GROUP + e
            acc = acc + jnp.where(gsel[g], logits[c0:c0 + 1, :], 0.0)
        l2.append(acc)
    for g in range(N_EXPERT_GROUPS):
        gid = gid + jnp.where(gsel[g], float(g), 0.0)

    v1, f1 = _first_max(l2)
    masked = [jnp.where(f1[e], -jnp.inf, l2[e]) for e in range(EXPERTS_PER_GROUP)]
    v2, f2 = _first_max(masked)
    e21 = jnp.exp(v2 - v1)
    w_first = 1.0 / (1.0 + e21)
    w_second = e21 / (1.0 + e21)
    pair_id = l1[0] * 0.0
    gate_a = l1[0] * 0.0
    gate_b = l1[0] * 0.0
    pid = 0
    for a in range(EXPERTS_PER_GROUP):
        for b in range(a + 1, EXPERTS_PER_GROUP):
            ab = jnp.logical_and(f1[a], f2[b])
            ba = jnp.logical_and(f1[b], f2[a])
            pair_id = pair_id + jnp.where(jnp.logical_or(ab, ba), float(pid), 0.0)
            gate_a = gate_a + jnp.where(ab, w_first, 0.0) + jnp.where(ba, w_second, 0.0)
            gate_b = gate_b + jnp.where(ab, w_second, 0.0) + jnp.where(ba, w_first, 0.0)
            pid += 1
    bin_id = gid * float(N_PAIRS) + pair_id
    row = lax.broadcasted_iota(jnp.int32, (LANES, x_ref.shape[0]), 0)
    route_t = jnp.where(row == 0, bin_id,
                        jnp.where(row == 1, p_g * gate_a, jnp.where(row == 2, p_g * gate_b, 0.0)))
    route_ref[...] = route_t[0:SUBLANES, :]
    store_token_tile(d // LANES, route_t.T)
    for j in range(d // LANES + 1, pitch):
        store_token_tile(j, jnp.zeros((tm, LANES), F32))


def _outproj(x2, hm, hs, woa, wob, g1, b1, wrh, wrl, br_col, shared, *, alpha, n_total, row_offset):
    n, d = x2.shape
    dm = hm.shape[1]
    blk0 = row_offset // ROW_TILE
    n_blocks = n // ROW_TILE
    pitch = _token_pitch(d)
    grid = n_blocks if shared is not None else n_total // ROW_TILE - blk0
    rows = lambda w: pl.BlockSpec((ROW_TILE, w), lambda i: (jnp.minimum(i, n_blocks - 1), 0))
    kern = functools.partial(_outproj_kernel, alpha=alpha, n_valid_blocks=n_blocks)
    in_specs = [rows(d), rows(dm), rows(hs.shape[1]), _resident(woa.shape), _resident(wob.shape),
                _resident(g1.shape), _resident(b1.shape), _resident(wrh.shape), _resident(wrl.shape),
                _resident(br_col.shape)]
    args = [x2, hm, hs, woa, wob, g1, b1, wrh, wrl, br_col]
    aliases = {}
    if shared is not None:
        in_specs += [pl.BlockSpec(memory_space=pl.ANY), pl.BlockSpec(memory_space=pl.ANY)]
        aliases = {len(args): 0, len(args) + 1: 1}
        args = args + list(shared)
    return pl.pallas_call(
        kern,
        grid=(grid,),
        in_specs=in_specs,
        out_specs=[pl.BlockSpec((ROW_TILE * pitch, LANES), lambda i: (i + blk0, 0)),
                   pl.BlockSpec((SUBLANES, ROW_TILE), lambda i: (0, i + blk0))],
        out_shape=[jax.ShapeDtypeStruct((n_total * pitch, LANES), F32),
                   jax.ShapeDtypeStruct((SUBLANES, n_total), F32)],
        input_output_aliases=aliases,
        compiler_params=_params("arbitrary"),
        name="outproj",
    )(*args)


def _moe_kernel(ea_ref, eb_ref, nv_ref, npr_ref,
                idx_ref, idxn_ref, idxp_ref, wga_ref, wua_ref, wda_ref, wgb_ref, wub_ref, wdb_ref,
                g2_ref, b2_ref, x1_hbm,
                yp_hbm, ys_hbm,
                xbuf, obuf, xb_ref, gsem, ssem, *, alpha, n_prompt):
    t = pl.program_id(0)
    nt = pl.num_programs(0)
    slot = t % 2
    tm, d = obuf.shape[1:]
    pitch = xbuf.shape[2]
    n_model_tiles = d // LANES

    def gather_copy(tok, r, s):
        src = x1_hbm.at[pl.ds(pl.multiple_of(tok * pitch, SUBLANES), pitch), :]
        return pltpu.make_async_copy(src, xbuf.at[s, r // SUBLANES, :, r % SUBLANES, :], gsem.at[s])

    def tile_lanes(j):
        return xbuf[slot, :, j].reshape(tm, LANES)

    def scatter_copy(tok, r, s, to_prompt):
        dst = yp_hbm.at[pl.ds(tok, 1)] if to_prompt else ys_hbm.at[pl.ds(tok - n_prompt, 1)]
        return pltpu.make_async_copy(obuf.at[s, pl.ds(r, 1)], dst, ssem.at[s])

    def for_rows(lo, hi, fn):
        n_groups = (hi - lo) // ISSUE_UNROLL

        def group(gi, c):
            base = lo + gi * ISSUE_UNROLL
            for j in range(ISSUE_UNROLL):
                fn(base + j)
            return c

        def single(r, c):
            fn(r)
            return c

        lax.fori_loop(0, n_groups, group, 0)
        lax.fori_loop(lo + n_groups * ISSUE_UNROLL, hi, single, 0)

    def scatter_tile(ids, s, n_prompt_rows, n_rows):
        for_rows(0, n_prompt_rows, lambda r: scatter_copy(ids[0, 0, r], r, s, True).start())
        for_rows(n_prompt_rows, n_rows, lambda r: scatter_copy(ids[0, 0, r], r, s, False).start())

    def wait_rows(src, dst, sem, count):
        bit = tm
        while bit >= 1:
            @pl.when((count & bit) != 0)
            def _(bit=bit):
                pltpu.make_async_copy(src.at[pl.ds(0, bit)], dst.at[pl.ds(0, bit)], sem).wait()
            bit //= 2

    n_rows = nv_ref[t]
    mixed = npr_ref[t] != n_rows
    prev = jnp.maximum(t - 1, 0)
    n_prev = jnp.where(jnp.logical_and(t >= 1, npr_ref[prev] == nv_ref[prev]), nv_ref[prev], 0)

    def wait_gather(s):
        pltpu.make_async_copy(xbuf.at[1 - s], xbuf.at[s], gsem.at[s]).wait()

    @pl.when(t == 0)
    def _():
        for_rows(0, tm, lambda r: gather_copy(idx_ref[0, 0, r], r, 0).start())

    @pl.when(jnp.logical_or(t == 0, nv_ref[prev] > 0))
    def _():
        wait_gather(slot)

    @pl.when(t >= 2)
    def _():
        wait_rows(obuf.at[slot], yp_hbm, ssem.at[slot], nv_ref[t - 2])

    @pl.when(n_rows > 0)
    def _():
        for j in range(n_model_tiles):
            xb_ref[:, j * LANES:(j + 1) * LANES] = tile_lanes(j).astype(BF16)
        n_stages = 8

        def issue_neighbours(stage):
            for r in range(stage * tm // n_stages, (stage + 1) * tm // n_stages):
                gather_copy(idxn_ref[0, 0, r], r, 1 - slot).start()

                @pl.when(r < n_prev)
                def _(r=r):
                    scatter_copy(idxp_ref[0, 0, r], r, 1 - slot, True).start()

        routing = tile_lanes(n_model_tiles)
        gate_a = routing[:, 1:2]
        gate_b = routing[:, 2:3]
        xb = xb_ref[...]
        issue_neighbours(0)
        ha = _dot(xb, wga_ref[0])
        issue_neighbours(1)
        ha = (ha * jax.nn.sigmoid(ha)) * _dot(xb, wua_ref[0])
        issue_neighbours(2)
        ya = _dot(ha.astype(BF16), wda_ref[0])
        issue_neighbours(3)
        hb = _dot(xb, wgb_ref[0])
        issue_neighbours(4)
        hb = (hb * jax.nn.sigmoid(hb)) * _dot(xb, wub_ref[0])
        issue_neighbours(5)
        yb = _dot(hb.astype(BF16), wdb_ref[0])
        issue_neighbours(6)
        moe = gate_a * ya + gate_b * yb
        issue_neighbours(7)
        x1 = jnp.concatenate([tile_lanes(j) for j in range(n_model_tiles)], axis=1)
        obuf[slot] = _layer_norm(alpha * x1 + moe, g2_ref[...], b2_ref[...])

    @pl.when(n_rows == 0)
    def _():
        scatter_tile(idxp_ref, 1 - slot, n_prev, n_prev)

    now = jnp.logical_or(mixed, t == nt - 1)
    scatter_tile(idx_ref, slot, jnp.where(now, npr_ref[t], 0), jnp.where(now, n_rows, 0))

    @pl.when(t == nt - 1)
    def _():
        @pl.when(n_rows > 0)
        def _():
            wait_gather(1 - slot)

        wait_rows(obuf.at[slot], yp_hbm, ssem.at[slot], nv_ref[t])

        @pl.when(t >= 1)
        def _():
            wait_rows(obuf.at[1 - slot], yp_hbm, ssem.at[1 - slot], nv_ref[t - 1])


def _moe(x1_all, route_all, wg, wu, wd, g2, b2, *, alpha, n_prompt):
    n = route_all.shape[1]
    d = wg.shape[1]
    pitch = x1_all.shape[0] // n
    n_p = n_prompt
    n_s = n - n_p
    tm = MOE_TILE
    nt = n // tm + N_BINS if n % tm == 0 else (n + N_BINS * (tm - 1)) // tm + 1
    bins = route_all[0].astype(jnp.int32)

    onehot = (bins[:, None] == jnp.arange(N_BINS, dtype=jnp.int32)[None, :]).astype(jnp.int32)
    csum = jnp.cumsum(onehot, axis=0)
    rank = jnp.sum(csum * onehot, axis=1) - 1
    cnt = csum[n - 1]
    cnt_p = csum[n_p - 1]
    tiles_b = (cnt + tm - 1) // tm
    tile_end = jnp.cumsum(tiles_b)
    tile_start = tile_end - tiles_b
    slot_of_token = jnp.sum(onehot * tile_start[None, :], axis=1) * tm + rank
    idx = jnp.zeros((nt * tm,), jnp.int32).at[slot_of_token].set(
        jnp.arange(n, dtype=jnp.int32), unique_indices=True).reshape(nt, tm)
    tid = jnp.arange(nt, dtype=jnp.int32)
    used = tid < tile_end[N_BINS - 1]
    tbin = jnp.minimum(jnp.sum((tid[:, None] >= tile_end[None, :]).astype(jnp.int32), axis=1), N_BINS - 1)
    last_bin = jnp.max(jnp.where(cnt > 0, jnp.arange(N_BINS, dtype=jnp.int32), 0))
    tbin = jnp.where(used, tbin, last_bin)
    tile_onehot = (tbin[:, None] == jnp.arange(N_BINS, dtype=jnp.int32)[None, :]).astype(jnp.int32)
    lookup = lambda table: jnp.sum(tile_onehot * table[None, :], axis=1)
    kk = tid - lookup(tile_start)
    nvalid = jnp.where(used, jnp.clip(lookup(cnt) - kk * tm, 0, tm), 0).astype(jnp.int32)
    nprompt = jnp.where(used, jnp.clip(lookup(cnt_p) - kk * tm, 0, nvalid), 0).astype(jnp.int32)
    pairs = [(a, b) for a in range(EXPERTS_PER_GROUP) for b in range(a + 1, EXPERTS_PER_GROUP)]
    bin_group = np.arange(N_BINS) // N_PAIRS
    ea = lookup(jnp.asarray(bin_group * EXPERTS_PER_GROUP + np.array([p[0] for p in pairs] * N_EXPERT_GROUPS),
                            jnp.int32))
    eb = lookup(jnp.asarray(bin_group * EXPERTS_PER_GROUP + np.array([p[1] for p in pairs] * N_EXPERT_GROUPS),
                            jnp.int32))
    idx3 = idx.reshape(nt, 1, tm)

    dff = wg.shape[2]
    wspec_in = lambda sel: pl.BlockSpec((1, d, dff), lambda t, ea, eb, nv, npr: (sel(ea, eb)[t], 0, 0))
    wspec_out = lambda sel: pl.BlockSpec((1, dff, d), lambda t, ea, eb, nv, npr: (sel(ea, eb)[t], 0, 0))
    first = lambda a, b: a
    second = lambda a, b: b
    const2 = lambda a: pl.BlockSpec(a.shape, lambda t, *_: (0,) * a.ndim)
    smem_rows = lambda fn: pl.BlockSpec((1, 1, tm), fn, memory_space=pltpu.SMEM)
    grid_spec = pltpu.PrefetchScalarGridSpec(
        num_scalar_prefetch=4,
        grid=(nt,),
        in_specs=[smem_rows(lambda t, *_: (t, 0, 0)),
                  smem_rows(lambda t, *_: (jnp.minimum(t + 1, nt - 1), 0, 0)),
                  smem_rows(lambda t, *_: (jnp.maximum(t - 1, 0), 0, 0)),
                  wspec_in(first), wspec_in(first), wspec_out(first),
                  wspec_in(second), wspec_in(second), wspec_out(second),
                  const2(g2), const2(b2),
                  pl.BlockSpec(memory_space=pl.ANY)],
        out_specs=[pl.BlockSpec(memory_space=pl.ANY), pl.BlockSpec(memory_space=pl.ANY)],
        scratch_shapes=[pltpu.VMEM((2, tm // SUBLANES, pitch, SUBLANES, LANES), F32),
                        pltpu.VMEM((2, tm, d), F32),
                        pltpu.VMEM((tm, d), BF16),
                        pltpu.SemaphoreType.DMA((2,)), pltpu.SemaphoreType.DMA((2,))],
    )
    return pl.pallas_call(
        functools.partial(_moe_kernel, alpha=alpha, n_prompt=n_p),
        grid_spec=grid_spec,
        out_shape=[jax.ShapeDtypeStruct((n_p, d), F32), jax.ShapeDtypeStruct((n_s, d), F32)],
        compiler_params=_params("arbitrary"),
        name="moe",
    )(ea, eb, nvalid, nprompt, idx3, idx3, idx3, wg, wu, wd, wg, wu, wd, g2, b2, x1_all)


def _s5_tables(lam_re, lam_im, log_dt, b_re, b_im, c_re, c_im, seg_lens):
    n_groups, n_state = lam_re.shape
    dt = jnp.exp(log_dt.astype(F32))[:, None]
    lam_re = lam_re.astype(F32)
    lam_im = lam_im.astype(F32)
    mag = jnp.exp(lam_re * dt)
    a_re = mag * jnp.cos(lam_im * dt)
    a_im = mag * jnp.sin(lam_im * dt)
    e_re = a_re - 1.0
    e_im = a_im
    lam_sq = lam_re * lam_re + lam_im * lam_im
    coef_re = (e_re * lam_re + e_im * lam_im) / lam_sq
    coef_im = (e_im * lam_re - e_re * lam_im) / lam_sq
    b_re = b_re.astype(F32)
    b_im = b_im.astype(F32)
    bb_re = coef_re[..., None] * b_re - coef_im[..., None] * b_im
    bb_im = coef_re[..., None] * b_im + coef_im[..., None] * b_re

    gpt = LANES // n_state
    n_tiles = n_groups // gpt
    gps = LANES // S5_GROUP
    g_in_slab = (jnp.arange(n_groups) % gps)
    rows_onehot = jax.nn.one_hot(g_in_slab, gps, dtype=F32)
    w_re = jnp.einsum("gs,gpc->gscp", rows_onehot, bb_re).reshape(n_groups, LANES, n_state)
    w_im = jnp.einsum("gs,gpc->gscp", rows_onehot, bb_im).reshape(n_groups, LANES, n_state)
    w_re = w_re.reshape(n_tiles, gpt, LANES, n_state).transpose(0, 2, 1, 3).reshape(n_tiles, LANES, LANES)
    w_im = w_im.reshape(n_tiles, gpt, LANES, n_state).transpose(0, 2, 1, 3).reshape(n_tiles, LANES, LANES)
    wb = jnp.concatenate([w_re, w_im], axis=-1).astype(BF16)

    n_flat = n_groups * n_state
    out_w = 2 * LANES
    g_per_out = out_w // S5_GROUP
    n_out_tiles = n_groups // g_per_out
    oh = jax.nn.one_hot(jnp.arange(n_groups) % g_per_out, g_per_out, dtype=F32)
    wcr = jnp.einsum("gs,gcp->gpsc", oh, c_re.astype(F32)).reshape(n_out_tiles, g_per_out * n_state, out_w)
    wci = jnp.einsum("gs,gcp->gpsc", oh, -c_im.astype(F32)).reshape(n_out_tiles, g_per_out * n_state, out_w)

    def powers(k):
        return ((mag ** k) * jnp.cos(lam_im * dt * k)).reshape(1, n_flat), \
               ((mag ** k) * jnp.sin(lam_im * dt * k)).reshape(1, n_flat)

    return (wb, a_re.reshape(1, n_flat), a_im.reshape(1, n_flat), [powers(float(k)) for k in seg_lens],
            wcr.astype(BF16), wci.astype(BF16))


def _layer(x_p, x_s, st, lp):
    (w_in, b_gates, conv_w, conv_b, norm_w, lam_re, lam_im, log_dt, b_re, b_im, c_re, c_im,
     d_skip, w_glu, b_glu, w_out, ln1_g, ln1_b, w_r1, b_r1, w_r2, b_r2, w_gate, w_up, w_down,
     ln2_g, ln2_b, alpha) = lp
    state_c, state_n, state_m, state_conv, state_re, state_im = st
    bp, tp, d = x_p.shape
    bs, ts, _ = x_s.shape
    dm = norm_w.shape[0]
    head_dim = dm // M_HEADS
    ds5 = d_skip.shape[0]
    n_groups, n_state = lam_re.shape
    n_flat = n_groups * n_state

    o0 = 2 * dm
    o1 = o0 + 2 * dm
    wqk = w_in[:, :o0].astype(BF16)
    wv = w_in[:, o0:o0 + dm].astype(BF16)
    wo = w_in[:, o0 + dm:o1].astype(BF16)
    wg = jnp.pad(w_in[:, o1:o1 + 2 * M_HEADS], ((0, 0), (0, LANES - 2 * M_HEADS))).astype(BF16)
    wu = w_in[:, o1 + 2 * M_HEADS:].astype(BF16)
    bg = jnp.pad(b_gates.astype(F32), (0, LANES - 2 * M_HEADS)).reshape(1, LANES)
    woa = w_out[:dm].astype(BF16)
    wob = w_out[dm:].astype(BF16)
    wr = jnp.concatenate([w_r1, jnp.transpose(w_r2, (1, 0, 2)).reshape(d, -1)], axis=1).astype(F32)
    n_logits = wr.shape[1]
    wr = jnp.pad(wr, ((0, 0), (0, LANES - n_logits)))
    wrh = wr.astype(BF16)
    wrl = (wr - wrh.astype(F32)).astype(BF16)
    br = jnp.pad(jnp.concatenate([b_r1, b_r2.reshape(-1)]).astype(F32), (0, LANES - n_logits))
    br = br.reshape(LANES, 1)

    blk_p = min(tp, ROW_TILE)
    blk_s = min(ts, ROW_TILE)
    seg_p = blk_p // N_STREAMS
    seg_s = blk_s // N_STREAMS
    wb, a_re, a_im, (as_p, as_s), wcr, wci = _s5_tables(lam_re, lam_im, log_dt, b_re, b_im, c_re, c_im,
                                                         (seg_p, seg_s))

    outs = []
    shared = None
    row_offset = 0
    for x, blk, seg, a_seg, zero_state in ((x_p, blk_p, seg_p, as_p, True), (x_s, blk_s, seg_s, as_s, False)):
        b, t, _ = x.shape
        x2 = x.reshape(b * t, d)
        perm_np = _stream_perm(blk, seg)
        perm = jnp.asarray(perm_np, BF16)
        permt = jnp.asarray(perm_np[:blk, :blk].T, BF16)
        qk, v, og, u, g = _inproj(x2, perm, wqk, wv, wo, wu, wg)

        if zero_state:
            c0 = jnp.zeros((b, M_HEADS, head_dim, head_dim), F32)
            n0 = jnp.zeros((b, M_HEADS, head_dim), F32)
            m0 = jnp.zeros((b, M_HEADS), F32)
            conv0 = jnp.zeros((b, CONV_W - 1, 2 * dm), F32)
            re0 = jnp.zeros((b, n_groups, n_state), F32)
            im0 = jnp.zeros((b, n_groups, n_state), F32)
        else:
            c0, n0, m0, conv0, re0, im0 = (state_c.astype(F32), state_n.astype(F32), state_m.astype(F32),
                                           state_conv.astype(F32), state_re.astype(F32), state_im.astype(F32))
        n0p = jnp.pad(n0, ((0, 0), (0, SUBLANES - M_HEADS), (0, 0)))
        m0p = jnp.broadcast_to(jnp.pad(m0, ((0, 0), (0, SUBLANES - M_HEADS)))[:, :, None], (b, SUBLANES, LANES))
        tail0 = jnp.pad(conv0, ((0, 0), (SUBLANES - (CONV_W - 1), 0), (0, 0)))
        chunk = min(t, LANES)
        hm, c_new, n_new, m_new = _mlstm(qk, v, og, g, c0, n0p, m0p, tail0, conv_w.astype(F32),
                                         conv_b.astype(F32).reshape(1, -1), bg, norm_w.astype(F32).reshape(1, -1),
                                         batch=b, seq=t, chunk=chunk)
        hs, re_new, im_new = _s5(u, re0.reshape(b, 1, n_flat), im0.reshape(b, 1, n_flat), wb, a_re, a_im,
                                 a_seg[0], a_seg[1], wcr, wci, d_skip.astype(F32).reshape(1, -1),
                                 w_glu.astype(BF16), b_glu.astype(F32).reshape(1, -1), permt,
                                 batch=b, seq=t, block=blk)
        shared = _outproj(x2, hm, hs, woa, wob, ln1_g.astype(F32).reshape(1, -1),
                          ln1_b.astype(F32).reshape(1, -1), wrh, wrl, br, shared, alpha=alpha,
                          n_total=bp * tp + bs * ts, row_offset=row_offset)
        row_offset += b * t
        conv_new = jnp.concatenate([conv0, qk.reshape(b, t, 2 * dm)], axis=1)[:, -(CONV_W - 1):]
        states = (c_new, n_new[:, :M_HEADS], m_new[:, :M_HEADS, 0], conv_new,
                  re_new.reshape(b, n_groups, n_state), im_new.reshape(b, n_groups, n_state))
        outs.append(states)

    st_p, st_s = outs
    x1_all, route_all = shared
    yp, ys = _moe(x1_all, route_all, w_gate.astype(BF16), w_up.astype(BF16), w_down.astype(BF16),
                  ln2_g.astype(F32).reshape(1, -1), ln2_b.astype(F32).reshape(1, -1), alpha=alpha,
                  n_prompt=bp * tp)
    return yp.reshape(bp, tp, d), ys.reshape(bs, ts, d), st_p, st_s


def kernel(x_prompt, x_sample, state_mlstm_C, state_mlstm_n, state_mlstm_m, state_conv, state_s5_re, state_s5_im, w_in, b_gates, conv_w, conv_b, mlstm_norm_w, s5_lam_re, s5_lam_im, s5_log_dt, s5_b_re, s5_b_im, s5_c_re, s5_c_im, s5_d, w_glu, b_glu, w_out, ln1_g, ln1_b, w_r1, b_r1, w_r2, b_r2, w_gate, w_up, w_down, ln2_g, ln2_b):
    depth = w_in.shape[0]
    alpha = (2 * depth) ** 0.25
    yp, ys = x_prompt, x_sample
    sts_p, sts_s = [], []
    for l in range(depth):
        lp = (w_in[l], b_gates[l], conv_w[l], conv_b[l], mlstm_norm_w[l], s5_lam_re[l], s5_lam_im[l],
              s5_log_dt[l], s5_b_re[l], s5_b_im[l], s5_c_re[l], s5_c_im[l], s5_d[l], w_glu[l], b_glu[l],
              w_out[l], ln1_g[l], ln1_b[l], w_r1[l], b_r1[l], w_r2[l], b_r2[l], w_gate[l], w_up[l],
              w_down[l], ln2_g[l], ln2_b[l], alpha)
        st = (state_mlstm_C[l], state_mlstm_n[l], state_mlstm_m[l], state_conv[l], state_s5_re[l],
              state_s5_im[l])
        yp, ys, sp, ss = _layer(yp, ys, st, lp)
        sts_p.append(sp)
        sts_s.append(ss)
    stack = lambda sts, i: jnp.stack([s[i] for s in sts])
    return (yp, ys) + tuple(stack(sts_p, i) for i in range(6)) + tuple(stack(sts_s, i) for i in range(6))
```

```python
import functools
import math

import jax
import jax.numpy as jnp
import numpy as np
from jax import lax
from jax.experimental import pallas as pl
from jax.experimental.pallas import tpu as pltpu

F32 = jnp.float32
BF16 = jnp.bfloat16

LANES = 128
SUBLANES = 8
VMEM_LIMIT = 56 * 1024 * 1024

M_HEADS = 4
CONV_W = 4
S5_GROUP = 16
S5_STATE = 64
N_EXPERT_GROUPS = 4
EXPERTS_PER_GROUP = 4
N_PAIRS = 6
N_BINS = N_EXPERT_GROUPS * N_PAIRS
LN_EPS = 1e-5
NEG = -1e30

ROW_TILE = 256
MOE_TILE = 256
ISSUE_UNROLL = 8
N_STREAMS = SUBLANES
SCAN_LANES = 512


def _dot(a, b):
    return jnp.dot(a, b, preferred_element_type=F32)


def _dot_nt(a, b):
    return lax.dot_general(a, b, (((1,), (1,)), ((), ())), preferred_element_type=F32)


def _resident(shape):
    nd = len(shape)
    return pl.BlockSpec(shape, lambda *_: (0,) * nd, pipeline_mode=pl.Buffered(1))


def _params(*sem):
    return pltpu.CompilerParams(dimension_semantics=sem, vmem_limit_bytes=VMEM_LIMIT)


def _token_pitch(d):
    return -(-(d // LANES + 1) // SUBLANES) * SUBLANES


def _stream_perm(block, seg):
    n_seg = block // seg
    r = np.arange(block)
    src = (r % n_seg) * seg + r // n_seg
    p = np.zeros((block, block), np.float32)
    p[r, src] = 1.0
    reps = ROW_TILE // block
    return np.kron(np.eye(reps, dtype=np.float32), p)


def _inproj_kernel(x_ref, perm_ref, wqk_ref, wv_ref, wo_ref, wu_ref, wg_ref,
                   qk_ref, v_ref, og_ref, u_ref, g_ref):
    xb = x_ref[...].astype(BF16)
    qk_ref[...] = _dot(xb, wqk_ref[...])
    v_ref[...] = _dot(xb, wv_ref[...]).astype(BF16)
    og_ref[...] = _dot(xb, wo_ref[...])
    g_ref[...] = _dot(xb, wg_ref[...])
    xp = _dot(perm_ref[...], xb).astype(BF16)
    u_ref[...] = _dot(xp, wu_ref[...])


def _inproj(x2, perm, wqk, wv, wo, wu, wg):
    n, d = x2.shape
    dm = wv.shape[1]
    ds5 = wu.shape[1]
    rows = lambda w: pl.BlockSpec((ROW_TILE, w), lambda i: (i, 0))
    return pl.pallas_call(
        _inproj_kernel,
        grid=(n // ROW_TILE,),
        in_specs=[rows(d), _resident(perm.shape), _resident(wqk.shape), _resident(wv.shape),
                  _resident(wo.shape), _resident(wu.shape), _resident(wg.shape)],
        out_specs=[rows(2 * dm), rows(dm), rows(dm), rows(ds5), rows(LANES)],
        out_shape=[jax.ShapeDtypeStruct((n, 2 * dm), F32), jax.ShapeDtypeStruct((n, dm), BF16),
                   jax.ShapeDtypeStruct((n, dm), F32), jax.ShapeDtypeStruct((n, ds5), F32),
                   jax.ShapeDtypeStruct((n, LANES), F32)],
        compiler_params=_params("arbitrary"),
        name="inproj",
    )(x2, perm, wqk, wv, wo, wu, wg)


def _cumsum_rows(x):
    n = x.shape[0]
    row = lax.broadcasted_iota(jnp.int32, x.shape, 0)
    s = 1
    while s < n:
        x = x + jnp.where(row >= s, pltpu.roll(x, s, 0), 0.0)
        s *= 2
    return x


def _transpose_gate_cols(x):
    sel = (lax.broadcasted_iota(jnp.int32, (SUBLANES, LANES), 0)
           == lax.broadcasted_iota(jnp.int32, (SUBLANES, LANES), 1)).astype(BF16)
    hi = x.astype(BF16)
    r1 = x - hi.astype(F32)
    mid = r1.astype(BF16)
    lo = (r1 - mid.astype(F32)).astype(BF16)
    return _dot_nt(sel, hi) + _dot_nt(sel, mid) + _dot_nt(sel, lo)


def _mlstm_kernel(qk_ref, v_ref, og_ref, g_ref, c0_ref, n0_ref, m0_ref, tail0_ref,
                  cw_ref, cb_ref, bg_ref, nw_ref,
                  h_ref, c_out, n_out, m_out,
                  ext_ref, c_ref, n_ref, m_ref, *, chunk, head_dim):
    ci = pl.program_id(1)
    dm = M_HEADS * head_dim
    L = chunk

    @pl.when(ci == 0)
    def _():
        c_ref[...] = c0_ref[0]
        n_ref[...] = n0_ref[0]
        m_ref[...] = m0_ref[0]
        ext_ref[0:SUBLANES, :] = tail0_ref[0]

    ext_ref[SUBLANES:SUBLANES + L, :] = qk_ref[...]
    full = ext_ref[...]
    acc = cb_ref[...] + full[SUBLANES:, :] * cw_ref[CONV_W - 1:CONV_W, :]
    for back in range(1, CONV_W):
        tap = cw_ref[CONV_W - 1 - back:CONV_W - back, :]
        acc = acc + pltpu.roll(full, back, 0)[SUBLANES:, :] * tap
    tail = ext_ref[L:L + SUBLANES, :]
    ext_ref[0:SUBLANES, :] = tail
    qk = acc * jax.nn.sigmoid(acc)

    gates = g_ref[...] + bg_ref[...]
    fpre = pltpu.roll(gates, LANES - M_HEADS, 1)
    logf = jnp.minimum(fpre, 0.0) - jnp.log1p(jnp.exp(-jnp.abs(fpre)))
    bcum = _cumsum_rows(logf)
    rrow = _transpose_gate_cols(gates - bcum)
    tri = (lax.broadcasted_iota(jnp.int32, (L, L), 0) >= lax.broadcasted_iota(jnp.int32, (L, L), 1))

    for h in range(M_HEADS):
        lo, hi = h * head_dim, (h + 1) * head_dim
        q = qk[:, lo:hi]
        k = qk[:, dm + lo:dm + hi] * (head_dim ** -0.5)
        v = v_ref[:, lo:hi]
        qb = q.astype(BF16)
        kb = k.astype(BF16)
        b_col = bcum[:, h:h + 1]
        ig_col = gates[:, h:h + 1]
        m_prev = m_ref[h:h + 1, 0:1]
        c_prev = c_ref[h]
        n_prev = n_ref[h:h + 1, :]

        dmat = jnp.where(tri, b_col + rrow[h:h + 1, :], NEG)
        inter = b_col + m_prev
        m_t = jnp.maximum(inter, jnp.max(dmat, axis=-1, keepdims=True))
        w_intra = jnp.exp(dmat - m_t)
        w_inter = jnp.exp(inter - m_t)
        s = _dot_nt(qb, kb) * w_intra
        num = w_inter * _dot(qb, c_prev.astype(BF16)) + _dot(s.astype(BF16), v)
        den = w_inter * jnp.sum(q * n_prev, axis=-1, keepdims=True) + jnp.sum(s, axis=-1, keepdims=True)
        hh = num / jnp.maximum(jnp.abs(den), jnp.exp(-m_t))

        m_new = m_t[L - 1:L, :]
        b_last = b_col[L - 1:L, :]
        w_s = jnp.exp(b_last - b_col + ig_col - m_new)
        decay = jnp.exp(b_last + m_prev - m_new)
        kw = k * w_s
        c_ref[h] = decay * c_prev + _dot(kw.T.astype(BF16), v)
        n_ref[h:h + 1, :] = decay * n_prev + jnp.sum(kw, axis=0, keepdims=True)
        m_ref[h:h + 1, :] = jnp.broadcast_to(m_new, (1, LANES))

        mu = jnp.mean(hh, axis=-1, keepdims=True)
        hc = hh - mu
        var = jnp.mean(hc * hc, axis=-1, keepdims=True)
        hn = hc * lax.rsqrt(var + LN_EPS) * nw_ref[:, lo:hi]
        h_ref[:, lo:hi] = (jax.nn.sigmoid(og_ref[:, lo:hi]) * hn).astype(BF16)

    @pl.when(ci == pl.num_programs(1) - 1)
    def _():
        c_out[0] = c_ref[...]
        n_out[0] = n_ref[...]
        m_out[0] = m_ref[...]


def _mlstm(qk, v, og, g, c0, n0p, m0p, tail0, conv_w, conv_b, bg, norm_w, *, batch, seq, chunk):
    n, dm = v.shape
    head_dim = dm // M_HEADS
    nc = seq // chunk
    rows = lambda w: pl.BlockSpec((chunk, w), lambda b, c: (b * nc + c, 0))
    per_b = lambda *s: pl.BlockSpec((1,) + s, lambda b, c: (b,) + (0,) * len(s))
    const = lambda a: pl.BlockSpec(a.shape, lambda b, c: (0,) * a.ndim)
    return pl.pallas_call(
        functools.partial(_mlstm_kernel, chunk=chunk, head_dim=head_dim),
        grid=(batch, nc),
        in_specs=[rows(2 * dm), rows(dm), rows(dm), rows(LANES),
                  per_b(M_HEADS, head_dim, head_dim), per_b(SUBLANES, head_dim), per_b(SUBLANES, LANES),
                  per_b(SUBLANES, 2 * dm),
                  const(conv_w), const(conv_b), const(bg), const(norm_w)],
        out_specs=[rows(dm), per_b(M_HEADS, head_dim, head_dim), per_b(SUBLANES, head_dim),
                   per_b(SUBLANES, LANES)],
        out_shape=[jax.ShapeDtypeStruct((n, dm), BF16),
                   jax.ShapeDtypeStruct((batch, M_HEADS, head_dim, head_dim), F32),
                   jax.ShapeDtypeStruct((batch, SUBLANES, head_dim), F32),
                   jax.ShapeDtypeStruct((batch, SUBLANES, LANES), F32)],
        scratch_shapes=[pltpu.VMEM((chunk + SUBLANES, 2 * dm), F32),
                        pltpu.VMEM((M_HEADS, head_dim, head_dim), F32),
                        pltpu.VMEM((SUBLANES, head_dim), F32),
                        pltpu.VMEM((SUBLANES, LANES), F32)],
        compiler_params=_params("arbitrary", "arbitrary"),
        name="mlstm",
    )(qk, v, og, g, c0, n0p, m0p, tail0, conv_w, conv_b, bg, norm_w)


def _s5_kernel(u_ref, sre0_ref, sim0_ref, wb_ref, are_ref, aim_ref, asre_ref, asim_ref,
               wcr_ref, wci_ref, dskip_ref, wglu_ref, bglu_ref, permt_ref,
               hs_ref, sre_out, sim_out,
               xr_ref, xi_ref, hr_ref, hi_ref, *, seg):
    bi = pl.program_id(1)
    n_state = xr_ref.shape[1]
    n_tiles = wb_ref.shape[0]

    @pl.when(bi == 0)
    def _():
        hr_ref[...] = sre0_ref[0]
        hi_ref[...] = sim0_ref[0]

    ub = u_ref[...].astype(BF16)
    tiles_per_slab = LANES // (2 * S5_GROUP)
    row = lax.broadcasted_iota(jnp.int32, (N_STREAMS, SCAN_LANES), 0)
    for sl in range(n_state // SCAN_LANES):
        cols = slice(sl * SCAN_LANES, (sl + 1) * SCAN_LANES)
        for j in range(sl * (SCAN_LANES // LANES), (sl + 1) * (SCAN_LANES // LANES)):
            cs = (j // tiles_per_slab) * LANES
            xj = _dot(ub[:, cs:cs + LANES], wb_ref[j])
            xr_ref[:, j * LANES:(j + 1) * LANES] = xj[:, :LANES]
            xi_ref[:, j * LANES:(j + 1) * LANES] = xj[:, LANES:]
        ar = jnp.broadcast_to(are_ref[:, cols], (N_STREAMS, SCAN_LANES))
        ai = jnp.broadcast_to(aim_ref[:, cols], (N_STREAMS, SCAN_LANES))

        er = xr_ref[0:N_STREAMS, cols]
        ei = xi_ref[0:N_STREAMS, cols]
        for i in range(1, seg):
            rows = slice(i * N_STREAMS, (i + 1) * N_STREAMS)
            nr = ar * er - ai * ei + xr_ref[rows, cols]
            ni = ar * ei + ai * er + xi_ref[rows, cols]
            xr_ref[rows, cols] = nr
            xi_ref[rows, cols] = ni
            er, ei = nr, ni

        asr = asre_ref[:, cols]
        asi = asim_ref[:, cols]
        cr = hr_ref[:, cols]
        ci = hi_ref[:, cols]
        cmr = jnp.zeros((N_STREAMS, SCAN_LANES), F32)
        cmi = jnp.zeros((N_STREAMS, SCAN_LANES), F32)
        for k in range(N_STREAMS):
            cmr = jnp.where(row == k, cr, cmr)
            cmi = jnp.where(row == k, ci, cmi)
            nr = asr * cr - asi * ci + er[k:k + 1, :]
            ni = asr * ci + asi * cr + ei[k:k + 1, :]
            cr, ci = nr, ni
        hr_ref[:, cols] = cr
        hi_ref[:, cols] = ci

        dr, di = cmr, cmi
        for i in range(seg):
            rows = slice(i * N_STREAMS, (i + 1) * N_STREAMS)
            dr, di = ar * dr - ai * di, ar * di + ai * dr
            xr_ref[rows, cols] += dr
            xi_ref[rows, cols] += di

    n_out_tiles = wcr_ref.shape[0]
    kw = n_state // n_out_tiles
    ys = []
    for qt in range(n_out_tiles):
        hr = xr_ref[:, qt * kw:(qt + 1) * kw].astype(BF16)
        hi = xi_ref[:, qt * kw:(qt + 1) * kw].astype(BF16)
        ys.append(_dot(hr, wcr_ref[qt]) + _dot(hi, wci_ref[qt]))
    y = jnp.concatenate(ys, axis=-1) + dskip_ref[...] * u_ref[...]
    gl = 0.5 * y * (1.0 + jnp.tanh(math.sqrt(2.0 / math.pi) * (y + 0.044715 * (y * y * y))))
    z = _dot(gl.astype(BF16), wglu_ref[...]) + bglu_ref[...]
    out = gl * jax.nn.sigmoid(z)
    hs_ref[...] = _dot(permt_ref[...], out.astype(BF16)).astype(BF16)

    @pl.when(bi == pl.num_programs(1) - 1)
    def _():
        sre_out[0] = hr_ref[...]
        sim_out[0] = hi_ref[...]


def _s5(u, sre0, sim0, wb, a_re, a_im, as_re, as_im, wcr, wci, dskip, wglu, bglu, permt,
        *, batch, seq, block):
    n, ds5 = u.shape
    n_state = a_re.shape[1]
    nb = seq // block
    rows = lambda w: pl.BlockSpec((block, w), lambda b, c: (b * nb + c, 0))
    per_b = pl.BlockSpec((1, 1, n_state), lambda b, c: (b, 0, 0))
    const = lambda a: pl.BlockSpec(a.shape, lambda b, c: (0,) * a.ndim)
    return pl.pallas_call(
        functools.partial(_s5_kernel, seg=block // N_STREAMS),
        grid=(batch, nb),
        in_specs=[rows(ds5), per_b, per_b, const(wb), const(a_re), const(a_im), const(as_re),
                  const(as_im), const(wcr), const(wci), const(dskip), const(wglu), const(bglu),
                  const(permt)],
        out_specs=[rows(ds5), per_b, per_b],
        out_shape=[jax.ShapeDtypeStruct((n, ds5), BF16),
                   jax.ShapeDtypeStruct((batch, 1, n_state), F32),
                   jax.ShapeDtypeStruct((batch, 1, n_state), F32)],
        scratch_shapes=[pltpu.VMEM((block, n_state), F32), pltpu.VMEM((block, n_state), F32),
                        pltpu.VMEM((1, n_state), F32), pltpu.VMEM((1, n_state), F32)],
        compiler_params=_params("arbitrary", "arbitrary"),
        name="s5",
    )(u, sre0, sim0, wb, a_re, a_im, as_re, as_im, wcr, wci, dskip, wglu, bglu, permt)


def _layer_norm(x, g, b):
    mu = jnp.mean(x, axis=-1, keepdims=True)
    xc = x - mu
    var = jnp.mean(xc * xc, axis=-1, keepdims=True)
    return xc * lax.rsqrt(var + LN_EPS) * g + b


def _first_max(cols):
    best = cols[0]
    for c in cols[1:]:
        best = jnp.maximum(best, c)
    flags = []
    taken = None
    for c in cols:
        hit = c == best
        if taken is not None:
            hit = jnp.logical_and(hit, jnp.logical_not(taken))
            taken = jnp.logical_or(taken, hit)
        else:
            taken = hit
        flags.append(hit)
    return best, flags


def _outproj_kernel(*refs, alpha, n_valid_blocks):
    x1_ref, route_ref = refs[-2:]
    i = pl.program_id(0)

    @pl.when(i < n_valid_blocks)
    def _():
        _outproj_tile(*refs, alpha=alpha)

    @pl.when(i >= n_valid_blocks)
    def _():
        x1_ref[...] = jnp.zeros_like(x1_ref)
        route_ref[...] = jnp.zeros_like(route_ref)


def _outproj_tile(x_ref, hm_ref, hs_ref, woa_ref, wob_ref, g1_ref, b1_ref, wrh_ref, wrl_ref, br_ref,
                  *rest, alpha):
    x1_ref, route_ref = rest[-2:]
    tm, d = x_ref.shape
    pitch = x1_ref.shape[0] // tm

    def store_token_tile(j, v):
        for g in range(tm // SUBLANES):
            x1_ref[pl.ds(g * SUBLANES * pitch + j, SUBLANES, stride=pitch), :] = v[g * SUBLANES:(g + 1) * SUBLANES, :]

    mix = _dot(hm_ref[...], woa_ref[...]) + _dot(hs_ref[...], wob_ref[...])
    x1 = _layer_norm(alpha * x_ref[...] + mix, g1_ref[...], b1_ref[...])
    for j in range(d // LANES):
        store_token_tile(j, x1[:, j * LANES:(j + 1) * LANES])

    xh = x1.astype(BF16)
    xl = (x1 - xh.astype(F32)).astype(BF16)
    logits = (_dot(xh, wrh_ref[...]) + _dot(xl, wrh_ref[...]) + _dot(xh, wrl_ref[...])).T + br_ref[...]

    l1 = [logits[g:g + 1, :] for g in range(N_EXPERT_GROUPS)]
    m1, gsel = _first_max(l1)
    denom = l1[0] * 0.0
    for c in l1:
        denom = denom + jnp.exp(c - m1)
    p_g = 1.0 / denom
    gid = l1[0] * 0.0
    l2 = []
    for e in range(EXPERTS_PER_GROUP):
        acc = l1[0] * 0.0
        for g in range(N_EXPERT_GROUPS):
            c0 = N_EXPERT_GROUPS + g * EXPERTS_PER_GROUP + e
            acc = acc + jnp.where(gsel[g], logits[c0:c0 + 1, :], 0.0)
        l2.append(acc)
    for g in range(N_EXPERT_GROUPS):
        gid = gid + jnp.where(gsel[g], float(g), 0.0)

    v1, f1 = _first_max(l2)
    masked = [jnp.where(f1[e], -jnp.inf, l2[e]) for e in range(EXPERTS_PER_GROUP)]
    v2, f2 = _first_max(masked)
    e21 = jnp.exp(v2 - v1)
    w_first = 1.0 / (1.0 + e21)
    w_second = e21 / (1.0 + e21)
    pair_id = l1[0] * 0.0
    gate_a = l1[0] * 0.0
    gate_b = l1[0] * 0.0
    pid = 0
    for a in range(EXPERTS_PER_GROUP):
        for b in range(a + 1, EXPERTS_PER_GROUP):
            ab = jnp.logical_and(f1[a], f2[b])
            ba = jnp.logical_and(f1[b], f2[a])
            pair_id = pair_id + jnp.where(jnp.logical_or(ab, ba), float(pid), 0.0)
            gate_a = gate_a + jnp.where(ab, w_first, 0.0) + jnp.where(ba, w_second, 0.0)
            gate_b = gate_b + jnp.where(ab, w_second, 0.0) + jnp.where(ba, w_first, 0.0)
            pid += 1
    bin_id = gid * float(N_PAIRS) + pair_id
    row = lax.broadcasted_iota(jnp.int32, (LANES, x_ref.shape[0]), 0)
    route_t = jnp.where(row == 0, bin_id,
                        jnp.where(row == 1, p_g * gate_a, jnp.where(row == 2, p_g * gate_b, 0.0)))
    route_ref[...] = route_t[0:SUBLANES, :]
    store_token_tile(d // LANES, route_t.T)
    for j in range(d // LANES + 1, pitch):
        store_token_tile(j, jnp.zeros((tm, LANES), F32))


def _outproj(x2, hm, hs, woa, wob, g1, b1, wrh, wrl, br_col, shared, *, alpha, n_total, row_offset):
    n, d = x2.shape
    dm = hm.shape[1]
    blk0 = row_offset // ROW_TILE
    n_blocks = n // ROW_TILE
    pitch = _token_pitch(d)
    grid = n_blocks if shared is not None else n_total // ROW_TILE - blk0
    rows = lambda w: pl.BlockSpec((ROW_TILE, w), lambda i: (jnp.minimum(i, n_blocks - 1), 0))
    kern = functools.partial(_outproj_kernel, alpha=alpha, n_valid_blocks=n_blocks)
    in_specs = [rows(d), rows(dm), rows(hs.shape[1]), _resident(woa.shape), _resident(wob.shape),
                _resident(g1.shape), _resident(b1.shape), _resident(wrh.shape), _resident(wrl.shape),
                _resident(br_col.shape)]
    args = [x2, hm, hs, woa, wob, g1, b1, wrh, wrl, br_col]
    aliases = {}
    if shared is not None:
        in_specs += [pl.BlockSpec(memory_space=pl.ANY), pl.BlockSpec(memory_space=pl.ANY)]
        aliases = {len(args): 0, len(args) + 1: 1}
        args = args + list(shared)
    return pl.pallas_call(
        kern,
        grid=(grid,),
        in_specs=in_specs,
        out_specs=[pl.BlockSpec((ROW_TILE * pitch, LANES), lambda i: (i + blk0, 0)),
                   pl.BlockSpec((SUBLANES, ROW_TILE), lambda i: (0, i + blk0))],
        out_shape=[jax.ShapeDtypeStruct((n_total * pitch, LANES), F32),
                   jax.ShapeDtypeStruct((SUBLANES, n_total), F32)],
        input_output_aliases=aliases,
        compiler_params=_params("arbitrary"),
        name="outproj",
    )(*args)


def _moe_kernel(ea_ref, eb_ref, nv_ref, npr_ref,
                idx_ref, idxn_ref, idxp_ref, wga_ref, wua_ref, wda_ref, wgb_ref, wub_ref, wdb_ref,
                g2_ref, b2_ref, x1_hbm,
                yp_hbm, ys_hbm,
                xbuf, obuf, xb_ref, gsem, ssem, *, alpha, n_prompt):
    t = pl.program_id(0)
    nt = pl.num_programs(0)
    slot = t % 2
    tm, d = obuf.shape[1:]
    pitch = xbuf.shape[2]
    n_model_tiles = d // LANES

    def gather_copy(tok, r, s):
        src = x1_hbm.at[pl.ds(pl.multiple_of(tok * pitch, SUBLANES), pitch), :]
        return pltpu.make_async_copy(src, xbuf.at[s, r // SUBLANES, :, r % SUBLANES, :], gsem.at[s])

    def tile_lanes(j):
        return xbuf[slot, :, j].reshape(tm, LANES)

    def scatter_copy(tok, r, s, to_prompt):
        dst = yp_hbm.at[pl.ds(tok, 1)] if to_prompt else ys_hbm.at[pl.ds(tok - n_prompt, 1)]
        return pltpu.make_async_copy(obuf.at[s, pl.ds(r, 1)], dst, ssem.at[s])

    def for_rows(lo, hi, fn):
        n_groups = (hi - lo) // ISSUE_UNROLL

        def group(gi, c):
            base = lo + gi * ISSUE_UNROLL
            for j in range(ISSUE_UNROLL):
                fn(base + j)
            return c

        def single(r, c):
            fn(r)
            return c

        lax.fori_loop(0, n_groups, group, 0)
        lax.fori_loop(lo + n_groups * ISSUE_UNROLL, hi, single, 0)

    def scatter_tile(ids, s, n_prompt_rows, n_rows):
        for_rows(0, n_prompt_rows, lambda r: scatter_copy(ids[0, 0, r], r, s, True).start())
        for_rows(n_prompt_rows, n_rows, lambda r: scatter_copy(ids[0, 0, r], r, s, False).start())

    def wait_rows(src, dst, sem, count):
        bit = tm
        while bit >= 1:
            @pl.when((count & bit) != 0)
            def _(bit=bit):
                pltpu.make_async_copy(src.at[pl.ds(0, bit)], dst.at[pl.ds(0, bit)], sem).wait()
            bit //= 2

    n_rows = nv_ref[t]
    mixed = npr_ref[t] != n_rows
    prev = jnp.maximum(t - 1, 0)
    n_prev = jnp.where(jnp.logical_and(t >= 1, npr_ref[prev] == nv_ref[prev]), nv_ref[prev], 0)

    def wait_gather(s):
        pltpu.make_async_copy(xbuf.at[1 - s], xbuf.at[s], gsem.at[s]).wait()

    @pl.when(t == 0)
    def _():
        for_rows(0, tm, lambda r: gather_copy(idx_ref[0, 0, r], r, 0).start())

    @pl.when(jnp.logical_or(t == 0, nv_ref[prev] > 0))
    def _():
        wait_gather(slot)

    n_old = jnp.where(t >= 2, nv_ref[jnp.maximum(t - 2, 0)], 0)

    @pl.when(n_rows > 0)
    def _():
        for j in range(n_model_tiles):
            xb_ref[:, j * LANES:(j + 1) * LANES] = tile_lanes(j).astype(BF16)
        n_stages = 8

        def issue_neighbours(stage):
            half = n_stages // 2
            per = tm // half
            if stage < half:
                for r in range(stage * per, (stage + 1) * per):
                    gather_copy(idxn_ref[0, 0, r], r, 1 - slot).start()
            else:
                for r in range((stage - half) * per, (stage - half + 1) * per):
                    @pl.when(r < n_prev)
                    def _(r=r):
                        scatter_copy(idxp_ref[0, 0, r], r, 1 - slot, True).start()

        routing = tile_lanes(n_model_tiles)
        gate_a = routing[:, 1:2]
        gate_b = routing[:, 2:3]
        xb = xb_ref[...]
        issue_neighbours(0)
        ha = _dot(xb, wga_ref[0])
        issue_neighbours(1)
        ha = (ha * jax.nn.sigmoid(ha)) * _dot(xb, wua_ref[0])
        issue_neighbours(2)
        ya = _dot(ha.astype(BF16), wda_ref[0])
        issue_neighbours(3)
        hb = _dot(xb, wgb_ref[0])
        issue_neighbours(4)
        hb = (hb * jax.nn.sigmoid(hb)) * _dot(xb, wub_ref[0])
        issue_neighbours(5)
        yb = _dot(hb.astype(BF16), wdb_ref[0])
        issue_neighbours(6)
        issue_neighbours(7)
        wait_rows(obuf.at[slot], yp_hbm, ssem.at[slot], n_old)
        moe = gate_a * ya + gate_b * yb
        x1 = jnp.concatenate([tile_lanes(j) for j in range(n_model_tiles)], axis=1)
        obuf[slot] = _layer_norm(alpha * x1 + moe, g2_ref[...], b2_ref[...])

    @pl.when(n_rows == 0)
    def _():
        wait_rows(obuf.at[slot], yp_hbm, ssem.at[slot], n_old)
        scatter_tile(idxp_ref, 1 - slot, n_prev, n_prev)

    now = jnp.logical_or(mixed, t == nt - 1)
    scatter_tile(idx_ref, slot, jnp.where(now, npr_ref[t], 0), jnp.where(now, n_rows, 0))

    @pl.when(t == nt - 1)
    def _():
        @pl.when(n_rows > 0)
        def _():
            wait_gather(1 - slot)

        wait_rows(obuf.at[slot], yp_hbm, ssem.at[slot], nv_ref[t])

        @pl.when(t >= 1)
        def _():
            wait_rows(obuf.at[1 - slot], yp_hbm, ssem.at[1 - slot], nv_ref[t - 1])


def _moe(x1_all, route_all, wg, wu, wd, g2, b2, *, alpha, n_prompt):
    n = route_all.shape[1]
    d = wg.shape[1]
    pitch = x1_all.shape[0] // n
    n_p = n_prompt
    n_s = n - n_p
    tm = MOE_TILE
    nt = n // tm + N_BINS if n % tm == 0 else (n + N_BINS * (tm - 1)) // tm + 1
    bins = route_all[0].astype(jnp.int32)

    onehot = (bins[:, None] == jnp.arange(N_BINS, dtype=jnp.int32)[None, :]).astype(jnp.int32)
    csum = jnp.cumsum(onehot, axis=0)
    rank = jnp.sum(csum * onehot, axis=1) - 1
    cnt = csum[n - 1]
    cnt_p = csum[n_p - 1]
    tiles_b = (cnt + tm - 1) // tm
    tile_end = jnp.cumsum(tiles_b)
    tile_start = tile_end - tiles_b
    slot_of_token = jnp.sum(onehot * tile_start[None, :], axis=1) * tm + rank
    idx = jnp.zeros((nt * tm,), jnp.int32).at[slot_of_token].set(
        jnp.arange(n, dtype=jnp.int32), unique_indices=True).reshape(nt, tm)
    tid = jnp.arange(nt, dtype=jnp.int32)
    used = tid < tile_end[N_BINS - 1]
    tbin = jnp.minimum(jnp.sum((tid[:, None] >= tile_end[None, :]).astype(jnp.int32), axis=1), N_BINS - 1)
    last_bin = jnp.max(jnp.where(cnt > 0, jnp.arange(N_BINS, dtype=jnp.int32), 0))
    tbin = jnp.where(used, tbin, last_bin)
    tile_onehot = (tbin[:, None] == jnp.arange(N_BINS, dtype=jnp.int32)[None, :]).astype(jnp.int32)
    lookup = lambda table: jnp.sum(tile_onehot * table[None, :], axis=1)
    kk = tid - lookup(tile_start)
    nvalid = jnp.where(used, jnp.clip(lookup(cnt) - kk * tm, 0, tm), 0).astype(jnp.int32)
    nprompt = jnp.where(used, jnp.clip(lookup(cnt_p) - kk * tm, 0, nvalid), 0).astype(jnp.int32)
    pairs = [(a, b) for a in range(EXPERTS_PER_GROUP) for b in range(a + 1, EXPERTS_PER_GROUP)]
    bin_group = np.arange(N_BINS) // N_PAIRS
    ea = lookup(jnp.asarray(bin_group * EXPERTS_PER_GROUP + np.array([p[0] for p in pairs] * N_EXPERT_GROUPS),
                            jnp.int32))
    eb = lookup(jnp.asarray(bin_group * EXPERTS_PER_GROUP + np.array([p[1] for p in pairs] * N_EXPERT_GROUPS),
                            jnp.int32))
    idx3 = idx.reshape(nt, 1, tm)

    dff = wg.shape[2]
    wspec_in = lambda sel: pl.BlockSpec((1, d, dff), lambda t, ea, eb, nv, npr: (sel(ea, eb)[t], 0, 0))
    wspec_out = lambda sel: pl.BlockSpec((1, dff, d), lambda t, ea, eb, nv, npr: (sel(ea, eb)[t], 0, 0))
    first = lambda a, b: a
    second = lambda a, b: b
    const2 = lambda a: pl.BlockSpec(a.shape, lambda t, *_: (0,) * a.ndim)
    smem_rows = lambda fn: pl.BlockSpec((1, 1, tm), fn, memory_space=pltpu.SMEM)
    grid_spec = pltpu.PrefetchScalarGridSpec(
        num_scalar_prefetch=4,
        grid=(nt,),
        in_specs=[smem_rows(lambda t, *_: (t, 0, 0)),
                  smem_rows(lambda t, *_: (jnp.minimum(t + 1, nt - 1), 0, 0)),
                  smem_rows(lambda t, *_: (jnp.maximum(t - 1, 0), 0, 0)),
                  wspec_in(first), wspec_in(first), wspec_out(first),
                  wspec_in(second), wspec_in(second), wspec_out(second),
                  const2(g2), const2(b2),
                  pl.BlockSpec(memory_space=pl.ANY)],
        out_specs=[pl.BlockSpec(memory_space=pl.ANY), pl.BlockSpec(memory_space=pl.ANY)],
        scratch_shapes=[pltpu.VMEM((2, tm // SUBLANES, pitch, SUBLANES, LANES), F32),
                        pltpu.VMEM((2, tm, d), F32),
                        pltpu.VMEM((tm, d), BF16),
                        pltpu.SemaphoreType.DMA((2,)), pltpu.SemaphoreType.DMA((2,))],
    )
    return pl.pallas_call(
        functools.partial(_moe_kernel, alpha=alpha, n_prompt=n_p),
        grid_spec=grid_spec,
        out_shape=[jax.ShapeDtypeStruct((n_p, d), F32), jax.ShapeDtypeStruct((n_s, d), F32)],
        compiler_params=_params("arbitrary"),
        name="moe",
    )(ea, eb, nvalid, nprompt, idx3, idx3, idx3, wg, wu, wd, wg, wu, wd, g2, b2, x1_all)


def _s5_tables(lam_re, lam_im, log_dt, b_re, b_im, c_re, c_im, seg_lens):
    n_groups, n_state = lam_re.shape
    dt = jnp.exp(log_dt.astype(F32))[:, None]
    lam_re = lam_re.astype(F32)
    lam_im = lam_im.astype(F32)
    mag = jnp.exp(lam_re * dt)
    a_re = mag * jnp.cos(lam_im * dt)
    a_im = mag * jnp.sin(lam_im * dt)
    e_re = a_re - 1.0
    e_im = a_im
    lam_sq = lam_re * lam_re + lam_im * lam_im
    coef_re = (e_re * lam_re + e_im * lam_im) / lam_sq
    coef_im = (e_im * lam_re - e_re * lam_im) / lam_sq
    b_re = b_re.astype(F32)
    b_im = b_im.astype(F32)
    bb_re = coef_re[..., None] * b_re - coef_im[..., None] * b_im
    bb_im = coef_re[..., None] * b_im + coef_im[..., None] * b_re

    gpt = LANES // n_state
    n_tiles = n_groups // gpt
    gps = LANES // S5_GROUP
    g_in_slab = (jnp.arange(n_groups) % gps)
    rows_onehot = jax.nn.one_hot(g_in_slab, gps, dtype=F32)
    w_re = jnp.einsum("gs,gpc->gscp", rows_onehot, bb_re).reshape(n_groups, LANES, n_state)
    w_im = jnp.einsum("gs,gpc->gscp", rows_onehot, bb_im).reshape(n_groups, LANES, n_state)
    w_re = w_re.reshape(n_tiles, gpt, LANES, n_state).transpose(0, 2, 1, 3).reshape(n_tiles, LANES, LANES)
    w_im = w_im.reshape(n_tiles, gpt, LANES, n_state).transpose(0, 2, 1, 3).reshape(n_tiles, LANES, LANES)
    wb = jnp.concatenate([w_re, w_im], axis=-1).astype(BF16)

    n_flat = n_groups * n_state
    out_w = 2 * LANES
    g_per_out = out_w // S5_GROUP
    n_out_tiles = n_groups // g_per_out
    oh = jax.nn.one_hot(jnp.arange(n_groups) % g_per_out, g_per_out, dtype=F32)
    wcr = jnp.einsum("gs,gcp->gpsc", oh, c_re.astype(F32)).reshape(n_out_tiles, g_per_out * n_state, out_w)
    wci = jnp.einsum("gs,gcp->gpsc", oh, -c_im.astype(F32)).reshape(n_out_tiles, g_per_out * n_state, out_w)

    def powers(k):
        return ((mag ** k) * jnp.cos(lam_im * dt * k)).reshape(1, n_flat), \
               ((mag ** k) * jnp.sin(lam_im * dt * k)).reshape(1, n_flat)

    return (wb, a_re.reshape(1, n_flat), a_im.reshape(1, n_flat), [powers(float(k)) for k in seg_lens],
            wcr.astype(BF16), wci.astype(BF16))


def _layer(x_p, x_s, st, lp):
    (w_in, b_gates, conv_w, conv_b, norm_w, lam_re, lam_im, log_dt, b_re, b_im, c_re, c_im,
     d_skip, w_glu, b_glu, w_out, ln1_g, ln1_b, w_r1, b_r1, w_r2, b_r2, w_gate, w_up, w_down,
     ln2_g, ln2_b, alpha) = lp
    state_c, state_n, state_m, state_conv, state_re, state_im = st
    bp, tp, d = x_p.shape
    bs, ts, _ = x_s.shape
    dm = norm_w.shape[0]
    head_dim = dm // M_HEADS
    ds5 = d_skip.shape[0]
    n_groups, n_state = lam_re.shape
    n_flat = n_groups * n_state

    o0 = 2 * dm
    o1 = o0 + 2 * dm
    wqk = w_in[:, :o0].astype(BF16)
    wv = w_in[:, o0:o0 + dm].astype(BF16)
    wo = w_in[:, o0 + dm:o1].astype(BF16)
    wg = jnp.pad(w_in[:, o1:o1 + 2 * M_HEADS], ((0, 0), (0, LANES - 2 * M_HEADS))).astype(BF16)
    wu = w_in[:, o1 + 2 * M_HEADS:].astype(BF16)
    bg = jnp.pad(b_gates.astype(F32), (0, LANES - 2 * M_HEADS)).reshape(1, LANES)
    woa = w_out[:dm].astype(BF16)
    wob = w_out[dm:].astype(BF16)
    wr = jnp.concatenate([w_r1, jnp.transpose(w_r2, (1, 0, 2)).reshape(d, -1)], axis=1).astype(F32)
    n_logits = wr.shape[1]
    wr = jnp.pad(wr, ((0, 0), (0, LANES - n_logits)))
    wrh = wr.astype(BF16)
    wrl = (wr - wrh.astype(F32)).astype(BF16)
    br = jnp.pad(jnp.concatenate([b_r1, b_r2.reshape(-1)]).astype(F32), (0, LANES - n_logits))
    br = br.reshape(LANES, 1)

    blk_p = min(tp, ROW_TILE)
    blk_s = min(ts, ROW_TILE)
    seg_p = blk_p // N_STREAMS
    seg_s = blk_s // N_STREAMS
    wb, a_re, a_im, (as_p, as_s), wcr, wci = _s5_tables(lam_re, lam_im, log_dt, b_re, b_im, c_re, c_im,
                                                         (seg_p, seg_s))

    outs = []
    shared = None
    row_offset = 0
    for x, blk, seg, a_seg, zero_state in ((x_p, blk_p, seg_p, as_p, True), (x_s, blk_s, seg_s, as_s, False)):
        b, t, _ = x.shape
        x2 = x.reshape(b * t, d)
        perm_np = _stream_perm(blk, seg)
        perm = jnp.asarray(perm_np, BF16)
        permt = jnp.asarray(perm_np[:blk, :blk].T, BF16)
        qk, v, og, u, g = _inproj(x2, perm, wqk, wv, wo, wu, wg)

        if zero_state:
            c0 = jnp.zeros((b, M_HEADS, head_dim, head_dim), F32)
            n0 = jnp.zeros((b, M_HEADS, head_dim), F32)
            m0 = jnp.zeros((b, M_HEADS), F32)
            conv0 = jnp.zeros((b, CONV_W - 1, 2 * dm), F32)
            re0 = jnp.zeros((b, n_groups, n_state), F32)
            im0 = jnp.zeros((b, n_groups, n_state), F32)
        else:
            c0, n0, m0, conv0, re0, im0 = (state_c.astype(F32), state_n.astype(F32), state_m.astype(F32),
                                           state_conv.astype(F32), state_re.astype(F32), state_im.astype(F32))
        n0p = jnp.pad(n0, ((0, 0), (0, SUBLANES - M_HEADS), (0, 0)))
        m0p = jnp.broadcast_to(jnp.pad(m0, ((0, 0), (0, SUBLANES - M_HEADS)))[:, :, None], (b, SUBLANES, LANES))
        tail0 = jnp.pad(conv0, ((0, 0), (SUBLANES - (CONV_W - 1), 0), (0, 0)))
        chunk = min(t, LANES)
        hm, c_new, n_new, m_new = _mlstm(qk, v, og, g, c0, n0p, m0p, tail0, conv_w.astype(F32),
                                         conv_b.astype(F32).reshape(1, -1), bg, norm_w.astype(F32).reshape(1, -1),
                                         batch=b, seq=t, chunk=chunk)
        hs, re_new, im_new = _s5(u, re0.reshape(b, 1, n_flat), im0.reshape(b, 1, n_flat), wb, a_re, a_im,
                                 a_seg[0], a_seg[1], wcr, wci, d_skip.astype(F32).reshape(1, -1),
                                 w_glu.astype(BF16), b_glu.astype(F32).reshape(1, -1), permt,
                                 batch=b, seq=t, block=blk)
        shared = _outproj(x2, hm, hs, woa, wob, ln1_g.astype(F32).reshape(1, -1),
                          ln1_b.astype(F32).reshape(1, -1), wrh, wrl, br, shared, alpha=alpha,
                          n_total=bp * tp + bs * ts, row_offset=row_offset)
        row_offset += b * t
        conv_new = jnp.concatenate([conv0, qk.reshape(b, t, 2 * dm)], axis=1)[:, -(CONV_W - 1):]
        states = (c_new, n_new[:, :M_HEADS], m_new[:, :M_HEADS, 0], conv_new,
                  re_new.reshape(b, n_groups, n_state), im_new.reshape(b, n_groups, n_state))
        outs.append(states)

    st_p, st_s = outs
    x1_all, route_all = shared
    yp, ys = _moe(x1_all, route_all, w_gate.astype(BF16), w_up.astype(BF16), w_down.astype(BF16),
                  ln2_g.astype(F32).reshape(1, -1), ln2_b.astype(F32).reshape(1, -1), alpha=alpha,
                  n_prompt=bp * tp)
    return yp.reshape(bp, tp, d), ys.reshape(bs, ts, d), st_p, st_s


def kernel(x_prompt, x_sample, state_mlstm_C, state_mlstm_n, state_mlstm_m, state_conv, state_s5_re, state_s5_im, w_in, b_gates, conv_w, conv_b, mlstm_norm_w, s5_lam_re, s5_lam_im, s5_log_dt, s5_b_re, s5_b_im, s5_c_re, s5_c_im, s5_d, w_glu, b_glu, w_out, ln1_g, ln1_b, w_r1, b_r1, w_r2, b_r2, w_gate, w_up, w_down, ln2_g, ln2_b):
    depth = w_in.shape[0]
    alpha = (2 * depth) ** 0.25
    yp, ys = x_prompt, x_sample
    sts_p, sts_s = [], []
    for l in range(depth):
        lp = (w_in[l], b_gates[l], conv_w[l], conv_b[l], mlstm_norm_w[l], s5_lam_re[l], s5_lam_im[l],
              s5_log_dt[l], s5_b_re[l], s5_b_im[l], s5_c_re[l], s5_c_im[l], s5_d[l], w_glu[l], b_glu[l],
              w_out[l], ln1_g[l], ln1_b[l], w_r1[l], b_r1[l], w_r2[l], b_r2[l], w_gate[l], w_up[l],
              w_down[l], ln2_g[l], ln2_b[l], alpha)
        st = (state_mlstm_C[l], state_mlstm_n[l], state_mlstm_m[l], state_conv[l], state_s5_re[l],
              state_s5_im[l])
        yp, ys, sp, ss = _layer(yp, ys, st, lp)
        sts_p.append(sp)
        sts_s.append(ss)
    stack = lambda sts, i: jnp.stack([s[i] for s in sts])
    return (yp, ys) + tuple(stack(sts_p, i) for i in range(6)) + tuple(stack(sts_s, i) for i in range(6))
```

```python
import functools
import math

import jax
import jax.numpy as jnp
import numpy as np
from jax import lax
from jax.experimental import pallas as pl
from jax.experimental.pallas import tpu as pltpu

F32 = jnp.float32
BF16 = jnp.bfloat16

LANES = 128
SUBLANES = 8
VMEM_LIMIT = 56 * 1024 * 1024

M_HEADS = 4
CONV_W = 4
S5_GROUP = 16
S5_STATE = 64
N_EXPERT_GROUPS = 4
EXPERTS_PER_GROUP = 4
N_PAIRS = 6
N_BINS = N_EXPERT_GROUPS * N_PAIRS
LN_EPS = 1e-5
NEG = -1e30

ROW_TILE = 256
PERM_ROWS = 256
OUT_ROWS = 256
MOE_TILE = 256
ISSUE_UNROLL = 8
N_STREAMS = SUBLANES
SCAN_LANES = 512


def _dot(a, b):
    return jnp.dot(a, b, preferred_element_type=F32)


def _dot_nt(a, b):
    return lax.dot_general(a, b, (((1,), (1,)), ((), ())), preferred_element_type=F32)


def _resident(shape):
    nd = len(shape)
    return pl.BlockSpec(shape, lambda *_: (0,) * nd, pipeline_mode=pl.Buffered(1))


def _params(*sem):
    return pltpu.CompilerParams(dimension_semantics=sem, vmem_limit_bytes=VMEM_LIMIT)


def _token_pitch(d):
    return -(-(d // LANES + 1) // SUBLANES) * SUBLANES


def _stream_perm(block, seg):
    n_seg = block // seg
    r = np.arange(block)
    src = (r % n_seg) * seg + r // n_seg
    p = np.zeros((block, block), np.float32)
    p[r, src] = 1.0
    reps = PERM_ROWS // block
    return np.kron(np.eye(reps, dtype=np.float32), p)


def _inproj_kernel(x_ref, perm_ref, wqk_ref, wv_ref, wo_ref, wu_ref, wg_ref,
                   qk_ref, v_ref, og_ref, u_ref, g_ref):
    xb = x_ref[...].astype(BF16)
    qk_ref[...] = _dot(xb, wqk_ref[...])
    v_ref[...] = _dot(xb, wv_ref[...]).astype(BF16)
    og_ref[...] = _dot(xb, wo_ref[...])
    g_ref[...] = _dot(xb, wg_ref[...])
    xp = jnp.concatenate([_dot(perm_ref[...], xb[r0:r0 + PERM_ROWS, :]).astype(BF16)
                          for r0 in range(0, xb.shape[0], PERM_ROWS)], axis=0)
    u_ref[...] = _dot(xp, wu_ref[...])


def _inproj(x2, perm, wqk, wv, wo, wu, wg):
    n, d = x2.shape
    dm = wv.shape[1]
    ds5 = wu.shape[1]
    tile = ROW_TILE if n % ROW_TILE == 0 else PERM_ROWS
    rows = lambda w: pl.BlockSpec((tile, w), lambda i: (i, 0))
    return pl.pallas_call(
        _inproj_kernel,
        grid=(n // tile,),
        in_specs=[rows(d), _resident(perm.shape), _resident(wqk.shape), _resident(wv.shape),
                  _resident(wo.shape), _resident(wu.shape), _resident(wg.shape)],
        out_specs=[rows(2 * dm), rows(dm), rows(dm), rows(ds5), rows(LANES)],
        out_shape=[jax.ShapeDtypeStruct((n, 2 * dm), F32), jax.ShapeDtypeStruct((n, dm), BF16),
                   jax.ShapeDtypeStruct((n, dm), F32), jax.ShapeDtypeStruct((n, ds5), F32),
                   jax.ShapeDtypeStruct((n, LANES), F32)],
        compiler_params=_params("arbitrary"),
        name="inproj",
    )(x2, perm, wqk, wv, wo, wu, wg)


def _cumsum_rows(x):
    n = x.shape[0]
    row = lax.broadcasted_iota(jnp.int32, x.shape, 0)
    s = 1
    while s < n:
        x = x + jnp.where(row >= s, pltpu.roll(x, s, 0), 0.0)
        s *= 2
    return x


def _transpose_gate_cols(x):
    sel = (lax.broadcasted_iota(jnp.int32, (SUBLANES, LANES), 0)
           == lax.broadcasted_iota(jnp.int32, (SUBLANES, LANES), 1)).astype(BF16)
    hi = x.astype(BF16)
    r1 = x - hi.astype(F32)
    mid = r1.astype(BF16)
    lo = (r1 - mid.astype(F32)).astype(BF16)
    return _dot_nt(sel, hi) + _dot_nt(sel, mid) + _dot_nt(sel, lo)


def _mlstm_kernel(qk_ref, v_ref, og_ref, g_ref, c0_ref, n0_ref, m0_ref, tail0_ref,
                  cw_ref, cb_ref, bg_ref, nw_ref,
                  h_ref, c_out, n_out, m_out,
                  ext_ref, c_ref, n_ref, m_ref, *, chunk, head_dim):
    ci = pl.program_id(1)
    dm = M_HEADS * head_dim
    L = chunk

    @pl.when(ci == 0)
    def _():
        c_ref[...] = c0_ref[0]
        n_ref[...] = n0_ref[0]
        m_ref[...] = m0_ref[0]
        ext_ref[0:SUBLANES, :] = tail0_ref[0]

    ext_ref[SUBLANES:SUBLANES + L, :] = qk_ref[...]
    full = ext_ref[...]
    acc = cb_ref[...] + full[SUBLANES:, :] * cw_ref[CONV_W - 1:CONV_W, :]
    for back in range(1, CONV_W):
        tap = cw_ref[CONV_W - 1 - back:CONV_W - back, :]
        acc = acc + pltpu.roll(full, back, 0)[SUBLANES:, :] * tap
    tail = ext_ref[L:L + SUBLANES, :]
    ext_ref[0:SUBLANES, :] = tail
    qk = acc * jax.nn.sigmoid(acc)

    gates = g_ref[...] + bg_ref[...]
    fpre = pltpu.roll(gates, LANES - M_HEADS, 1)
    logf = jnp.minimum(fpre, 0.0) - jnp.log1p(jnp.exp(-jnp.abs(fpre)))
    bcum = _cumsum_rows(logf)
    rrow = _transpose_gate_cols(gates - bcum)
    tri = (lax.broadcasted_iota(jnp.int32, (L, L), 0) >= lax.broadcasted_iota(jnp.int32, (L, L), 1))

    for h in range(M_HEADS):
        lo, hi = h * head_dim, (h + 1) * head_dim
        q = qk[:, lo:hi]
        k = qk[:, dm + lo:dm + hi] * (head_dim ** -0.5)
        v = v_ref[:, lo:hi]
        qb = q.astype(BF16)
        kb = k.astype(BF16)
        b_col = bcum[:, h:h + 1]
        ig_col = gates[:, h:h + 1]
        m_prev = m_ref[h:h + 1, 0:1]
        c_prev = c_ref[h]
        n_prev = n_ref[h:h + 1, :]

        dmat = jnp.where(tri, b_col + rrow[h:h + 1, :], NEG)
        inter = b_col + m_prev
        m_t = jnp.maximum(inter, jnp.max(dmat, axis=-1, keepdims=True))
        w_intra = jnp.exp(dmat - m_t)
        w_inter = jnp.exp(inter - m_t)
        s = _dot_nt(qb, kb) * w_intra
        num = w_inter * _dot(qb, c_prev.astype(BF16)) + _dot(s.astype(BF16), v)
        den = w_inter * jnp.sum(q * n_prev, axis=-1, keepdims=True) + jnp.sum(s, axis=-1, keepdims=True)
        hh = num / jnp.maximum(jnp.abs(den), jnp.exp(-m_t))

        m_new = m_t[L - 1:L, :]
        b_last = b_col[L - 1:L, :]
        w_s = jnp.exp(b_last - b_col + ig_col - m_new)
        decay = jnp.exp(b_last + m_prev - m_new)
        kw = k * w_s
        c_ref[h] = decay * c_prev + _dot(kw.T.astype(BF16), v)
        n_ref[h:h + 1, :] = decay * n_prev + jnp.sum(kw, axis=0, keepdims=True)
        m_ref[h:h + 1, :] = jnp.broadcast_to(m_new, (1, LANES))

        mu = jnp.mean(hh, axis=-1, keepdims=True)
        hc = hh - mu
        var = jnp.mean(hc * hc, axis=-1, keepdims=True)
        hn = hc * lax.rsqrt(var + LN_EPS) * nw_ref[:, lo:hi]
        h_ref[:, lo:hi] = (jax.nn.sigmoid(og_ref[:, lo:hi]) * hn).astype(BF16)

    @pl.when(ci == pl.num_programs(1) - 1)
    def _():
        c_out[0] = c_ref[...]
        n_out[0] = n_ref[...]
        m_out[0] = m_ref[...]


def _mlstm(qk, v, og, g, c0, n0p, m0p, tail0, conv_w, conv_b, bg, norm_w, *, batch, seq, chunk):
    n, dm = v.shape
    head_dim = dm // M_HEADS
    nc = seq // chunk
    rows = lambda w: pl.BlockSpec((chunk, w), lambda b, c: (b * nc + c, 0))
    per_b = lambda *s: pl.BlockSpec((1,) + s, lambda b, c: (b,) + (0,) * len(s))
    const = lambda a: pl.BlockSpec(a.shape, lambda b, c: (0,) * a.ndim)
    return pl.pallas_call(
        functools.partial(_mlstm_kernel, chunk=chunk, head_dim=head_dim),
        grid=(batch, nc),
        in_specs=[rows(2 * dm), rows(dm), rows(dm), rows(LANES),
                  per_b(M_HEADS, head_dim, head_dim), per_b(SUBLANES, head_dim), per_b(SUBLANES, LANES),
                  per_b(SUBLANES, 2 * dm),
                  const(conv_w), const(conv_b), const(bg), const(norm_w)],
        out_specs=[rows(dm), per_b(M_HEADS, head_dim, head_dim), per_b(SUBLANES, head_dim),
                   per_b(SUBLANES, LANES)],
        out_shape=[jax.ShapeDtypeStruct((n, dm), BF16),
                   jax.ShapeDtypeStruct((batch, M_HEADS, head_dim, head_dim), F32),
                   jax.ShapeDtypeStruct((batch, SUBLANES, head_dim), F32),
                   jax.ShapeDtypeStruct((batch, SUBLANES, LANES), F32)],
        scratch_shapes=[pltpu.VMEM((chunk + SUBLANES, 2 * dm), F32),
                        pltpu.VMEM((M_HEADS, head_dim, head_dim), F32),
                        pltpu.VMEM((SUBLANES, head_dim), F32),
                        pltpu.VMEM((SUBLANES, LANES), F32)],
        compiler_params=_params("arbitrary", "arbitrary"),
        name="mlstm",
    )(qk, v, og, g, c0, n0p, m0p, tail0, conv_w, conv_b, bg, norm_w)


def _s5_kernel(u_ref, sre0_ref, sim0_ref, wb_ref, are_ref, aim_ref, asre_ref, asim_ref,
               wcr_ref, wci_ref, dskip_ref, wglu_ref, bglu_ref, permt_ref,
               hs_ref, sre_out, sim_out,
               xr_ref, xi_ref, hr_ref, hi_ref, *, seg):
    bi = pl.program_id(1)
    n_state = xr_ref.shape[1]
    n_tiles = wb_ref.shape[0]

    @pl.when(bi == 0)
    def _():
        hr_ref[...] = sre0_ref[0]
        hi_ref[...] = sim0_ref[0]

    ub = u_ref[...].astype(BF16)
    tiles_per_slab = LANES // (2 * S5_GROUP)
    row = lax.broadcasted_iota(jnp.int32, (N_STREAMS, SCAN_LANES), 0)
    for sl in range(n_state // SCAN_LANES):
        cols = slice(sl * SCAN_LANES, (sl + 1) * SCAN_LANES)
        for j in range(sl * (SCAN_LANES // LANES), (sl + 1) * (SCAN_LANES // LANES)):
            cs = (j // tiles_per_slab) * LANES
            xj = _dot(ub[:, cs:cs + LANES], wb_ref[j])
            xr_ref[:, j * LANES:(j + 1) * LANES] = xj[:, :LANES]
            xi_ref[:, j * LANES:(j + 1) * LANES] = xj[:, LANES:]
        ar = jnp.broadcast_to(are_ref[:, cols], (N_STREAMS, SCAN_LANES))
        ai = jnp.broadcast_to(aim_ref[:, cols], (N_STREAMS, SCAN_LANES))

        er = xr_ref[0:N_STREAMS, cols]
        ei = xi_ref[0:N_STREAMS, cols]
        for i in range(1, seg):
            rows = slice(i * N_STREAMS, (i + 1) * N_STREAMS)
            nr = ar * er - ai * ei + xr_ref[rows, cols]
            ni = ar * ei + ai * er + xi_ref[rows, cols]
            xr_ref[rows, cols] = nr
            xi_ref[rows, cols] = ni
            er, ei = nr, ni

        asr = asre_ref[:, cols]
        asi = asim_ref[:, cols]
        cr = hr_ref[:, cols]
        ci = hi_ref[:, cols]
        cmr = jnp.zeros((N_STREAMS, SCAN_LANES), F32)
        cmi = jnp.zeros((N_STREAMS, SCAN_LANES), F32)
        for k in range(N_STREAMS):
            cmr = jnp.where(row == k, cr, cmr)
            cmi = jnp.where(row == k, ci, cmi)
            nr = asr * cr - asi * ci + er[k:k + 1, :]
            ni = asr * ci + asi * cr + ei[k:k + 1, :]
            cr, ci = nr, ni
        hr_ref[:, cols] = cr
        hi_ref[:, cols] = ci

        dr, di = cmr, cmi
        for i in range(seg):
            rows = slice(i * N_STREAMS, (i + 1) * N_STREAMS)
            dr, di = ar * dr - ai * di, ar * di + ai * dr
            xr_ref[rows, cols] += dr
            xi_ref[rows, cols] += di

    n_out_tiles = wcr_ref.shape[0]
    kw = n_state // n_out_tiles
    ys = []
    for qt in range(n_out_tiles):
        hr = xr_ref[:, qt * kw:(qt + 1) * kw].astype(BF16)
        hi = xi_ref[:, qt * kw:(qt + 1) * kw].astype(BF16)
        ys.append(_dot(hr, wcr_ref[qt]) + _dot(hi, wci_ref[qt]))
    y = jnp.concatenate(ys, axis=-1) + dskip_ref[...] * u_ref[...]
    gl = 0.5 * y * (1.0 + jnp.tanh(math.sqrt(2.0 / math.pi) * (y + 0.044715 * (y * y * y))))
    z = _dot(gl.astype(BF16), wglu_ref[...]) + bglu_ref[...]
    out = gl * jax.nn.sigmoid(z)
    hs_ref[...] = _dot(permt_ref[...], out.astype(BF16)).astype(BF16)

    @pl.when(bi == pl.num_programs(1) - 1)
    def _():
        sre_out[0] = hr_ref[...]
        sim_out[0] = hi_ref[...]


def _s5(u, sre0, sim0, wb, a_re, a_im, as_re, as_im, wcr, wci, dskip, wglu, bglu, permt,
        *, batch, seq, block):
    n, ds5 = u.shape
    n_state = a_re.shape[1]
    nb = seq // block
    rows = lambda w: pl.BlockSpec((block, w), lambda b, c: (b * nb + c, 0))
    per_b = pl.BlockSpec((1, 1, n_state), lambda b, c: (b, 0, 0))
    const = lambda a: pl.BlockSpec(a.shape, lambda b, c: (0,) * a.ndim)
    return pl.pallas_call(
        functools.partial(_s5_kernel, seg=block // N_STREAMS),
        grid=(batch, nb),
        in_specs=[rows(ds5), per_b, per_b, const(wb), const(a_re), const(a_im), const(as_re),
                  const(as_im), const(wcr), const(wci), const(dskip), const(wglu), const(bglu),
                  const(permt)],
        out_specs=[rows(ds5), per_b, per_b],
        out_shape=[jax.ShapeDtypeStruct((n, ds5), BF16),
                   jax.ShapeDtypeStruct((batch, 1, n_state), F32),
                   jax.ShapeDtypeStruct((batch, 1, n_state), F32)],
        scratch_shapes=[pltpu.VMEM((block, n_state), F32), pltpu.VMEM((block, n_state), F32),
                        pltpu.VMEM((1, n_state), F32), pltpu.VMEM((1, n_state), F32)],
        compiler_params=_params("arbitrary", "arbitrary"),
        name="s5",
    )(u, sre0, sim0, wb, a_re, a_im, as_re, as_im, wcr, wci, dskip, wglu, bglu, permt)


def _layer_norm(x, g, b):
    mu = jnp.mean(x, axis=-1, keepdims=True)
    xc = x - mu
    var = jnp.mean(xc * xc, axis=-1, keepdims=True)
    return xc * lax.rsqrt(var + LN_EPS) * g + b


def _first_max(cols):
    best = cols[0]
    for c in cols[1:]:
        best = jnp.maximum(best, c)
    flags = []
    taken = None
    for c in cols:
        hit = c == best
        if taken is not None:
            hit = jnp.logical_and(hit, jnp.logical_not(taken))
            taken = jnp.logical_or(taken, hit)
        else:
            taken = hit
        flags.append(hit)
    return best, flags


def _outproj_kernel(*refs, alpha, n_valid_blocks):
    x1_ref, route_ref = refs[-2:]
    i = pl.program_id(0)

    @pl.when(i < n_valid_blocks)
    def _():
        _outproj_tile(*refs, alpha=alpha)

    @pl.when(i >= n_valid_blocks)
    def _():
        x1_ref[...] = jnp.zeros_like(x1_ref)
        route_ref[...] = jnp.zeros_like(route_ref)


def _outproj_tile(x_ref, hm_ref, hs_ref, wo_ref, g1_ref, b1_ref, wrh_ref, wrl_ref, br_ref,
                  *rest, alpha):
    x1_ref, route_ref = rest[-2:]
    starts = range(0, x_ref.shape[0], OUT_ROWS)
    mixes = [_dot(jnp.concatenate([hm_ref[r0:r0 + OUT_ROWS, :], hs_ref[r0:r0 + OUT_ROWS, :]], axis=1), wo_ref[...])
             for r0 in starts]
    for r0, mix in zip(starts, mixes):
        _outproj_rows(r0, mix, x_ref, g1_ref, b1_ref, wrh_ref, wrl_ref, br_ref, x1_ref, route_ref, alpha=alpha)


def _outproj_rows(r0, mix, x_ref, g1_ref, b1_ref, wrh_ref, wrl_ref, br_ref, x1_ref, route_ref, *, alpha):
    tm = OUT_ROWS
    d = x_ref.shape[1]
    pitch = x1_ref.shape[0] // x_ref.shape[0]
    rows = slice(r0, r0 + tm)

    def store_token_tile(j, v):
        for g in range(tm // SUBLANES):
            first = (r0 + g * SUBLANES) * pitch + j
            x1_ref[pl.ds(first, SUBLANES, stride=pitch), :] = v[g * SUBLANES:(g + 1) * SUBLANES, :]

    x1 = _layer_norm(alpha * x_ref[rows, :] + mix, g1_ref[...], b1_ref[...])
    for j in range(d // LANES):
        store_token_tile(j, x1[:, j * LANES:(j + 1) * LANES])

    xh = x1.astype(BF16)
    xl = (x1 - xh.astype(F32)).astype(BF16)
    logits = (_dot(xh, wrh_ref[...]) + _dot(xl, wrh_ref[...]) + _dot(xh, wrl_ref[...])).T + br_ref[...]

    l1 = [logits[g:g + 1, :] for g in range(N_EXPERT_GROUPS)]
    m1, gsel = _first_max(l1)
    denom = l1[0] * 0.0
    for c in l1:
        denom = denom + jnp.exp(c - m1)
    p_g = 1.0 / denom
    gid = l1[0] * 0.0
    l2 = []
    for e in range(EXPERTS_PER_GROUP):
        acc = l1[0] * 0.0
        for g in range(N_EXPERT_GROUPS):
            c0 = N_EXPERT_GROUPS + g * EXPERTS_PER_GROUP + e
            acc = acc + jnp.where(gsel[g], logits[c0:c0 + 1, :], 0.0)
        l2.append(acc)
    for g in range(N_EXPERT_GROUPS):
        gid = gid + jnp.where(gsel[g], float(g), 0.0)

    v1, f1 = _first_max(l2)
    masked = [jnp.where(f1[e], -jnp.inf, l2[e]) for e in range(EXPERTS_PER_GROUP)]
    v2, f2 = _first_max(masked)
    e21 = jnp.exp(v2 - v1)
    w_first = 1.0 / (1.0 + e21)
    w_second = e21 / (1.0 + e21)
    pair_id = l1[0] * 0.0
    gate_a = l1[0] * 0.0
    gate_b = l1[0] * 0.0
    pid = 0
    for a in range(EXPERTS_PER_GROUP):
        for b in range(a + 1, EXPERTS_PER_GROUP):
            ab = jnp.logical_and(f1[a], f2[b])
            ba = jnp.logical_and(f1[b], f2[a])
            pair_id = pair_id + jnp.where(jnp.logical_or(ab, ba), float(pid), 0.0)
            gate_a = gate_a + jnp.where(ab, w_first, 0.0) + jnp.where(ba, w_second, 0.0)
            gate_b = gate_b + jnp.where(ab, w_second, 0.0) + jnp.where(ba, w_first, 0.0)
            pid += 1
    bin_id = gid * float(N_PAIRS) + pair_id
    row = lax.broadcasted_iota(jnp.int32, (LANES, tm), 0)
    route_t = jnp.where(row == 0, bin_id,
                        jnp.where(row == 1, p_g * gate_a, jnp.where(row == 2, p_g * gate_b, 0.0)))
    route_ref[:, rows] = route_t[0:SUBLANES, :]
    store_token_tile(d // LANES, route_t.T)
    for j in range(d // LANES + 1, pitch):
        store_token_tile(j, jnp.zeros((tm, LANES), F32))


def _outproj(x2, hm, hs, wo, g1, b1, wrh, wrl, br_col, shared, *, alpha, n_total, row_offset):
    n, d = x2.shape
    dm = hm.shape[1]
    groups = 2 if all(e % (2 * OUT_ROWS) == 0 for e in (n, row_offset, n_total - row_offset)) else 1
    tile = groups * OUT_ROWS
    blk0 = row_offset // tile
    n_blocks = n // tile
    pitch = _token_pitch(d)
    grid = n_blocks if shared is not None else n_total // tile - blk0
    rows = lambda w: pl.BlockSpec((tile, w), lambda i: (jnp.minimum(i, n_blocks - 1), 0))
    kern = functools.partial(_outproj_kernel, alpha=alpha, n_valid_blocks=n_blocks)
    in_specs = [rows(d), rows(dm), rows(hs.shape[1]), _resident(wo.shape),
                _resident(g1.shape), _resident(b1.shape), _resident(wrh.shape), _resident(wrl.shape),
                _resident(br_col.shape)]
    args = [x2, hm, hs, wo, g1, b1, wrh, wrl, br_col]
    aliases = {}
    if shared is not None:
        in_specs += [pl.BlockSpec(memory_space=pl.ANY), pl.BlockSpec(memory_space=pl.ANY)]
        aliases = {len(args): 0, len(args) + 1: 1}
        args = args + list(shared)
    return pl.pallas_call(
        kern,
        grid=(grid,),
        in_specs=in_specs,
        out_specs=[pl.BlockSpec((tile * pitch, LANES), lambda i: (i + blk0, 0)),
                   pl.BlockSpec((SUBLANES, tile), lambda i: (0, i + blk0))],
        out_shape=[jax.ShapeDtypeStruct((n_total * pitch, LANES), F32),
                   jax.ShapeDtypeStruct((SUBLANES, n_total), F32)],
        input_output_aliases=aliases,
        compiler_params=_params("arbitrary"),
        name="outproj",
    )(*args)


def _moe_kernel(ea_ref, eb_ref, nv_ref, npr_ref,
                idx_ref, idxn_ref, idxp_ref, wga_ref, wua_ref, wda_ref, wgb_ref, wub_ref, wdb_ref,
                g2_ref, b2_ref, x1_hbm,
                yp_hbm, ys_hbm,
                xbuf, obuf, xb_ref, gsem, ssem, *, alpha, n_prompt):
    t = pl.program_id(0)
    nt = pl.num_programs(0)
    slot = t % 2
    tm, d = obuf.shape[1:]
    pitch = xbuf.shape[2]
    n_model_tiles = d // LANES

    def gather_copy(tok, r, s):
        src = x1_hbm.at[pl.ds(pl.multiple_of(tok * pitch, SUBLANES), pitch), :]
        return pltpu.make_async_copy(src, xbuf.at[s, r // SUBLANES, :, r % SUBLANES, :], gsem.at[s])

    def tile_lanes(j):
        return xbuf[slot, :, j].reshape(tm, LANES)

    def scatter_copy(tok, r, s, to_prompt):
        dst = yp_hbm.at[pl.ds(tok, 1)] if to_prompt else ys_hbm.at[pl.ds(tok - n_prompt, 1)]
        return pltpu.make_async_copy(obuf.at[s, pl.ds(r, 1)], dst, ssem.at[s])

    def for_rows(lo, hi, fn):
        n_groups = (hi - lo) // ISSUE_UNROLL

        def group(gi, c):
            base = lo + gi * ISSUE_UNROLL
            for j in range(ISSUE_UNROLL):
                fn(base + j)
            return c

        def single(r, c):
            fn(r)
            return c

        lax.fori_loop(0, n_groups, group, 0)
        lax.fori_loop(lo + n_groups * ISSUE_UNROLL, hi, single, 0)

    def scatter_tile(ids, s, n_prompt_rows, n_rows):
        for_rows(0, n_prompt_rows, lambda r: scatter_copy(ids[0, 0, r], r, s, True).start())
        for_rows(n_prompt_rows, n_rows, lambda r: scatter_copy(ids[0, 0, r], r, s, False).start())

    def wait_rows(src, dst, sem, count):
        bit = tm
        while bit >= 1:
            @pl.when((count & bit) != 0)
            def _(bit=bit):
                pltpu.make_async_copy(src.at[pl.ds(0, bit)], dst.at[pl.ds(0, bit)], sem).wait()
            bit //= 2

    n_rows = nv_ref[t]
    mixed = npr_ref[t] != n_rows
    prev = jnp.maximum(t - 1, 0)
    n_prev = jnp.where(jnp.logical_and(t >= 1, npr_ref[prev] == nv_ref[prev]), nv_ref[prev], 0)

    def wait_gather(s):
        pltpu.make_async_copy(xbuf.at[1 - s], xbuf.at[s], gsem.at[s]).wait()

    @pl.when(t == 0)
    def _():
        for_rows(0, tm, lambda r: gather_copy(idx_ref[0, 0, r], r, 0).start())

    @pl.when(jnp.logical_or(t == 0, nv_ref[prev] > 0))
    def _():
        wait_gather(slot)

    @pl.when(t >= 2)
    def _():
        wait_rows(obuf.at[slot], yp_hbm, ssem.at[slot], nv_ref[t - 2])

    @pl.when(n_rows > 0)
    def _():
        for j in range(n_model_tiles):
            xb_ref[:, j * LANES:(j + 1) * LANES] = tile_lanes(j).astype(BF16)
        n_stages = 8

        def issue_neighbours(stage):
            for r in range(stage * tm // n_stages, (stage + 1) * tm // n_stages):
                gather_copy(idxn_ref[0, 0, r], r, 1 - slot).start(priority=r % 2)

                @pl.when(r < n_prev)
                def _(r=r):
                    scatter_copy(idxp_ref[0, 0, r], r, 1 - slot, True).start(priority=r % 2)

        routing = tile_lanes(n_model_tiles)
        gate_a = routing[:, 1:2]
        gate_b = routing[:, 2:3]
        xb = xb_ref[...]
        issue_neighbours(0)
        ha = _dot(xb, wga_ref[0])
        issue_neighbours(1)
        ha = (ha * jax.nn.sigmoid(ha)) * _dot(xb, wua_ref[0])
        issue_neighbours(2)
        ya = _dot(ha.astype(BF16), wda_ref[0])
        issue_neighbours(3)
        hb = _dot(xb, wgb_ref[0])
        issue_neighbours(4)
        hb = (hb * jax.nn.sigmoid(hb)) * _dot(xb, wub_ref[0])
        issue_neighbours(5)
        yb = _dot(hb.astype(BF16), wdb_ref[0])
        issue_neighbours(6)
        moe = gate_a * ya + gate_b * yb
        issue_neighbours(7)
        x1 = jnp.concatenate([tile_lanes(j) for j in range(n_model_tiles)], axis=1)
        obuf[slot] = _layer_norm(alpha * x1 + moe, g2_ref[...], b2_ref[...])

    @pl.when(n_rows == 0)
    def _():
        scatter_tile(idxp_ref, 1 - slot, n_prev, n_prev)

    now = jnp.logical_or(mixed, t == nt - 1)
    scatter_tile(idx_ref, slot, jnp.where(now, npr_ref[t], 0), jnp.where(now, n_rows, 0))

    @pl.when(t == nt - 1)
    def _():
        @pl.when(n_rows > 0)
        def _():
            wait_gather(1 - slot)

        wait_rows(obuf.at[slot], yp_hbm, ssem.at[slot], nv_ref[t])

        @pl.when(t >= 1)
        def _():
            wait_rows(obuf.at[1 - slot], yp_hbm, ssem.at[1 - slot], nv_ref[t - 1])


def _moe(x1_all, route_all, wg, wu, wd, g2, b2, *, alpha, n_prompt):
    n = route_all.shape[1]
    d = wg.shape[1]
    pitch = x1_all.shape[0] // n
    n_p = n_prompt
    n_s = n - n_p
    tm = MOE_TILE
    nt = n // tm + N_BINS if n % tm == 0 else (n + N_BINS * (tm - 1)) // tm + 1
    bins = route_all[0].astype(jnp.int32)

    onehot = (bins[:, None] == jnp.arange(N_BINS, dtype=jnp.int32)[None, :]).astype(jnp.int32)
    csum = jnp.cumsum(onehot, axis=0)
    rank = jnp.sum(csum * onehot, axis=1) - 1
    cnt = csum[n - 1]
    cnt_p = csum[n_p - 1]
    tiles_b = (cnt + tm - 1) // tm
    tile_end = jnp.cumsum(tiles_b)
    tile_start = tile_end - tiles_b
    slot_of_token = jnp.sum(onehot * tile_start[None, :], axis=1) * tm + rank
    idx = jnp.zeros((nt * tm,), jnp.int32).at[slot_of_token].set(
        jnp.arange(n, dtype=jnp.int32), unique_indices=True).reshape(nt, tm)
    tid = jnp.arange(nt, dtype=jnp.int32)
    used = tid < tile_end[N_BINS - 1]
    tbin = jnp.minimum(jnp.sum((tid[:, None] >= tile_end[None, :]).astype(jnp.int32), axis=1), N_BINS - 1)
    last_bin = jnp.max(jnp.where(cnt > 0, jnp.arange(N_BINS, dtype=jnp.int32), 0))
    tbin = jnp.where(used, tbin, last_bin)
    tile_onehot = (tbin[:, None] == jnp.arange(N_BINS, dtype=jnp.int32)[None, :]).astype(jnp.int32)
    lookup = lambda table: jnp.sum(tile_onehot * table[None, :], axis=1)
    kk = tid - lookup(tile_start)
    nvalid = jnp.where(used, jnp.clip(lookup(cnt) - kk * tm, 0, tm), 0).astype(jnp.int32)
    nprompt = jnp.where(used, jnp.clip(lookup(cnt_p) - kk * tm, 0, nvalid), 0).astype(jnp.int32)
    pairs = [(a, b) for a in range(EXPERTS_PER_GROUP) for b in range(a + 1, EXPERTS_PER_GROUP)]
    bin_group = np.arange(N_BINS) // N_PAIRS
    ea = lookup(jnp.asarray(bin_group * EXPERTS_PER_GROUP + np.array([p[0] for p in pairs] * N_EXPERT_GROUPS),
                            jnp.int32))
    eb = lookup(jnp.asarray(bin_group * EXPERTS_PER_GROUP + np.array([p[1] for p in pairs] * N_EXPERT_GROUPS),
                            jnp.int32))
    idx3 = idx.reshape(nt, 1, tm)

    dff = wg.shape[2]
    wspec_in = lambda sel: pl.BlockSpec((1, d, dff), lambda t, ea, eb, nv, npr: (sel(ea, eb)[t], 0, 0))
    wspec_out = lambda sel: pl.BlockSpec((1, dff, d), lambda t, ea, eb, nv, npr: (sel(ea, eb)[t], 0, 0))
    first = lambda a, b: a
    second = lambda a, b: b
    const2 = lambda a: pl.BlockSpec(a.shape, lambda t, *_: (0,) * a.ndim)
    smem_rows = lambda fn: pl.BlockSpec((1, 1, tm), fn, memory_space=pltpu.SMEM)
    grid_spec = pltpu.PrefetchScalarGridSpec(
        num_scalar_prefetch=4,
        grid=(nt,),
        in_specs=[smem_rows(lambda t, *_: (t, 0, 0)),
                  smem_rows(lambda t, *_: (jnp.minimum(t + 1, nt - 1), 0, 0)),
                  smem_rows(lambda t, *_: (jnp.maximum(t - 1, 0), 0, 0)),
                  wspec_in(first), wspec_in(first), wspec_out(first),
                  wspec_in(second), wspec_in(second), wspec_out(second),
                  const2(g2), const2(b2),
                  pl.BlockSpec(memory_space=pl.ANY)],
        out_specs=[pl.BlockSpec(memory_space=pl.ANY), pl.BlockSpec(memory_space=pl.ANY)],
        scratch_shapes=[pltpu.VMEM((2, tm // SUBLANES, pitch, SUBLANES, LANES), F32),
                        pltpu.VMEM((2, tm, d), F32),
                        pltpu.VMEM((tm, d), BF16),
                        pltpu.SemaphoreType.DMA((2,)), pltpu.SemaphoreType.DMA((2,))],
    )
    return pl.pallas_call(
        functools.partial(_moe_kernel, alpha=alpha, n_prompt=n_p),
        grid_spec=grid_spec,
        out_shape=[jax.ShapeDtypeStruct((n_p, d), F32), jax.ShapeDtypeStruct((n_s, d), F32)],
        compiler_params=_params("arbitrary"),
        name="moe",
    )(ea, eb, nvalid, nprompt, idx3, idx3, idx3, wg, wu, wd, wg, wu, wd, g2, b2, x1_all)


def _s5_tables(lam_re, lam_im, log_dt, b_re, b_im, c_re, c_im, seg_lens):
    n_groups, n_state = lam_re.shape
    dt = jnp.exp(log_dt.astype(F32))[:, None]
    lam_re = lam_re.astype(F32)
    lam_im = lam_im.astype(F32)
    mag = jnp.exp(lam_re * dt)
    a_re = mag * jnp.cos(lam_im * dt)
    a_im = mag * jnp.sin(lam_im * dt)
    e_re = a_re - 1.0
    e_im = a_im
    lam_sq = lam_re * lam_re + lam_im * lam_im
    coef_re = (e_re * lam_re + e_im * lam_im) / lam_sq
    coef_im = (e_im * lam_re - e_re * lam_im) / lam_sq
    b_re = b_re.astype(F32)
    b_im = b_im.astype(F32)
    bb_re = coef_re[..., None] * b_re - coef_im[..., None] * b_im
    bb_im = coef_re[..., None] * b_im + coef_im[..., None] * b_re

    gpt = LANES // n_state
    n_tiles = n_groups // gpt
    gps = LANES // S5_GROUP
    g_in_slab = (jnp.arange(n_groups) % gps)
    rows_onehot = jax.nn.one_hot(g_in_slab, gps, dtype=F32)
    w_re = jnp.einsum("gs,gpc->gscp", rows_onehot, bb_re).reshape(n_groups, LANES, n_state)
    w_im = jnp.einsum("gs,gpc->gscp", rows_onehot, bb_im).reshape(n_groups, LANES, n_state)
    w_re = w_re.reshape(n_tiles, gpt, LANES, n_state).transpose(0, 2, 1, 3).reshape(n_tiles, LANES, LANES)
    w_im = w_im.reshape(n_tiles, gpt, LANES, n_state).transpose(0, 2, 1, 3).reshape(n_tiles, LANES, LANES)
    wb = jnp.concatenate([w_re, w_im], axis=-1).astype(BF16)

    n_flat = n_groups * n_state
    out_w = 2 * LANES
    g_per_out = out_w // S5_GROUP
    n_out_tiles = n_groups // g_per_out
    oh = jax.nn.one_hot(jnp.arange(n_groups) % g_per_out, g_per_out, dtype=F32)
    wcr = jnp.einsum("gs,gcp->gpsc", oh, c_re.astype(F32)).reshape(n_out_tiles, g_per_out * n_state, out_w)
    wci = jnp.einsum("gs,gcp->gpsc", oh, -c_im.astype(F32)).reshape(n_out_tiles, g_per_out * n_state, out_w)

    def powers(k):
        return ((mag ** k) * jnp.cos(lam_im * dt * k)).reshape(1, n_flat), \
               ((mag ** k) * jnp.sin(lam_im * dt * k)).reshape(1, n_flat)

    return (wb, a_re.reshape(1, n_flat), a_im.reshape(1, n_flat), [powers(float(k)) for k in seg_lens],
            wcr.astype(BF16), wci.astype(BF16))


def _layer(x_p, x_s, st, lp):
    (w_in, b_gates, conv_w, conv_b, norm_w, lam_re, lam_im, log_dt, b_re, b_im, c_re, c_im,
     d_skip, w_glu, b_glu, w_out, ln1_g, ln1_b, w_r1, b_r1, w_r2, b_r2, w_gate, w_up, w_down,
     ln2_g, ln2_b, alpha) = lp
    state_c, state_n, state_m, state_conv, state_re, state_im = st
    bp, tp, d = x_p.shape
    bs, ts, _ = x_s.shape
    dm = norm_w.shape[0]
    head_dim = dm // M_HEADS
    ds5 = d_skip.shape[0]
    n_groups, n_state = lam_re.shape
    n_flat = n_groups * n_state

    o0 = 2 * dm
    o1 = o0 + 2 * dm
    wqk = w_in[:, :o0].astype(BF16)
    wv = w_in[:, o0:o0 + dm].astype(BF16)
    wo = w_in[:, o0 + dm:o1].astype(BF16)
    wg = jnp.pad(w_in[:, o1:o1 + 2 * M_HEADS], ((0, 0), (0, LANES - 2 * M_HEADS))).astype(BF16)
    wu = w_in[:, o1 + 2 * M_HEADS:].astype(BF16)
    bg = jnp.pad(b_gates.astype(F32), (0, LANES - 2 * M_HEADS)).reshape(1, LANES)
    w_out_b = w_out.astype(BF16)
    wr = jnp.concatenate([w_r1, jnp.transpose(w_r2, (1, 0, 2)).reshape(d, -1)], axis=1).astype(F32)
    n_logits = wr.shape[1]
    wr = jnp.pad(wr, ((0, 0), (0, LANES - n_logits)))
    wrh = wr.astype(BF16)
    wrl = (wr - wrh.astype(F32)).astype(BF16)
    br = jnp.pad(jnp.concatenate([b_r1, b_r2.reshape(-1)]).astype(F32), (0, LANES - n_logits))
    br = br.reshape(LANES, 1)

    blk_p = min(tp, PERM_ROWS)
    blk_s = min(ts, PERM_ROWS)
    seg_p = blk_p // N_STREAMS
    seg_s = blk_s // N_STREAMS
    wb, a_re, a_im, (as_p, as_s), wcr, wci = _s5_tables(lam_re, lam_im, log_dt, b_re, b_im, c_re, c_im,
                                                         (seg_p, seg_s))

    outs = []
    shared = None
    row_offset = 0
    for x, blk, seg, a_seg, zero_state in ((x_p, blk_p, seg_p, as_p, True), (x_s, blk_s, seg_s, as_s, False)):
        b, t, _ = x.shape
        x2 = x.reshape(b * t, d)
        perm_np = _stream_perm(blk, seg)
        perm = jnp.asarray(perm_np, BF16)
        permt = jnp.asarray(perm_np[:blk, :blk].T, BF16)
        qk, v, og, u, g = _inproj(x2, perm, wqk, wv, wo, wu, wg)

        if zero_state:
            c0 = jnp.zeros((b, M_HEADS, head_dim, head_dim), F32)
            n0 = jnp.zeros((b, M_HEADS, head_dim), F32)
            m0 = jnp.zeros((b, M_HEADS), F32)
            conv0 = jnp.zeros((b, CONV_W - 1, 2 * dm), F32)
            re0 = jnp.zeros((b, n_groups, n_state), F32)
            im0 = jnp.zeros((b, n_groups, n_state), F32)
        else:
            c0, n0, m0, conv0, re0, im0 = (state_c.astype(F32), state_n.astype(F32), state_m.astype(F32),
                                           state_conv.astype(F32), state_re.astype(F32), state_im.astype(F32))
        n0p = jnp.pad(n0, ((0, 0), (0, SUBLANES - M_HEADS), (0, 0)))
        m0p = jnp.broadcast_to(jnp.pad(m0, ((0, 0), (0, SUBLANES - M_HEADS)))[:, :, None], (b, SUBLANES, LANES))
        tail0 = jnp.pad(conv0, ((0, 0), (SUBLANES - (CONV_W - 1), 0), (0, 0)))
        chunk = min(t, LANES)
        hm, c_new, n_new, m_new = _mlstm(qk, v, og, g, c0, n0p, m0p, tail0, conv_w.astype(F32),
                                         conv_b.astype(F32).reshape(1, -1), bg, norm_w.astype(F32).reshape(1, -1),
                                         batch=b, seq=t, chunk=chunk)
        hs, re_new, im_new = _s5(u, re0.reshape(b, 1, n_flat), im0.reshape(b, 1, n_flat), wb, a_re, a_im,
                                 a_seg[0], a_seg[1], wcr, wci, d_skip.astype(F32).reshape(1, -1),
                                 w_glu.astype(BF16), b_glu.astype(F32).reshape(1, -1), permt,
                                 batch=b, seq=t, block=blk)
        shared = _outproj(x2, hm, hs, w_out_b, ln1_g.astype(F32).reshape(1, -1),
                          ln1_b.astype(F32).reshape(1, -1), wrh, wrl, br, shared, alpha=alpha,
                          n_total=bp * tp + bs * ts, row_offset=row_offset)
        row_offset += b * t
        qk_tail = qk.reshape(b, t, 2 * dm)[:, max(t - (CONV_W - 1), 0):]
        conv_new = qk_tail if t >= CONV_W - 1 else jnp.concatenate([conv0, qk_tail], axis=1)[:, -(CONV_W - 1):]
        states = (c_new, n_new[:, :M_HEADS], m_new[:, :M_HEADS, 0], conv_new,
                  re_new.reshape(b, n_groups, n_state), im_new.reshape(b, n_groups, n_state))
        outs.append(states)

    st_p, st_s = outs
    x1_all, route_all = shared
    yp, ys = _moe(x1_all, route_all, w_gate.astype(BF16), w_up.astype(BF16), w_down.astype(BF16),
                  ln2_g.astype(F32).reshape(1, -1), ln2_b.astype(F32).reshape(1, -1), alpha=alpha,
                  n_prompt=bp * tp)
    return yp.reshape(bp, tp, d), ys.reshape(bs, ts, d), st_p, st_s


def kernel(x_prompt, x_sample, state_mlstm_C, state_mlstm_n, state_mlstm_m, state_conv, state_s5_re, state_s5_im, w_in, b_gates, conv_w, conv_b, mlstm_norm_w, s5_lam_re, s5_lam_im, s5_log_dt, s5_b_re, s5_b_im, s5_c_re, s5_c_im, s5_d, w_glu, b_glu, w_out, ln1_g, ln1_b, w_r1, b_r1, w_r2, b_r2, w_gate, w_up, w_down, ln2_g, ln2_b):
    depth = w_in.shape[0]
    alpha = (2 * depth) ** 0.25
    yp, ys = x_prompt, x_sample
    sts_p, sts_s = [], []
    for l in range(depth):
        lp = (w_in[l], b_gates[l], conv_w[l], conv_b[l], mlstm_norm_w[l], s5_lam_re[l], s5_lam_im[l],
              s5_log_dt[l], s5_b_re[l], s5_b_im[l], s5_c_re[l], s5_c_im[l], s5_d[l], w_glu[l], b_glu[l],
              w_out[l], ln1_g[l], ln1_b[l], w_r1[l], b_r1[l], w_r2[l], b_r2[l], w_gate[l], w_up[l],
              w_down[l], ln2_g[l], ln2_b[l], alpha)
        st = (state_mlstm_C[l], state_mlstm_n[l], state_mlstm_m[l], state_conv[l], state_s5_re[l],
              state_s5_im[l])
        yp, ys, sp, ss = _layer(yp, ys, st, lp)
        sts_p.append(sp)
        sts_s.append(ss)
    stack = lambda sts, i: jnp.stack([s[i] for s in sts])
    return (yp, ys) + tuple(stack(sts_p, i) for i in range(6)) + tuple(stack(sts_s, i) for i in range(6))
```

```python
import functools
import math

import jax
import jax.numpy as jnp
import numpy as np
from jax import lax
from jax.experimental import pallas as pl
from jax.experimental.pallas import tpu as pltpu

F32 = jnp.float32
BF16 = jnp.bfloat16

LANES = 128
SUBLANES = 8
VMEM_LIMIT = 56 * 1024 * 1024

M_HEADS = 4
CONV_W = 4
S5_GROUP = 16
S5_STATE = 64
N_EXPERT_GROUPS = 4
EXPERTS_PER_GROUP = 4
N_PAIRS = 6
N_BINS = N_EXPERT_GROUPS * N_PAIRS
LN_EPS = 1e-5
NEG = -1e30

ROW_TILE = 256
PERM_ROWS = 256
OUT_ROWS = 256
MOE_TILE = 256
ISSUE_UNROLL = 8
N_STREAMS = SUBLANES
SCAN_LANES = 512


def _dot(a, b):
    return jnp.dot(a, b, preferred_element_type=F32)


def _dot_nt(a, b):
    return lax.dot_general(a, b, (((1,), (1,)), ((), ())), preferred_element_type=F32)


def _resident(shape):
    nd = len(shape)
    return pl.BlockSpec(shape, lambda *_: (0,) * nd, pipeline_mode=pl.Buffered(1))


def _params(*sem):
    return pltpu.CompilerParams(dimension_semantics=sem, vmem_limit_bytes=VMEM_LIMIT)


def _token_pitch(d):
    return -(-(d // LANES + 1) // SUBLANES) * SUBLANES


def _stream_perm(block, seg):
    n_seg = block // seg
    r = np.arange(block)
    src = (r % n_seg) * seg + r // n_seg
    p = np.zeros((block, block), np.float32)
    p[r, src] = 1.0
    reps = PERM_ROWS // block
    return np.kron(np.eye(reps, dtype=np.float32), p)


def _inproj_kernel(x_ref, perm_ref, wqk_ref, wv_ref, wo_ref, wu_ref, wg_ref,
                   qk_ref, v_ref, og_ref, u_ref, g_ref):
    xb = x_ref[...].astype(BF16)
    qk_ref[...] = _dot(xb, wqk_ref[...])
    v_ref[...] = _dot(xb, wv_ref[...]).astype(BF16)
    og_ref[...] = _dot(xb, wo_ref[...])
    g_ref[...] = _dot(xb, wg_ref[...])
    xp = jnp.concatenate([_dot(perm_ref[...], xb[r0:r0 + PERM_ROWS, :]).astype(BF16)
                          for r0 in range(0, xb.shape[0], PERM_ROWS)], axis=0)
    u_ref[...] = _dot(xp, wu_ref[...])


def _inproj(x2, perm, wqk, wv, wo, wu, wg):
    n, d = x2.shape
    dm = wv.shape[1]
    ds5 = wu.shape[1]
    tile = ROW_TILE if n % ROW_TILE == 0 else PERM_ROWS
    rows = lambda w: pl.BlockSpec((tile, w), lambda i: (i, 0))
    return pl.pallas_call(
        _inproj_kernel,
        grid=(n // tile,),
        in_specs=[rows(d), _resident(perm.shape), _resident(wqk.shape), _resident(wv.shape),
                  _resident(wo.shape), _resident(wu.shape), _resident(wg.shape)],
        out_specs=[rows(2 * dm), rows(dm), rows(dm), rows(ds5), rows(LANES)],
        out_shape=[jax.ShapeDtypeStruct((n, 2 * dm), F32), jax.ShapeDtypeStruct((n, dm), BF16),
                   jax.ShapeDtypeStruct((n, dm), F32), jax.ShapeDtypeStruct((n, ds5), F32),
                   jax.ShapeDtypeStruct((n, LANES), F32)],
        compiler_params=_params("arbitrary"),
        name="inproj",
    )(x2, perm, wqk, wv, wo, wu, wg)


def _cumsum_rows(x):
    n = x.shape[0]
    row = lax.broadcasted_iota(jnp.int32, x.shape, 0)
    s = 1
    while s < n:
        x = x + jnp.where(row >= s, pltpu.roll(x, s, 0), 0.0)
        s *= 2
    return x


def _transpose_gate_cols(x):
    sel = (lax.broadcasted_iota(jnp.int32, (SUBLANES, LANES), 0)
           == lax.broadcasted_iota(jnp.int32, (SUBLANES, LANES), 1)).astype(BF16)
    hi = x.astype(BF16)
    r1 = x - hi.astype(F32)
    mid = r1.astype(BF16)
    lo = (r1 - mid.astype(F32)).astype(BF16)
    return _dot_nt(sel, hi) + _dot_nt(sel, mid) + _dot_nt(sel, lo)


def _mlstm_kernel(qk_ref, v_ref, og_ref, g_ref, c0_ref, n0_ref, m0_ref, tail0_ref,
                  cw_ref, cb_ref, bg_ref, nw_ref,
                  h_ref, c_out, n_out, m_out,
                  ext_ref, c_ref, n_ref, m_ref, *, chunk, head_dim):
    ci = pl.program_id(1)
    dm = M_HEADS * head_dim
    L = chunk

    @pl.when(ci == 0)
    def _():
        c_ref[...] = c0_ref[0]
        n_ref[...] = n0_ref[0]
        m_ref[...] = m0_ref[0]
        ext_ref[0:SUBLANES, :] = tail0_ref[0]

    ext_ref[SUBLANES:SUBLANES + L, :] = qk_ref[...]
    full = ext_ref[...]
    acc = cb_ref[...] + full[SUBLANES:, :] * cw_ref[CONV_W - 1:CONV_W, :]
    for back in range(1, CONV_W):
        tap = cw_ref[CONV_W - 1 - back:CONV_W - back, :]
        acc = acc + pltpu.roll(full, back, 0)[SUBLANES:, :] * tap
    tail = ext_ref[L:L + SUBLANES, :]
    ext_ref[0:SUBLANES, :] = tail
    qk = acc * jax.nn.sigmoid(acc)

    gates = g_ref[...] + bg_ref[...]
    fpre = pltpu.roll(gates, LANES - M_HEADS, 1)
    logf = jnp.minimum(fpre, 0.0) - jnp.log1p(jnp.exp(-jnp.abs(fpre)))
    bcum = _cumsum_rows(logf)
    rrow = _transpose_gate_cols(gates - bcum)
    tri = (lax.broadcasted_iota(jnp.int32, (L, L), 0) >= lax.broadcasted_iota(jnp.int32, (L, L), 1))

    for h in range(M_HEADS):
        lo, hi = h * head_dim, (h + 1) * head_dim
        q = qk[:, lo:hi]
        k = qk[:, dm + lo:dm + hi] * (head_dim ** -0.5)
        v = v_ref[:, lo:hi]
        qb = q.astype(BF16)
        kb = k.astype(BF16)
        b_col = bcum[:, h:h + 1]
        ig_col = gates[:, h:h + 1]
        m_prev = m_ref[h:h + 1, 0:1]
        c_prev = c_ref[h]
        n_prev = n_ref[h:h + 1, :]

        dmat = jnp.where(tri, b_col + rrow[h:h + 1, :], NEG)
        inter = b_col + m_prev
        m_t = jnp.maximum(inter, jnp.max(dmat, axis=-1, keepdims=True))
        w_intra = jnp.exp(dmat - m_t)
        w_inter = jnp.exp(inter - m_t)
        s = _dot_nt(qb, kb) * w_intra
        num = w_inter * _dot(qb, c_prev.astype(BF16)) + _dot(s.astype(BF16), v)
        den = w_inter * jnp.sum(q * n_prev, axis=-1, keepdims=True) + jnp.sum(s, axis=-1, keepdims=True)
        hh = num / jnp.maximum(jnp.abs(den), jnp.exp(-m_t))

        m_new = m_t[L - 1:L, :]
        b_last = b_col[L - 1:L, :]
        w_s = jnp.exp(b_last - b_col + ig_col - m_new)
        decay = jnp.exp(b_last + m_prev - m_new)
        kw = k * w_s
        c_ref[h] = decay * c_prev + _dot(kw.T.astype(BF16), v)
        n_ref[h:h + 1, :] = decay * n_prev + jnp.sum(kw, axis=0, keepdims=True)
        m_ref[h:h + 1, :] = jnp.broadcast_to(m_new, (1, LANES))

        mu = jnp.mean(hh, axis=-1, keepdims=True)
        hc = hh - mu
        var = jnp.mean(hc * hc, axis=-1, keepdims=True)
        hn = hc * lax.rsqrt(var + LN_EPS) * nw_ref[:, lo:hi]
        h_ref[:, lo:hi] = (jax.nn.sigmoid(og_ref[:, lo:hi]) * hn).astype(BF16)

    @pl.when(ci == pl.num_programs(1) - 1)
    def _():
        c_out[0] = c_ref[...]
        n_out[0] = n_ref[...]
        m_out[0] = m_ref[...]


def _mlstm(qk, v, og, g, c0, n0p, m0p, tail0, conv_w, conv_b, bg, norm_w, *, batch, seq, chunk):
    n, dm = v.shape
    head_dim = dm // M_HEADS
    nc = seq // chunk
    rows = lambda w: pl.BlockSpec((chunk, w), lambda b, c: (b * nc + c, 0))
    per_b = lambda *s: pl.BlockSpec((1,) + s, lambda b, c: (b,) + (0,) * len(s))
    const = lambda a: pl.BlockSpec(a.shape, lambda b, c: (0,) * a.ndim)
    return pl.pallas_call(
        functools.partial(_mlstm_kernel, chunk=chunk, head_dim=head_dim),
        grid=(batch, nc),
        in_specs=[rows(2 * dm), rows(dm), rows(dm), rows(LANES),
                  per_b(M_HEADS, head_dim, head_dim), per_b(SUBLANES, head_dim), per_b(SUBLANES, LANES),
                  per_b(SUBLANES, 2 * dm),
                  const(conv_w), const(conv_b), const(bg), const(norm_w)],
        out_specs=[rows(dm), per_b(M_HEADS, head_dim, head_dim), per_b(SUBLANES, head_dim),
                   per_b(SUBLANES, LANES)],
        out_shape=[jax.ShapeDtypeStruct((n, dm), BF16),
                   jax.ShapeDtypeStruct((batch, M_HEADS, head_dim, head_dim), F32),
                   jax.ShapeDtypeStruct((batch, SUBLANES, head_dim), F32),
                   jax.ShapeDtypeStruct((batch, SUBLANES, LANES), F32)],
        scratch_shapes=[pltpu.VMEM((chunk + SUBLANES, 2 * dm), F32),
                        pltpu.VMEM((M_HEADS, head_dim, head_dim), F32),
                        pltpu.VMEM((SUBLANES, head_dim), F32),
                        pltpu.VMEM((SUBLANES, LANES), F32)],
        compiler_params=_params("arbitrary", "arbitrary"),
        name="mlstm",
    )(qk, v, og, g, c0, n0p, m0p, tail0, conv_w, conv_b, bg, norm_w)


def _s5_kernel(u_ref, sre0_ref, sim0_ref, wb_ref, are_ref, aim_ref, asre_ref, asim_ref,
               wcr_ref, wci_ref, dskip_ref, wglu_ref, bglu_ref, permt_ref,
               hs_ref, sre_out, sim_out,
               xr_ref, xi_ref, hr_ref, hi_ref, *, seg):
    bi = pl.program_id(1)
    n_state = xr_ref.shape[1]
    n_tiles = wb_ref.shape[0]

    @pl.when(bi == 0)
    def _():
        hr_ref[...] = sre0_ref[0]
        hi_ref[...] = sim0_ref[0]

    ub = u_ref[...].astype(BF16)
    tiles_per_slab = LANES // (2 * S5_GROUP)
    row = lax.broadcasted_iota(jnp.int32, (N_STREAMS, SCAN_LANES), 0)
    for sl in range(n_state // SCAN_LANES):
        cols = slice(sl * SCAN_LANES, (sl + 1) * SCAN_LANES)
        for j in range(sl * (SCAN_LANES // LANES), (sl + 1) * (SCAN_LANES // LANES)):
            cs = (j // tiles_per_slab) * LANES
            xj = _dot(ub[:, cs:cs + LANES], wb_ref[j])
            xr_ref[:, j * LANES:(j + 1) * LANES] = xj[:, :LANES]
            xi_ref[:, j * LANES:(j + 1) * LANES] = xj[:, LANES:]
        ar = jnp.broadcast_to(are_ref[:, cols], (N_STREAMS, SCAN_LANES))
        ai = jnp.broadcast_to(aim_ref[:, cols], (N_STREAMS, SCAN_LANES))

        er = xr_ref[0:N_STREAMS, cols]
        ei = xi_ref[0:N_STREAMS, cols]
        for i in range(1, seg):
            rows = slice(i * N_STREAMS, (i + 1) * N_STREAMS)
            nr = ar * er - ai * ei + xr_ref[rows, cols]
            ni = ar * ei + ai * er + xi_ref[rows, cols]
            xr_ref[rows, cols] = nr
            xi_ref[rows, cols] = ni
            er, ei = nr, ni

        asr = asre_ref[:, cols]
        asi = asim_ref[:, cols]
        cr = hr_ref[:, cols]
        ci = hi_ref[:, cols]
        cmr = jnp.zeros((N_STREAMS, SCAN_LANES), F32)
        cmi = jnp.zeros((N_STREAMS, SCAN_LANES), F32)
        for k in range(N_STREAMS):
            cmr = jnp.where(row == k, cr, cmr)
            cmi = jnp.where(row == k, ci, cmi)
            nr = asr * cr - asi * ci + er[k:k + 1, :]
            ni = asr * ci + asi * cr + ei[k:k + 1, :]
            cr, ci = nr, ni
        hr_ref[:, cols] = cr
        hi_ref[:, cols] = ci

        dr, di = cmr, cmi
        for i in range(seg):
            rows = slice(i * N_STREAMS, (i + 1) * N_STREAMS)
            dr, di = ar * dr - ai * di, ar * di + ai * dr
            xr_ref[rows, cols] += dr
            xi_ref[rows, cols] += di

    n_out_tiles = wcr_ref.shape[0]
    kw = n_state // n_out_tiles
    ys = []
    for qt in range(n_out_tiles):
        hr = xr_ref[:, qt * kw:(qt + 1) * kw].astype(BF16)
        hi = xi_ref[:, qt * kw:(qt + 1) * kw].astype(BF16)
        ys.append(_dot(hr, wcr_ref[qt]) + _dot(hi, wci_ref[qt]))
    y = jnp.concatenate(ys, axis=-1) + dskip_ref[...] * u_ref[...]
    gl = 0.5 * y * (1.0 + jnp.tanh(math.sqrt(2.0 / math.pi) * (y + 0.044715 * (y * y * y))))
    z = _dot(gl.astype(BF16), wglu_ref[...]) + bglu_ref[...]
    out = gl * jax.nn.sigmoid(z)
    hs_ref[...] = _dot(permt_ref[...], out.astype(BF16)).astype(BF16)

    @pl.when(bi == pl.num_programs(1) - 1)
    def _():
        sre_out[0] = hr_ref[...]
        sim_out[0] = hi_ref[...]


def _s5(u, sre0, sim0, wb, a_re, a_im, as_re, as_im, wcr, wci, dskip, wglu, bglu, permt,
        *, batch, seq, block):
    n, ds5 = u.shape
    n_state = a_re.shape[1]
    nb = seq // block
    rows = lambda w: pl.BlockSpec((block, w), lambda b, c: (b * nb + c, 0))
    per_b = pl.BlockSpec((1, 1, n_state), lambda b, c: (b, 0, 0))
    const = lambda a: pl.BlockSpec(a.shape, lambda b, c: (0,) * a.ndim)
    return pl.pallas_call(
        functools.partial(_s5_kernel, seg=block // N_STREAMS),
        grid=(batch, nb),
        in_specs=[rows(ds5), per_b, per_b, const(wb), const(a_re), const(a_im), const(as_re),
                  const(as_im), const(wcr), const(wci), const(dskip), const(wglu), const(bglu),
                  const(permt)],
        out_specs=[rows(ds5), per_b, per_b],
        out_shape=[jax.ShapeDtypeStruct((n, ds5), BF16),
                   jax.ShapeDtypeStruct((batch, 1, n_state), F32),
                   jax.ShapeDtypeStruct((batch, 1, n_state), F32)],
        scratch_shapes=[pltpu.VMEM((block, n_state), F32), pltpu.VMEM((block, n_state), F32),
                        pltpu.VMEM((1, n_state), F32), pltpu.VMEM((1, n_state), F32)],
        compiler_params=_params("arbitrary", "arbitrary"),
        name="s5",
    )(u, sre0, sim0, wb, a_re, a_im, as_re, as_im, wcr, wci, dskip, wglu, bglu, permt)


def _layer_norm(x, g, b):
    mu = jnp.mean(x, axis=-1, keepdims=True)
    xc = x - mu
    var = jnp.mean(xc * xc, axis=-1, keepdims=True)
    return xc * lax.rsqrt(var + LN_EPS) * g + b


def _first_max(cols):
    best = cols[0]
    for c in cols[1:]:
        best = jnp.maximum(best, c)
    flags = []
    taken = None
    for c in cols:
        hit = c == best
        if taken is not None:
            hit = jnp.logical_and(hit, jnp.logical_not(taken))
            taken = jnp.logical_or(taken, hit)
        else:
            taken = hit
        flags.append(hit)
    return best, flags


def _outproj_kernel(*refs, alpha, n_valid_blocks):
    x1_ref, route_ref = refs[-2:]
    i = pl.program_id(0)

    @pl.when(i < n_valid_blocks)
    def _():
        _outproj_tile(*refs, alpha=alpha)

    @pl.when(i >= n_valid_blocks)
    def _():
        x1_ref[...] = jnp.zeros_like(x1_ref)
        route_ref[...] = jnp.zeros_like(route_ref)


def _outproj_tile(x_ref, hm_ref, hs_ref, wo_ref, g1_ref, b1_ref, wrh_ref, wrl_ref, br_ref,
                  *rest, alpha):
    x1_ref, route_ref = rest[-2:]
    starts = range(0, x_ref.shape[0], OUT_ROWS)
    mixes = [_dot(jnp.concatenate([hm_ref[r0:r0 + OUT_ROWS, :], hs_ref[r0:r0 + OUT_ROWS, :]], axis=1), wo_ref[...])
             for r0 in starts]
    for r0, mix in zip(starts, mixes):
        _outproj_rows(r0, mix, x_ref, g1_ref, b1_ref, wrh_ref, wrl_ref, br_ref, x1_ref, route_ref, alpha=alpha)


def _outproj_rows(r0, mix, x_ref, g1_ref, b1_ref, wrh_ref, wrl_ref, br_ref, x1_ref, route_ref, *, alpha):
    tm = OUT_ROWS
    d = x_ref.shape[1]
    pitch = x1_ref.shape[0] // x_ref.shape[0]
    rows = slice(r0, r0 + tm)

    def store_token_tile(j, v):
        for g in range(tm // SUBLANES):
            first = (r0 + g * SUBLANES) * pitch + j
            x1_ref[pl.ds(first, SUBLANES, stride=pitch), :] = v[g * SUBLANES:(g + 1) * SUBLANES, :]

    x1 = _layer_norm(alpha * x_ref[rows, :] + mix, g1_ref[...], b1_ref[...])
    for j in range(d // LANES):
        store_token_tile(j, x1[:, j * LANES:(j + 1) * LANES])

    xh = x1.astype(BF16)
    xl = (x1 - xh.astype(F32)).astype(BF16)
    logits = (_dot(xh, wrh_ref[...]) + _dot(xl, wrh_ref[...]) + _dot(xh, wrl_ref[...])).T + br_ref[...]

    l1 = [logits[g:g + 1, :] for g in range(N_EXPERT_GROUPS)]
    m1, gsel = _first_max(l1)
    denom = l1[0] * 0.0
    for c in l1:
        denom = denom + jnp.exp(c - m1)
    p_g = 1.0 / denom
    gid = l1[0] * 0.0
    l2 = []
    for e in range(EXPERTS_PER_GROUP):
        acc = l1[0] * 0.0
        for g in range(N_EXPERT_GROUPS):
            c0 = N_EXPERT_GROUPS + g * EXPERTS_PER_GROUP + e
            acc = acc + jnp.where(gsel[g], logits[c0:c0 + 1, :], 0.0)
        l2.append(acc)
    for g in range(N_EXPERT_GROUPS):
        gid = gid + jnp.where(gsel[g], float(g), 0.0)

    v1, f1 = _first_max(l2)
    masked = [jnp.where(f1[e], -jnp.inf, l2[e]) for e in range(EXPERTS_PER_GROUP)]
    v2, f2 = _first_max(masked)
    e21 = jnp.exp(v2 - v1)
    w_first = 1.0 / (1.0 + e21)
    w_second = e21 / (1.0 + e21)
    pair_id = l1[0] * 0.0
    gate_a = l1[0] * 0.0
    gate_b = l1[0] * 0.0
    pid = 0
    for a in range(EXPERTS_PER_GROUP):
        for b in range(a + 1, EXPERTS_PER_GROUP):
            ab = jnp.logical_and(f1[a], f2[b])
            ba = jnp.logical_and(f1[b], f2[a])
            pair_id = pair_id + jnp.where(jnp.logical_or(ab, ba), float(pid), 0.0)
            gate_a = gate_a + jnp.where(ab, w_first, 0.0) + jnp.where(ba, w_second, 0.0)
            gate_b = gate_b + jnp.where(ab, w_second, 0.0) + jnp.where(ba, w_first, 0.0)
            pid += 1
    bin_id = gid * float(N_PAIRS) + pair_id
    row = lax.broadcasted_iota(jnp.int32, (LANES, tm), 0)
    route_t = jnp.where(row == 0, bin_id,
                        jnp.where(row == 1, p_g * gate_a, jnp.where(row == 2, p_g * gate_b, 0.0)))
    route_ref[:, rows] = route_t[0:SUBLANES, :]
    store_token_tile(d // LANES, route_t.T)
    for j in range(d // LANES + 1, pitch):
        store_token_tile(j, jnp.zeros((tm, LANES), F32))


def _outproj(x2, hm, hs, wo, g1, b1, wrh, wrl, br_col, shared, *, alpha, n_total, row_offset):
    n, d = x2.shape
    dm = hm.shape[1]
    groups = 2 if all(e % (2 * OUT_ROWS) == 0 for e in (n, row_offset, n_total - row_offset)) else 1
    tile = groups * OUT_ROWS
    blk0 = row_offset // tile
    n_blocks = n // tile
    pitch = _token_pitch(d)
    grid = n_blocks if shared is not None else n_total // tile - blk0
    rows = lambda w: pl.BlockSpec((tile, w), lambda i: (jnp.minimum(i, n_blocks - 1), 0))
    kern = functools.partial(_outproj_kernel, alpha=alpha, n_valid_blocks=n_blocks)
    in_specs = [rows(d), rows(dm), rows(hs.shape[1]), _resident(wo.shape),
                _resident(g1.shape), _resident(b1.shape), _resident(wrh.shape), _resident(wrl.shape),
                _resident(br_col.shape)]
    args = [x2, hm, hs, wo, g1, b1, wrh, wrl, br_col]
    aliases = {}
    if shared is not None:
        in_specs += [pl.BlockSpec(memory_space=pl.ANY), pl.BlockSpec(memory_space=pl.ANY)]
        aliases = {len(args): 0, len(args) + 1: 1}
        args = args + list(shared)
    return pl.pallas_call(
        kern,
        grid=(grid,),
        in_specs=in_specs,
        out_specs=[pl.BlockSpec((tile * pitch, LANES), lambda i: (i + blk0, 0)),
                   pl.BlockSpec((SUBLANES, tile), lambda i: (0, i + blk0))],
        out_shape=[jax.ShapeDtypeStruct((n_total * pitch, LANES), F32),
                   jax.ShapeDtypeStruct((SUBLANES, n_total), F32)],
        input_output_aliases=aliases,
        compiler_params=_params("arbitrary"),
        name="outproj",
    )(*args)


def _moe_kernel(ea_ref, eb_ref, nv_ref, npr_ref,
                idx_ref, idxn_ref, idxp_ref, wga_ref, wua_ref, wda_ref, wgb_ref, wub_ref, wdb_ref,
                g2_ref, b2_ref, x1_hbm,
                yp_hbm, ys_hbm,
                xbuf, obuf, xb_ref, gsem, ssem, *, alpha, n_prompt):
    t = pl.program_id(0)
    nt = pl.num_programs(0)
    slot = t % 2
    tm, d = obuf.shape[1:]
    pitch = xbuf.shape[1] // tm
    n_model_tiles = d // LANES

    def gather_copy(tok, r, s):
        src = x1_hbm.at[pl.ds(pl.multiple_of(tok * pitch, SUBLANES), pitch), :]
        dst = xbuf.at[s, pl.ds(pl.multiple_of(r * pitch, SUBLANES), pitch), :]
        return pltpu.make_async_copy(src, dst, gsem.at[s])

    def tile_lanes(j):
        return xbuf[slot, pl.ds(j, tm, stride=pitch), :]

    def scatter_copy(tok, r, s, to_prompt):
        dst = yp_hbm.at[pl.ds(tok, 1)] if to_prompt else ys_hbm.at[pl.ds(tok - n_prompt, 1)]
        return pltpu.make_async_copy(obuf.at[s, pl.ds(r, 1)], dst, ssem.at[s])

    def for_rows(lo, hi, fn):
        n_groups = (hi - lo) // ISSUE_UNROLL

        def group(gi, c):
            base = lo + gi * ISSUE_UNROLL
            for j in range(ISSUE_UNROLL):
                fn(base + j)
            return c

        def single(r, c):
            fn(r)
            return c

        lax.fori_loop(0, n_groups, group, 0)
        lax.fori_loop(lo + n_groups * ISSUE_UNROLL, hi, single, 0)

    def scatter_tile(ids, s, n_prompt_rows, n_rows):
        for_rows(0, n_prompt_rows, lambda r: scatter_copy(ids[0, 0, r], r, s, True).start())
        for_rows(n_prompt_rows, n_rows, lambda r: scatter_copy(ids[0, 0, r], r, s, False).start())

    def wait_rows(src, dst, sem, count):
        bit = tm
        while bit >= 1:
            @pl.when((count & bit) != 0)
            def _(bit=bit):
                pltpu.make_async_copy(src.at[pl.ds(0, bit)], dst.at[pl.ds(0, bit)], sem).wait()
            bit //= 2

    n_rows = nv_ref[t]
    mixed = npr_ref[t] != n_rows
    prev = jnp.maximum(t - 1, 0)
    n_prev = jnp.where(jnp.logical_and(t >= 1, npr_ref[prev] == nv_ref[prev]), nv_ref[prev], 0)

    def wait_gather(s):
        pltpu.make_async_copy(x1_hbm.at[pl.ds(0, tm * pitch), :], xbuf.at[s], gsem.at[s]).wait()

    @pl.when(t == 0)
    def _():
        for_rows(0, tm, lambda r: gather_copy(idx_ref[0, 0, r], r, 0).start())

    @pl.when(jnp.logical_or(t == 0, nv_ref[prev] > 0))
    def _():
        wait_gather(slot)

    @pl.when(t >= 2)
    def _():
        wait_rows(obuf.at[slot], yp_hbm, ssem.at[slot], nv_ref[t - 2])

    @pl.when(n_rows > 0)
    def _():
        for j in range(n_model_tiles):
            xb_ref[:, j * LANES:(j + 1) * LANES] = tile_lanes(j).astype(BF16)
        n_stages = 8

        def issue_neighbours(stage):
            for r in range(stage * tm // n_stages, (stage + 1) * tm // n_stages):
                gather_copy(idxn_ref[0, 0, r], r, 1 - slot).start(priority=r % 2)

                @pl.when(r < n_prev)
                def _(r=r):
                    scatter_copy(idxp_ref[0, 0, r], r, 1 - slot, True).start(priority=r % 2)

        routing = tile_lanes(n_model_tiles)
        gate_a = routing[:, 1:2]
        gate_b = routing[:, 2:3]
        xb = xb_ref[...]
        issue_neighbours(0)
        ha = _dot(xb, wga_ref[0])
        issue_neighbours(1)
        ha = (ha * jax.nn.sigmoid(ha)) * _dot(xb, wua_ref[0])
        issue_neighbours(2)
        ya = _dot(ha.astype(BF16), wda_ref[0])
        issue_neighbours(3)
        hb = _dot(xb, wgb_ref[0])
        issue_neighbours(4)
        hb = (hb * jax.nn.sigmoid(hb)) * _dot(xb, wub_ref[0])
        issue_neighbours(5)
        yb = _dot(hb.astype(BF16), wdb_ref[0])
        issue_neighbours(6)
        moe = gate_a * ya + gate_b * yb
        issue_neighbours(7)
        x1 = jnp.concatenate([tile_lanes(j) for j in range(n_model_tiles)], axis=1)
        obuf[slot] = _layer_norm(alpha * x1 + moe, g2_ref[...], b2_ref[...])

    @pl.when(n_rows == 0)
    def _():
        scatter_tile(idxp_ref, 1 - slot, n_prev, n_prev)

    now = jnp.logical_or(mixed, t == nt - 1)
    scatter_tile(idx_ref, slot, jnp.where(now, npr_ref[t], 0), jnp.where(now, n_rows, 0))

    @pl.when(t == nt - 1)
    def _():
        @pl.when(n_rows > 0)
        def _():
            wait_gather(1 - slot)

        wait_rows(obuf.at[slot], yp_hbm, ssem.at[slot], nv_ref[t])

        @pl.when(t >= 1)
        def _():
            wait_rows(obuf.at[1 - slot], yp_hbm, ssem.at[1 - slot], nv_ref[t - 1])


def _moe(x1_all, route_all, wg, wu, wd, g2, b2, *, alpha, n_prompt):
    n = route_all.shape[1]
    d = wg.shape[1]
    pitch = x1_all.shape[0] // n
    n_p = n_prompt
    n_s = n - n_p
    tm = MOE_TILE
    nt = n // tm + N_BINS if n % tm == 0 else (n + N_BINS * (tm - 1)) // tm + 1
    bins = route_all[0].astype(jnp.int32)

    onehot = (bins[:, None] == jnp.arange(N_BINS, dtype=jnp.int32)[None, :]).astype(jnp.int32)
    csum = jnp.cumsum(onehot, axis=0)
    rank = jnp.sum(csum * onehot, axis=1) - 1
    cnt = csum[n - 1]
    cnt_p = csum[n_p - 1]
    tiles_b = (cnt + tm - 1) // tm
    tile_end = jnp.cumsum(tiles_b)
    tile_start = tile_end - tiles_b
    slot_of_token = jnp.sum(onehot * tile_start[None, :], axis=1) * tm + rank
    idx = jnp.zeros((nt * tm,), jnp.int32).at[slot_of_token].set(
        jnp.arange(n, dtype=jnp.int32), unique_indices=True).reshape(nt, tm)
    tid = jnp.arange(nt, dtype=jnp.int32)
    used = tid < tile_end[N_BINS - 1]
    tbin = jnp.minimum(jnp.sum((tid[:, None] >= tile_end[None, :]).astype(jnp.int32), axis=1), N_BINS - 1)
    last_bin = jnp.max(jnp.where(cnt > 0, jnp.arange(N_BINS, dtype=jnp.int32), 0))
    tbin = jnp.where(used, tbin, last_bin)
    tile_onehot = (tbin[:, None] == jnp.arange(N_BINS, dtype=jnp.int32)[None, :]).astype(jnp.int32)
    lookup = lambda table: jnp.sum(tile_onehot * table[None, :], axis=1)
    kk = tid - lookup(tile_start)
    nvalid = jnp.where(used, jnp.clip(lookup(cnt) - kk * tm, 0, tm), 0).astype(jnp.int32)
    nprompt = jnp.where(used, jnp.clip(lookup(cnt_p) - kk * tm, 0, nvalid), 0).astype(jnp.int32)
    pairs = [(a, b) for a in range(EXPERTS_PER_GROUP) for b in range(a + 1, EXPERTS_PER_GROUP)]
    bin_group = np.arange(N_BINS) // N_PAIRS
    ea = lookup(jnp.asarray(bin_group * EXPERTS_PER_GROUP + np.array([p[0] for p in pairs] * N_EXPERT_GROUPS),
                            jnp.int32))
    eb = lookup(jnp.asarray(bin_group * EXPERTS_PER_GROUP + np.array([p[1] for p in pairs] * N_EXPERT_GROUPS),
                            jnp.int32))
    idx3 = idx.reshape(nt, 1, tm)

    dff = wg.shape[2]
    wspec_in = lambda sel: pl.BlockSpec((1, d, dff), lambda t, ea, eb, nv, npr: (sel(ea, eb)[t], 0, 0))
    wspec_out = lambda sel: pl.BlockSpec((1, dff, d), lambda t, ea, eb, nv, npr: (sel(ea, eb)[t], 0, 0))
    first = lambda a, b: a
    second = lambda a, b: b
    const2 = lambda a: pl.BlockSpec(a.shape, lambda t, *_: (0,) * a.ndim)
    smem_rows = lambda fn: pl.BlockSpec((1, 1, tm), fn, memory_space=pltpu.SMEM)
    grid_spec = pltpu.PrefetchScalarGridSpec(
        num_scalar_prefetch=4,
        grid=(nt,),
        in_specs=[smem_rows(lambda t, *_: (t, 0, 0)),
                  smem_rows(lambda t, *_: (jnp.minimum(t + 1, nt - 1), 0, 0)),
                  smem_rows(lambda t, *_: (jnp.maximum(t - 1, 0), 0, 0)),
                  wspec_in(first), wspec_in(first), wspec_out(first),
                  wspec_in(second), wspec_in(second), wspec_out(second),
                  const2(g2), const2(b2),
                  pl.BlockSpec(memory_space=pl.ANY)],
        out_specs=[pl.BlockSpec(memory_space=pl.ANY), pl.BlockSpec(memory_space=pl.ANY)],
        scratch_shapes=[pltpu.VMEM((2, tm * pitch, LANES), F32),
                        pltpu.VMEM((2, tm, d), F32),
                        pltpu.VMEM((tm, d), BF16),
                        pltpu.SemaphoreType.DMA((2,)), pltpu.SemaphoreType.DMA((2,))],
    )
    return pl.pallas_call(
        functools.partial(_moe_kernel, alpha=alpha, n_prompt=n_p),
        grid_spec=grid_spec,
        out_shape=[jax.ShapeDtypeStruct((n_p, d), F32), jax.ShapeDtypeStruct((n_s, d), F32)],
        compiler_params=_params("arbitrary"),
        name="moe",
    )(ea, eb, nvalid, nprompt, idx3, idx3, idx3, wg, wu, wd, wg, wu, wd, g2, b2, x1_all)


def _s5_tables(lam_re, lam_im, log_dt, b_re, b_im, c_re, c_im, seg_lens):
    n_groups, n_state = lam_re.shape
    dt = jnp.exp(log_dt.astype(F32))[:, None]
    lam_re = lam_re.astype(F32)
    lam_im = lam_im.astype(F32)
    mag = jnp.exp(lam_re * dt)
    a_re = mag * jnp.cos(lam_im * dt)
    a_im = mag * jnp.sin(lam_im * dt)
    e_re = a_re - 1.0
    e_im = a_im
    lam_sq = lam_re * lam_re + lam_im * lam_im
    coef_re = (e_re * lam_re + e_im * lam_im) / lam_sq
    coef_im = (e_im * lam_re - e_re * lam_im) / lam_sq
    b_re = b_re.astype(F32)
    b_im = b_im.astype(F32)
    bb_re = coef_re[..., None] * b_re - coef_im[..., None] * b_im
    bb_im = coef_re[..., None] * b_im + coef_im[..., None] * b_re

    gpt = LANES // n_state
    n_tiles = n_groups // gpt
    gps = LANES // S5_GROUP
    g_in_slab = (jnp.arange(n_groups) % gps)
    rows_onehot = jax.nn.one_hot(g_in_slab, gps, dtype=F32)
    w_re = jnp.einsum("gs,gpc->gscp", rows_onehot, bb_re).reshape(n_groups, LANES, n_state)
    w_im = jnp.einsum("gs,gpc->gscp", rows_onehot, bb_im).reshape(n_groups, LANES, n_state)
    w_re = w_re.reshape(n_tiles, gpt, LANES, n_state).transpose(0, 2, 1, 3).reshape(n_tiles, LANES, LANES)
    w_im = w_im.reshape(n_tiles, gpt, LANES, n_state).transpose(0, 2, 1, 3).reshape(n_tiles, LANES, LANES)
    wb = jnp.concatenate([w_re, w_im], axis=-1).astype(BF16)

    n_flat = n_groups * n_state
    out_w = 2 * LANES
    g_per_out = out_w // S5_GROUP
    n_out_tiles = n_groups // g_per_out
    oh = jax.nn.one_hot(jnp.arange(n_groups) % g_per_out, g_per_out, dtype=F32)
    wcr = jnp.einsum("gs,gcp->gpsc", oh, c_re.astype(F32)).reshape(n_out_tiles, g_per_out * n_state, out_w)
    wci = jnp.einsum("gs,gcp->gpsc", oh, -c_im.astype(F32)).reshape(n_out_tiles, g_per_out * n_state, out_w)

    def powers(k):
        return ((mag ** k) * jnp.cos(lam_im * dt * k)).reshape(1, n_flat), \
               ((mag ** k) * jnp.sin(lam_im * dt * k)).reshape(1, n_flat)

    return (wb, a_re.reshape(1, n_flat), a_im.reshape(1, n_flat), [powers(float(k)) for k in seg_lens],
            wcr.astype(BF16), wci.astype(BF16))


def _layer(x_p, x_s, st, lp):
    (w_in, b_gates, conv_w, conv_b, norm_w, lam_re, lam_im, log_dt, b_re, b_im, c_re, c_im,
     d_skip, w_glu, b_glu, w_out, ln1_g, ln1_b, w_r1, b_r1, w_r2, b_r2, w_gate, w_up, w_down,
     ln2_g, ln2_b, alpha) = lp
    state_c, state_n, state_m, state_conv, state_re, state_im = st
    bp, tp, d = x_p.shape
    bs, ts, _ = x_s.shape
    dm = norm_w.shape[0]
    head_dim = dm // M_HEADS
    ds5 = d_skip.shape[0]
    n_groups, n_state = lam_re.shape
    n_flat = n_groups * n_state

    o0 = 2 * dm
    o1 = o0 + 2 * dm
    wqk = w_in[:, :o0].astype(BF16)
    wv = w_in[:, o0:o0 + dm].astype(BF16)
    wo = w_in[:, o0 + dm:o1].astype(BF16)
    wg = jnp.pad(w_in[:, o1:o1 + 2 * M_HEADS], ((0, 0), (0, LANES - 2 * M_HEADS))).astype(BF16)
    wu = w_in[:, o1 + 2 * M_HEADS:].astype(BF16)
    bg = jnp.pad(b_gates.astype(F32), (0, LANES - 2 * M_HEADS)).reshape(1, LANES)
    w_out_b = w_out.astype(BF16)
    wr = jnp.concatenate([w_r1, jnp.transpose(w_r2, (1, 0, 2)).reshape(d, -1)], axis=1).astype(F32)
    n_logits = wr.shape[1]
    wr = jnp.pad(wr, ((0, 0), (0, LANES - n_logits)))
    wrh = wr.astype(BF16)
    wrl = (wr - wrh.astype(F32)).astype(BF16)
    br = jnp.pad(jnp.concatenate([b_r1, b_r2.reshape(-1)]).astype(F32), (0, LANES - n_logits))
    br = br.reshape(LANES, 1)

    blk_p = min(tp, PERM_ROWS)
    blk_s = min(ts, PERM_ROWS)
    seg_p = blk_p // N_STREAMS
    seg_s = blk_s // N_STREAMS
    wb, a_re, a_im, (as_p, as_s), wcr, wci = _s5_tables(lam_re, lam_im, log_dt, b_re, b_im, c_re, c_im,
                                                         (seg_p, seg_s))

    outs = []
    shared = None
    row_offset = 0
    for x, blk, seg, a_seg, zero_state in ((x_p, blk_p, seg_p, as_p, True), (x_s, blk_s, seg_s, as_s, False)):
        b, t, _ = x.shape
        x2 = x.reshape(b * t, d)
        perm_np = _stream_perm(blk, seg)
        perm = jnp.asarray(perm_np, BF16)
        permt = jnp.asarray(perm_np[:blk, :blk].T, BF16)
        qk, v, og, u, g = _inproj(x2, perm, wqk, wv, wo, wu, wg)

        if zero_state:
            c0 = jnp.zeros((b, M_HEADS, head_dim, head_dim), F32)
            n0 = jnp.zeros((b, M_HEADS, head_dim), F32)
            m0 = jnp.zeros((b, M_HEADS), F32)
            conv0 = jnp.zeros((b, CONV_W - 1, 2 * dm), F32)
            re0 = jnp.zeros((b, n_groups, n_state), F32)
            im0 = jnp.zeros((b, n_groups, n_state), F32)
        else:
            c0, n0, m0, conv0, re0, im0 = (state_c.astype(F32), state_n.astype(F32), state_m.astype(F32),
                                           state_conv.astype(F32), state_re.astype(F32), state_im.astype(F32))
        n0p = jnp.pad(n0, ((0, 0), (0, SUBLANES - M_HEADS), (0, 0)))
        m0p = jnp.broadcast_to(jnp.pad(m0, ((0, 0), (0, SUBLANES - M_HEADS)))[:, :, None], (b, SUBLANES, LANES))
        tail0 = jnp.pad(conv0, ((0, 0), (SUBLANES - (CONV_W - 1), 0), (0, 0)))
        chunk = min(t, LANES)
        hm, c_new, n_new, m_new = _mlstm(qk, v, og, g, c0, n0p, m0p, tail0, conv_w.astype(F32),
                                         conv_b.astype(F32).reshape(1, -1), bg, norm_w.astype(F32).reshape(1, -1),
                                         batch=b, seq=t, chunk=chunk)
        hs, re_new, im_new = _s5(u, re0.reshape(b, 1, n_flat), im0.reshape(b, 1, n_flat), wb, a_re, a_im,
                                 a_seg[0], a_seg[1], wcr, wci, d_skip.astype(F32).reshape(1, -1),
                                 w_glu.astype(BF16), b_glu.astype(F32).reshape(1, -1), permt,
                                 batch=b, seq=t, block=blk)
        shared = _outproj(x2, hm, hs, w_out_b, ln1_g.astype(F32).reshape(1, -1),
                          ln1_b.astype(F32).reshape(1, -1), wrh, wrl, br, shared, alpha=alpha,
                          n_total=bp * tp + bs * ts, row_offset=row_offset)
        row_offset += b * t
        qk_tail = qk.reshape(b, t, 2 * dm)[:, max(t - (CONV_W - 1), 0):]
        conv_new = qk_tail if t >= CONV_W - 1 else jnp.concatenate([conv0, qk_tail], axis=1)[:, -(CONV_W - 1):]
        states = (c_new, n_new[:, :M_HEADS], m_new[:, :M_HEADS, 0], conv_new,
                  re_new.reshape(b, n_groups, n_state), im_new.reshape(b, n_groups, n_state))
        outs.append(states)

    st_p, st_s = outs
    x1_all, route_all = shared
    yp, ys = _moe(x1_all, route_all, w_gate.astype(BF16), w_up.astype(BF16), w_down.astype(BF16),
                  ln2_g.astype(F32).reshape(1, -1), ln2_b.astype(F32).reshape(1, -1), alpha=alpha,
                  n_prompt=bp * tp)
    return yp.reshape(bp, tp, d), ys.reshape(bs, ts, d), st_p, st_s


def kernel(x_prompt, x_sample, state_mlstm_C, state_mlstm_n, state_mlstm_m, state_conv, state_s5_re, state_s5_im, w_in, b_gates, conv_w, conv_b, mlstm_norm_w, s5_lam_re, s5_lam_im, s5_log_dt, s5_b_re, s5_b_im, s5_c_re, s5_c_im, s5_d, w_glu, b_glu, w_out, ln1_g, ln1_b, w_r1, b_r1, w_r2, b_r2, w_gate, w_up, w_down, ln2_g, ln2_b):
    depth = w_in.shape[0]
    alpha = (2 * depth) ** 0.25
    yp, ys = x_prompt, x_sample
    sts_p, sts_s = [], []
    for l in range(depth):
        lp = (w_in[l], b_gates[l], conv_w[l], conv_b[l], mlstm_norm_w[l], s5_lam_re[l], s5_lam_im[l],
              s5_log_dt[l], s5_b_re[l], s5_b_im[l], s5_c_re[l], s5_c_im[l], s5_d[l], w_glu[l], b_glu[l],
              w_out[l], ln1_g[l], ln1_b[l], w_r1[l], b_r1[l], w_r2[l], b_r2[l], w_gate[l], w_up[l],
              w_down[l], ln2_g[l], ln2_b[l], alpha)
        st = (state_mlstm_C[l], state_mlstm_n[l], state_mlstm_m[l], state_conv[l], state_s5_re[l],
              state_s5_im[l])
        yp, ys, sp, ss = _layer(yp, ys, st, lp)
        sts_p.append(sp)
        sts_s.append(ss)
    stack = lambda sts, i: jnp.stack([s[i] for s in sts])
    return (yp, ys) + tuple(stack(sts_p, i) for i in range(6)) + tuple(stack(sts_s, i) for i in range(6))
```

```python
import functools
import math

import jax
import jax.numpy as jnp
import numpy as np
from jax import lax
from jax.experimental import pallas as pl
from jax.experimental.pallas import tpu as pltpu

F32 = jnp.float32
BF16 = jnp.bfloat16

LANES = 128
SUBLANES = 8
VMEM_LIMIT = 56 * 1024 * 1024

M_HEADS = 4
CONV_W = 4
S5_GROUP = 16
S5_STATE = 64
N_EXPERT_GROUPS = 4
EXPERTS_PER_GROUP = 4
N_PAIRS = 6
N_BINS = N_EXPERT_GROUPS * N_PAIRS
LN_EPS = 1e-5
NEG = -1e30

IN_TILE = 512
PERM_ROWS = 256
OUT_ROWS = 256
MOE_TILE = 256
ISSUE_UNROLL = 8
N_STREAMS = SUBLANES
SCAN_LANES = 512


def _dot(a, b):
    return jnp.dot(a, b, preferred_element_type=F32)


def _dot_nt(a, b):
    return lax.dot_general(a, b, (((1,), (1,)), ((), ())), preferred_element_type=F32)


def _resident(shape):
    nd = len(shape)
    return pl.BlockSpec(shape, lambda *_: (0,) * nd, pipeline_mode=pl.Buffered(1))


def _params(*sem):
    return pltpu.CompilerParams(dimension_semantics=sem, vmem_limit_bytes=VMEM_LIMIT)


def _token_pitch(d):
    return -(-(d // LANES + 1) // SUBLANES) * SUBLANES


def _stream_perm(block, seg):
    n_seg = block // seg
    r = np.arange(block)
    src = (r % n_seg) * seg + r // n_seg
    p = np.zeros((block, block), np.float32)
    p[r, src] = 1.0
    reps = PERM_ROWS // block
    return np.kron(np.eye(reps, dtype=np.float32), p)


def _cumsum_rows(x):
    n = x.shape[0]
    row = lax.broadcasted_iota(jnp.int32, x.shape, 0)
    s = 1
    while s < n:
        x = x + jnp.where(row >= s, pltpu.roll(x, s, 0), 0.0)
        s *= 2
    return x


def _transpose_gate_cols(x):
    sel = (lax.broadcasted_iota(jnp.int32, (SUBLANES, LANES), 0)
           == lax.broadcasted_iota(jnp.int32, (SUBLANES, LANES), 1)).astype(BF16)
    hi = x.astype(BF16)
    r1 = x - hi.astype(F32)
    mid = r1.astype(BF16)
    lo = (r1 - mid.astype(F32)).astype(BF16)
    return _dot_nt(sel, hi) + _dot_nt(sel, mid) + _dot_nt(sel, lo)


def _mixer_in_kernel(x_ref, perm_ref, wqk_ref, wv_ref, wo_ref, wu_ref, wg_ref,
                     c0_ref, n0_ref, m0_ref, tail0_ref, cw_ref, cb_ref, bg_ref, nw_ref,
                     u_ref, h_ref, c_out, n_out, m_out, tail_out,
                     qk_s, v_s, og_s, g_s, ext_ref, c_ref, n_ref, m_ref, *, chunk, head_dim):
    ci = pl.program_id(1)

    @pl.when(ci == 0)
    def _():
        c_ref[...] = c0_ref[0]
        n_ref[...] = n0_ref[0]
        m_ref[...] = m0_ref[0]
        ext_ref[0:SUBLANES, :] = tail0_ref[0]

    tile_rows = x_ref.shape[0]
    group = perm_ref.shape[0]
    for r0 in range(0, tile_rows, group):
        rows = slice(r0, r0 + group)
        xb = x_ref[rows, :].astype(BF16)
        qk_s[rows, :] = _dot(xb, wqk_ref[...])
        v_s[rows, :] = _dot(xb, wv_ref[...]).astype(BF16)
        og_s[rows, :] = _dot(xb, wo_ref[...])
        g_s[rows, :] = _dot(xb, wg_ref[...])
        u_ref[rows, :] = _dot(_dot(perm_ref[...], xb).astype(BF16), wu_ref[...])

    for c0 in range(0, tile_rows, chunk):
        _mlstm_chunk(slice(c0, c0 + chunk), qk_s, v_s, og_s, g_s, cw_ref, cb_ref, bg_ref, nw_ref,
                     h_ref, ext_ref, c_ref, n_ref, m_ref, head_dim=head_dim)

    @pl.when(ci == pl.num_programs(1) - 1)
    def _():
        c_out[0] = c_ref[...]
        n_out[0] = n_ref[...]
        m_out[0] = m_ref[...]
        tail_out[0] = ext_ref[0:SUBLANES, :]


def _mlstm_chunk(rows, qk_ref, v_ref, og_ref, g_ref, cw_ref, cb_ref, bg_ref, nw_ref,
                 h_ref, ext_ref, c_ref, n_ref, m_ref, *, head_dim):
    dm = M_HEADS * head_dim
    L = rows.stop - rows.start

    ext_ref[SUBLANES:SUBLANES + L, :] = qk_ref[rows, :]
    full = ext_ref[...]
    acc = cb_ref[...] + full[SUBLANES:, :] * cw_ref[CONV_W - 1:CONV_W, :]
    for back in range(1, CONV_W):
        tap = cw_ref[CONV_W - 1 - back:CONV_W - back, :]
        acc = acc + pltpu.roll(full, back, 0)[SUBLANES:, :] * tap
    tail = ext_ref[L:L + SUBLANES, :]
    ext_ref[0:SUBLANES, :] = tail
    qk = acc * jax.nn.sigmoid(acc)

    gates = g_ref[rows, :] + bg_ref[...]
    fpre = pltpu.roll(gates, LANES - M_HEADS, 1)
    logf = jnp.minimum(fpre, 0.0) - jnp.log1p(jnp.exp(-jnp.abs(fpre)))
    bcum = _cumsum_rows(logf)
    rrow = _transpose_gate_cols(gates - bcum)
    tri = (lax.broadcasted_iota(jnp.int32, (L, L), 0) >= lax.broadcasted_iota(jnp.int32, (L, L), 1))

    for h in range(M_HEADS):
        lo, hi = h * head_dim, (h + 1) * head_dim
        q = qk[:, lo:hi]
        k = qk[:, dm + lo:dm + hi] * (head_dim ** -0.5)
        v = v_ref[rows, lo:hi]
        qb = q.astype(BF16)
        kb = k.astype(BF16)
        b_col = bcum[:, h:h + 1]
        ig_col = gates[:, h:h + 1]
        m_prev = m_ref[h:h + 1, 0:1]
        c_prev = c_ref[h]
        n_prev = n_ref[h:h + 1, :]

        dmat = jnp.where(tri, b_col + rrow[h:h + 1, :], NEG)
        inter = b_col + m_prev
        m_t = jnp.maximum(inter, jnp.max(dmat, axis=-1, keepdims=True))
        w_intra = jnp.exp(dmat - m_t)
        w_inter = jnp.exp(inter - m_t)
        s = _dot_nt(qb, kb) * w_intra
        num = w_inter * _dot(qb, c_prev.astype(BF16)) + _dot(s.astype(BF16), v)
        den = w_inter * jnp.sum(q * n_prev, axis=-1, keepdims=True) + jnp.sum(s, axis=-1, keepdims=True)
        hh = num / jnp.maximum(jnp.abs(den), jnp.exp(-m_t))

        m_new = m_t[L - 1:L, :]
        b_last = b_col[L - 1:L, :]
        w_s = jnp.exp(b_last - b_col + ig_col - m_new)
        decay = jnp.exp(b_last + m_prev - m_new)
        kw = k * w_s
        c_ref[h] = decay * c_prev + _dot(kw.T.astype(BF16), v)
        n_ref[h:h + 1, :] = decay * n_prev + jnp.sum(kw, axis=0, keepdims=True)
        m_ref[h:h + 1, :] = jnp.broadcast_to(m_new, (1, LANES))

        mu = jnp.mean(hh, axis=-1, keepdims=True)
        hc = hh - mu
        var = jnp.mean(hc * hc, axis=-1, keepdims=True)
        hn = hc * lax.rsqrt(var + LN_EPS) * nw_ref[:, lo:hi]
        h_ref[rows, lo:hi] = (jax.nn.sigmoid(og_ref[rows, lo:hi]) * hn).astype(BF16)


def _mixer_in(x2, perm, wqk, wv, wo, wu, wg, c0, n0p, m0p, tail0, conv_w, conv_b, bg, norm_w,
              *, batch, seq, chunk):
    n, d = x2.shape
    dm = wv.shape[1]
    ds5 = wu.shape[1]
    head_dim = dm // M_HEADS
    tile = min(seq, IN_TILE)
    nt = seq // tile
    rows = lambda w: pl.BlockSpec((tile, w), lambda b, c: (b * nt + c, 0))
    per_b = lambda *s: pl.BlockSpec((1,) + s, lambda b, c: (b,) + (0,) * len(s))
    weights = [perm, wqk, wv, wo, wu, wg]
    small = [conv_w, conv_b, bg, norm_w]
    return pl.pallas_call(
        functools.partial(_mixer_in_kernel, chunk=chunk, head_dim=head_dim),
        grid=(batch, nt),
        in_specs=[rows(d)] + [_resident(w.shape) for w in weights]
                 + [per_b(M_HEADS, head_dim, head_dim), per_b(SUBLANES, head_dim), per_b(SUBLANES, LANES),
                    per_b(SUBLANES, 2 * dm)] + [_resident(a.shape) for a in small],
        out_specs=[rows(ds5), rows(dm), per_b(M_HEADS, head_dim, head_dim), per_b(SUBLANES, head_dim),
                   per_b(SUBLANES, LANES), per_b(SUBLANES, 2 * dm)],
        out_shape=[jax.ShapeDtypeStruct((n, ds5), F32), jax.ShapeDtypeStruct((n, dm), BF16),
                   jax.ShapeDtypeStruct((batch, M_HEADS, head_dim, head_dim), F32),
                   jax.ShapeDtypeStruct((batch, SUBLANES, head_dim), F32),
                   jax.ShapeDtypeStruct((batch, SUBLANES, LANES), F32),
                   jax.ShapeDtypeStruct((batch, SUBLANES, 2 * dm), F32)],
        scratch_shapes=[pltpu.VMEM((tile, 2 * dm), F32), pltpu.VMEM((tile, dm), BF16),
                        pltpu.VMEM((tile, dm), F32), pltpu.VMEM((tile, LANES), F32),
                        pltpu.VMEM((chunk + SUBLANES, 2 * dm), F32),
                        pltpu.VMEM((M_HEADS, head_dim, head_dim), F32),
                        pltpu.VMEM((SUBLANES, head_dim), F32),
                        pltpu.VMEM((SUBLANES, LANES), F32)],
        compiler_params=_params("arbitrary", "arbitrary"),
        name="mixer_in",
    )(x2, *weights, c0, n0p, m0p, tail0, *small)


def _s5_kernel(u_ref, sre0_ref, sim0_ref, wb_ref, are_ref, aim_ref, asre_ref, asim_ref,
               wcr_ref, wci_ref, dskip_ref, wglu_ref, bglu_ref, permt_ref,
               hs_ref, sre_out, sim_out,
               xr_ref, xi_ref, hr_ref, hi_ref, *, seg):
    bi = pl.program_id(1)
    n_state = xr_ref.shape[1]
    n_tiles = wb_ref.shape[0]

    @pl.when(bi == 0)
    def _():
        hr_ref[...] = sre0_ref[0]
        hi_ref[...] = sim0_ref[0]

    ub = u_ref[...].astype(BF16)
    tiles_per_slab = LANES // (2 * S5_GROUP)
    row = lax.broadcasted_iota(jnp.int32, (N_STREAMS, SCAN_LANES), 0)
    for sl in range(n_state // SCAN_LANES):
        cols = slice(sl * SCAN_LANES, (sl + 1) * SCAN_LANES)
        for j in range(sl * (SCAN_LANES // LANES), (sl + 1) * (SCAN_LANES // LANES)):
            cs = (j // tiles_per_slab) * LANES
            xj = _dot(ub[:, cs:cs + LANES], wb_ref[j])
            xr_ref[:, j * LANES:(j + 1) * LANES] = xj[:, :LANES]
            xi_ref[:, j * LANES:(j + 1) * LANES] = xj[:, LANES:]
        ar = jnp.broadcast_to(are_ref[:, cols], (N_STREAMS, SCAN_LANES))
        ai = jnp.broadcast_to(aim_ref[:, cols], (N_STREAMS, SCAN_LANES))

        er = xr_ref[0:N_STREAMS, cols]
        ei = xi_ref[0:N_STREAMS, cols]
        for i in range(1, seg):
            rows = slice(i * N_STREAMS, (i + 1) * N_STREAMS)
            nr = ar * er - ai * ei + xr_ref[rows, cols]
            ni = ar * ei + ai * er + xi_ref[rows, cols]
            xr_ref[rows, cols] = nr
            xi_ref[rows, cols] = ni
            er, ei = nr, ni

        asr = asre_ref[:, cols]
        asi = asim_ref[:, cols]
        cr = hr_ref[:, cols]
        ci = hi_ref[:, cols]
        cmr = jnp.zeros((N_STREAMS, SCAN_LANES), F32)
        cmi = jnp.zeros((N_STREAMS, SCAN_LANES), F32)
        for k in range(N_STREAMS):
            cmr = jnp.where(row == k, cr, cmr)
            cmi = jnp.where(row == k, ci, cmi)
            nr = asr * cr - asi * ci + er[k:k + 1, :]
            ni = asr * ci + asi * cr + ei[k:k + 1, :]
            cr, ci = nr, ni
        hr_ref[:, cols] = cr
        hi_ref[:, cols] = ci

        dr, di = cmr, cmi
        for i in range(seg):
            rows = slice(i * N_STREAMS, (i + 1) * N_STREAMS)
            dr, di = ar * dr - ai * di, ar * di + ai * dr
            xr_ref[rows, cols] += dr
            xi_ref[rows, cols] += di

    n_out_tiles = wcr_ref.shape[0]
    kw = n_state // n_out_tiles
    ys = []
    for qt in range(n_out_tiles):
        hr = xr_ref[:, qt * kw:(qt + 1) * kw].astype(BF16)
        hi = xi_ref[:, qt * kw:(qt + 1) * kw].astype(BF16)
        ys.append(_dot(hr, wcr_ref[qt]) + _dot(hi, wci_ref[qt]))
    y = jnp.concatenate(ys, axis=-1) + dskip_ref[...] * u_ref[...]
    gl = 0.5 * y * (1.0 + jnp.tanh(math.sqrt(2.0 / math.pi) * (y + 0.044715 * (y * y * y))))
    z = _dot(gl.astype(BF16), wglu_ref[...]) + bglu_ref[...]
    out = gl * jax.nn.sigmoid(z)
    hs_ref[...] = _dot(permt_ref[...], out.astype(BF16)).astype(BF16)

    @pl.when(bi == pl.num_programs(1) - 1)
    def _():
        sre_out[0] = hr_ref[...]
        sim_out[0] = hi_ref[...]


def _s5(u, sre0, sim0, wb, a_re, a_im, as_re, as_im, wcr, wci, dskip, wglu, bglu, permt,
        *, batch, seq, block):
    n, ds5 = u.shape
    n_state = a_re.shape[1]
    nb = seq // block
    rows = lambda w: pl.BlockSpec((block, w), lambda b, c: (b * nb + c, 0))
    per_b = pl.BlockSpec((1, 1, n_state), lambda b, c: (b, 0, 0))
    const = lambda a: pl.BlockSpec(a.shape, lambda b, c: (0,) * a.ndim)
    return pl.pallas_call(
        functools.partial(_s5_kernel, seg=block // N_STREAMS),
        grid=(batch, nb),
        in_specs=[rows(ds5), per_b, per_b, const(wb), const(a_re), const(a_im), const(as_re),
                  const(as_im), const(wcr), const(wci), const(dskip), const(wglu), const(bglu),
                  const(permt)],
        out_specs=[rows(ds5), per_b, per_b],
        out_shape=[jax.ShapeDtypeStruct((n, ds5), BF16),
                   jax.ShapeDtypeStruct((batch, 1, n_state), F32),
                   jax.ShapeDtypeStruct((batch, 1, n_state), F32)],
        scratch_shapes=[pltpu.VMEM((block, n_state), F32), pltpu.VMEM((block, n_state), F32),
                        pltpu.VMEM((1, n_state), F32), pltpu.VMEM((1, n_state), F32)],
        compiler_params=_params("arbitrary", "arbitrary"),
        name="s5",
    )(u, sre0, sim0, wb, a_re, a_im, as_re, as_im, wcr, wci, dskip, wglu, bglu, permt)


def _layer_norm(x, g, b):
    mu = jnp.mean(x, axis=-1, keepdims=True)
    xc = x - mu
    var = jnp.mean(xc * xc, axis=-1, keepdims=True)
    return xc * lax.rsqrt(var + LN_EPS) * g + b


def _first_max(cols):
    best = cols[0]
    for c in cols[1:]:
        best = jnp.maximum(best, c)
    flags = []
    taken = None
    for c in cols:
        hit = c == best
        if taken is not None:
            hit = jnp.logical_and(hit, jnp.logical_not(taken))
            taken = jnp.logical_or(taken, hit)
        else:
            taken = hit
        flags.append(hit)
    return best, flags


def _outproj_kernel(*refs, alpha, n_valid_blocks):
    x1_ref, route_ref = refs[-2:]
    i = pl.program_id(0)

    @pl.when(i < n_valid_blocks)
    def _():
        _outproj_tile(*refs, alpha=alpha)

    @pl.when(i >= n_valid_blocks)
    def _():
        x1_ref[...] = jnp.zeros_like(x1_ref)
        route_ref[...] = jnp.zeros_like(route_ref)


def _outproj_tile(x_ref, hm_ref, hs_ref, wo_ref, g1_ref, b1_ref, wrh_ref, wrl_ref, br_ref,
                  *rest, alpha):
    x1_ref, route_ref = rest[-2:]
    starts = range(0, x_ref.shape[0], OUT_ROWS)
    mixes = [_dot(jnp.concatenate([hm_ref[r0:r0 + OUT_ROWS, :], hs_ref[r0:r0 + OUT_ROWS, :]], axis=1), wo_ref[...])
             for r0 in starts]
    for r0, mix in zip(starts, mixes):
        _outproj_rows(r0, mix, x_ref, g1_ref, b1_ref, wrh_ref, wrl_ref, br_ref, x1_ref, route_ref, alpha=alpha)


def _outproj_rows(r0, mix, x_ref, g1_ref, b1_ref, wrh_ref, wrl_ref, br_ref, x1_ref, route_ref, *, alpha):
    tm = OUT_ROWS
    d = x_ref.shape[1]
    pitch = x1_ref.shape[0] // x_ref.shape[0]
    rows = slice(r0, r0 + tm)

    def store_token_tile(j, v):
        for g in range(tm // SUBLANES):
            first = (r0 + g * SUBLANES) * pitch + j
            x1_ref[pl.ds(first, SUBLANES, stride=pitch), :] = v[g * SUBLANES:(g + 1) * SUBLANES, :]

    x1 = _layer_norm(alpha * x_ref[rows, :] + mix, g1_ref[...], b1_ref[...])
    for j in range(d // LANES):
        store_token_tile(j, x1[:, j * LANES:(j + 1) * LANES])

    xh = x1.astype(BF16)
    xl = (x1 - xh.astype(F32)).astype(BF16)
    logits = (_dot(xh, wrh_ref[...]) + _dot(xl, wrh_ref[...]) + _dot(xh, wrl_ref[...])).T + br_ref[...]

    l1 = [logits[g:g + 1, :] for g in range(N_EXPERT_GROUPS)]
    m1, gsel = _first_max(l1)
    denom = l1[0] * 0.0
    for c in l1:
        denom = denom + jnp.exp(c - m1)
    p_g = 1.0 / denom
    gid = l1[0] * 0.0
    l2 = []
    for e in range(EXPERTS_PER_GROUP):
        acc = l1[0] * 0.0
        for g in range(N_EXPERT_GROUPS):
            c0 = N_EXPERT_GROUPS + g * EXPERTS_PER_GROUP + e
            acc = acc + jnp.where(gsel[g], logits[c0:c0 + 1, :], 0.0)
        l2.append(acc)
    for g in range(N_EXPERT_GROUPS):
        gid = gid + jnp.where(gsel[g], float(g), 0.0)

    v1, f1 = _first_max(l2)
    masked = [jnp.where(f1[e], -jnp.inf, l2[e]) for e in range(EXPERTS_PER_GROUP)]
    v2, f2 = _first_max(masked)
    e21 = jnp.exp(v2 - v1)
    w_first = 1.0 / (1.0 + e21)
    w_second = e21 / (1.0 + e21)
    pair_id = l1[0] * 0.0
    gate_a = l1[0] * 0.0
    gate_b = l1[0] * 0.0
    pid = 0
    for a in range(EXPERTS_PER_GROUP):
        for b in range(a + 1, EXPERTS_PER_GROUP):
            ab = jnp.logical_and(f1[a], f2[b])
            ba = jnp.logical_and(f1[b], f2[a])
            pair_id = pair_id + jnp.where(jnp.logical_or(ab, ba), float(pid), 0.0)
            gate_a = gate_a + jnp.where(ab, w_first, 0.0) + jnp.where(ba, w_second, 0.0)
            gate_b = gate_b + jnp.where(ab, w_second, 0.0) + jnp.where(ba, w_first, 0.0)
            pid += 1
    bin_id = gid * float(N_PAIRS) + pair_id
    row = lax.broadcasted_iota(jnp.int32, (LANES, tm), 0)
    route_t = jnp.where(row == 0, bin_id,
                        jnp.where(row == 1, p_g * gate_a, jnp.where(row == 2, p_g * gate_b, 0.0)))
    route_ref[:, rows] = route_t[0:SUBLANES, :]
    store_token_tile(d // LANES, route_t.T)
    for j in range(d // LANES + 1, pitch):
        store_token_tile(j, jnp.zeros((tm, LANES), F32))


def _outproj(x2, hm, hs, wo, g1, b1, wrh, wrl, br_col, shared, *, alpha, n_total, row_offset):
    n, d = x2.shape
    dm = hm.shape[1]
    groups = 2 if all(e % (2 * OUT_ROWS) == 0 for e in (n, row_offset, n_total - row_offset)) else 1
    tile = groups * OUT_ROWS
    blk0 = row_offset // tile
    n_blocks = n // tile
    pitch = _token_pitch(d)
    grid = n_blocks if shared is not None else n_total // tile - blk0
    rows = lambda w: pl.BlockSpec((tile, w), lambda i: (jnp.minimum(i, n_blocks - 1), 0))
    kern = functools.partial(_outproj_kernel, alpha=alpha, n_valid_blocks=n_blocks)
    in_specs = [rows(d), rows(dm), rows(hs.shape[1]), _resident(wo.shape),
                _resident(g1.shape), _resident(b1.shape), _resident(wrh.shape), _resident(wrl.shape),
                _resident(br_col.shape)]
    args = [x2, hm, hs, wo, g1, b1, wrh, wrl, br_col]
    aliases = {}
    if shared is not None:
        in_specs += [pl.BlockSpec(memory_space=pl.ANY), pl.BlockSpec(memory_space=pl.ANY)]
        aliases = {len(args): 0, len(args) + 1: 1}
        args = args + list(shared)
    return pl.pallas_call(
        kern,
        grid=(grid,),
        in_specs=in_specs,
        out_specs=[pl.BlockSpec((tile * pitch, LANES), lambda i: (i + blk0, 0)),
                   pl.BlockSpec((SUBLANES, tile), lambda i: (0, i + blk0))],
        out_shape=[jax.ShapeDtypeStruct((n_total * pitch, LANES), F32),
                   jax.ShapeDtypeStruct((SUBLANES, n_total), F32)],
        input_output_aliases=aliases,
        compiler_params=_params("arbitrary"),
        name="outproj",
    )(*args)


def _moe_kernel(ea_ref, eb_ref, nv_ref, npr_ref,
                idx_ref, idxn_ref, idxp_ref, wga_ref, wua_ref, wda_ref, wgb_ref, wub_ref, wdb_ref,
                g2_ref, b2_ref, x1_hbm,
                yp_hbm, ys_hbm,
                xbuf, obuf, xb_ref, gsem, ssem, *, alpha, n_prompt):
    t = pl.program_id(0)
    nt = pl.num_programs(0)
    slot = t % 2
    tm, d = obuf.shape[1:]
    pitch = xbuf.shape[2]
    n_model_tiles = d // LANES

    def gather_copy(tok, r, s):
        src = x1_hbm.at[pl.ds(pl.multiple_of(tok * pitch, SUBLANES), pitch), :]
        return pltpu.make_async_copy(src, xbuf.at[s, r // SUBLANES, :, r % SUBLANES, :], gsem.at[s])

    def tile_lanes(j):
        return xbuf[slot, :, j].reshape(tm, LANES)

    def scatter_copy(tok, r, s, to_prompt):
        dst = yp_hbm.at[pl.ds(tok, 1)] if to_prompt else ys_hbm.at[pl.ds(tok - n_prompt, 1)]
        return pltpu.make_async_copy(obuf.at[s, pl.ds(r, 1)], dst, ssem.at[s])

    def for_rows(lo, hi, fn):
        n_groups = (hi - lo) // ISSUE_UNROLL

        def group(gi, c):
            base = lo + gi * ISSUE_UNROLL
            for j in range(ISSUE_UNROLL):
                fn(base + j)
            return c

        def single(r, c):
            fn(r)
            return c

        lax.fori_loop(0, n_groups, group, 0)
        lax.fori_loop(lo + n_groups * ISSUE_UNROLL, hi, single, 0)

    def scatter_tile(ids, s, n_prompt_rows, n_rows):
        for_rows(0, n_prompt_rows, lambda r: scatter_copy(ids[0, 0, r], r, s, True).start())
        for_rows(n_prompt_rows, n_rows, lambda r: scatter_copy(ids[0, 0, r], r, s, False).start())

    def wait_rows(src, dst, sem, count):
        bit = tm
        while bit >= 1:
            @pl.when((count & bit) != 0)
            def _(bit=bit):
                pltpu.make_async_copy(src.at[pl.ds(0, bit)], dst.at[pl.ds(0, bit)], sem).wait()
            bit //= 2

    n_rows = nv_ref[t]
    mixed = npr_ref[t] != n_rows
    prev = jnp.maximum(t - 1, 0)
    n_prev = jnp.where(jnp.logical_and(t >= 1, npr_ref[prev] == nv_ref[prev]), nv_ref[prev], 0)

    def wait_gather(s):
        pltpu.make_async_copy(xbuf.at[1 - s], xbuf.at[s], gsem.at[s]).wait()

    @pl.when(t == 0)
    def _():
        for_rows(0, tm, lambda r: gather_copy(idx_ref[0, 0, r], r, 0).start())

    @pl.when(jnp.logical_or(t == 0, nv_ref[prev] > 0))
    def _():
        wait_gather(slot)

    @pl.when(t >= 2)
    def _():
        wait_rows(obuf.at[slot], yp_hbm, ssem.at[slot], nv_ref[t - 2])

    @pl.when(n_rows > 0)
    def _():
        for j in range(n_model_tiles):
            xb_ref[:, j * LANES:(j + 1) * LANES] = tile_lanes(j).astype(BF16)
        n_stages = 8

        def issue_neighbours(stage):
            for r in range(stage * tm // n_stages, (stage + 1) * tm // n_stages):
                gather_copy(idxn_ref[0, 0, r], r, 1 - slot).start(priority=r % 2)

                @pl.when(r < n_prev)
                def _(r=r):
                    scatter_copy(idxp_ref[0, 0, r], r, 1 - slot, True).start(priority=r % 2)

        routing = tile_lanes(n_model_tiles)
        gate_a = routing[:, 1:2]
        gate_b = routing[:, 2:3]
        xb = xb_ref[...]
        issue_neighbours(0)
        ha = _dot(xb, wga_ref[0])
        issue_neighbours(1)
        ha = (ha * jax.nn.sigmoid(ha)) * _dot(xb, wua_ref[0])
        issue_neighbours(2)
        ya = _dot(ha.astype(BF16), wda_ref[0])
        issue_neighbours(3)
        hb = _dot(xb, wgb_ref[0])
        issue_neighbours(4)
        hb = (hb * jax.nn.sigmoid(hb)) * _dot(xb, wub_ref[0])
        issue_neighbours(5)
        yb = _dot(hb.astype(BF16), wdb_ref[0])
        issue_neighbours(6)
        moe = gate_a * ya + gate_b * yb
        issue_neighbours(7)
        x1 = jnp.concatenate([tile_lanes(j) for j in range(n_model_tiles)], axis=1)
        obuf[slot] = _layer_norm(alpha * x1 + moe, g2_ref[...], b2_ref[...])

    @pl.when(n_rows == 0)
    def _():
        scatter_tile(idxp_ref, 1 - slot, n_prev, n_prev)

    now = jnp.logical_or(mixed, t == nt - 1)
    scatter_tile(idx_ref, slot, jnp.where(now, npr_ref[t], 0), jnp.where(now, n_rows, 0))

    @pl.when(t == nt - 1)
    def _():
        @pl.when(n_rows > 0)
        def _():
            wait_gather(1 - slot)

        wait_rows(obuf.at[slot], yp_hbm, ssem.at[slot], nv_ref[t])

        @pl.when(t >= 1)
        def _():
            wait_rows(obuf.at[1 - slot], yp_hbm, ssem.at[1 - slot], nv_ref[t - 1])


def _moe(x1_all, route_all, wg, wu, wd, g2, b2, *, alpha, n_prompt):
    n = route_all.shape[1]
    d = wg.shape[1]
    pitch = x1_all.shape[0] // n
    n_p = n_prompt
    n_s = n - n_p
    tm = MOE_TILE
    nt = n // tm + N_BINS if n % tm == 0 else (n + N_BINS * (tm - 1)) // tm + 1
    bins = route_all[0].astype(jnp.int32)

    onehot = (bins[:, None] == jnp.arange(N_BINS, dtype=jnp.int32)[None, :]).astype(jnp.int32)
    csum = jnp.cumsum(onehot, axis=0)
    rank = jnp.sum(csum * onehot, axis=1) - 1
    cnt = csum[n - 1]
    cnt_p = csum[n_p - 1]
    tiles_b = (cnt + tm - 1) // tm
    tile_end = jnp.cumsum(tiles_b)
    tile_start = tile_end - tiles_b
    slot_of_token = jnp.sum(onehot * tile_start[None, :], axis=1) * tm + rank
    idx = jnp.zeros((nt * tm,), jnp.int32).at[slot_of_token].set(
        jnp.arange(n, dtype=jnp.int32), unique_indices=True).reshape(nt, tm)
    tid = jnp.arange(nt, dtype=jnp.int32)
    used = tid < tile_end[N_BINS - 1]
    tbin = jnp.minimum(jnp.sum((tid[:, None] >= tile_end[None, :]).astype(jnp.int32), axis=1), N_BINS - 1)
    last_bin = jnp.max(jnp.where(cnt > 0, jnp.arange(N_BINS, dtype=jnp.int32), 0))
    tbin = jnp.where(used, tbin, last_bin)
    tile_onehot = (tbin[:, None] == jnp.arange(N_BINS, dtype=jnp.int32)[None, :]).astype(jnp.int32)
    lookup = lambda table: jnp.sum(tile_onehot * table[None, :], axis=1)
    kk = tid - lookup(tile_start)
    nvalid = jnp.where(used, jnp.clip(lookup(cnt) - kk * tm, 0, tm), 0).astype(jnp.int32)
    nprompt = jnp.where(used, jnp.clip(lookup(cnt_p) - kk * tm, 0, nvalid), 0).astype(jnp.int32)
    pairs = [(a, b) for a in range(EXPERTS_PER_GROUP) for b in range(a + 1, EXPERTS_PER_GROUP)]
    bin_group = np.arange(N_BINS) // N_PAIRS
    ea = lookup(jnp.asarray(bin_group * EXPERTS_PER_GROUP + np.array([p[0] for p in pairs] * N_EXPERT_GROUPS),
                            jnp.int32))
    eb = lookup(jnp.asarray(bin_group * EXPERTS_PER_GROUP + np.array([p[1] for p in pairs] * N_EXPERT_GROUPS),
                            jnp.int32))
    idx3 = idx.reshape(nt, 1, tm)

    dff = wg.shape[2]
    wspec_in = lambda sel: pl.BlockSpec((1, d, dff), lambda t, ea, eb, nv, npr: (sel(ea, eb)[t], 0, 0))
    wspec_out = lambda sel: pl.BlockSpec((1, dff, d), lambda t, ea, eb, nv, npr: (sel(ea, eb)[t], 0, 0))
    first = lambda a, b: a
    second = lambda a, b: b
    const2 = lambda a: pl.BlockSpec(a.shape, lambda t, *_: (0,) * a.ndim)
    smem_rows = lambda fn: pl.BlockSpec((1, 1, tm), fn, memory_space=pltpu.SMEM)
    grid_spec = pltpu.PrefetchScalarGridSpec(
        num_scalar_prefetch=4,
        grid=(nt,),
        in_specs=[smem_rows(lambda t, *_: (t, 0, 0)),
                  smem_rows(lambda t, *_: (jnp.minimum(t + 1, nt - 1), 0, 0)),
                  smem_rows(lambda t, *_: (jnp.maximum(t - 1, 0), 0, 0)),
                  wspec_in(first), wspec_in(first), wspec_out(first),
                  wspec_in(second), wspec_in(second), wspec_out(second),
                  const2(g2), const2(b2),
                  pl.BlockSpec(memory_space=pl.ANY)],
        out_specs=[pl.BlockSpec(memory_space=pl.ANY), pl.BlockSpec(memory_space=pl.ANY)],
        scratch_shapes=[pltpu.VMEM((2, tm // SUBLANES, pitch, SUBLANES, LANES), F32),
                        pltpu.VMEM((2, tm, d), F32),
                        pltpu.VMEM((tm, d), BF16),
                        pltpu.SemaphoreType.DMA((2,)), pltpu.SemaphoreType.DMA((2,))],
    )
    return pl.pallas_call(
        functools.partial(_moe_kernel, alpha=alpha, n_prompt=n_p),
        grid_spec=grid_spec,
        out_shape=[jax.ShapeDtypeStruct((n_p, d), F32), jax.ShapeDtypeStruct((n_s, d), F32)],
        compiler_params=_params("arbitrary"),
        name="moe",
    )(ea, eb, nvalid, nprompt, idx3, idx3, idx3, wg, wu, wd, wg, wu, wd, g2, b2, x1_all)


def _s5_tables(lam_re, lam_im, log_dt, b_re, b_im, c_re, c_im, seg_lens):
    n_groups, n_state = lam_re.shape
    dt = jnp.exp(log_dt.astype(F32))[:, None]
    lam_re = lam_re.astype(F32)
    lam_im = lam_im.astype(F32)
    mag = jnp.exp(lam_re * dt)
    a_re = mag * jnp.cos(lam_im * dt)
    a_im = mag * jnp.sin(lam_im * dt)
    e_re = a_re - 1.0
    e_im = a_im
    lam_sq = lam_re * lam_re + lam_im * lam_im
    coef_re = (e_re * lam_re + e_im * lam_im) / lam_sq
    coef_im = (e_im * lam_re - e_re * lam_im) / lam_sq
    b_re = b_re.astype(F32)
    b_im = b_im.astype(F32)
    bb_re = coef_re[..., None] * b_re - coef_im[..., None] * b_im
    bb_im = coef_re[..., None] * b_im + coef_im[..., None] * b_re

    gpt = LANES // n_state
    n_tiles = n_groups // gpt
    gps = LANES // S5_GROUP
    g_in_slab = (jnp.arange(n_groups) % gps)
    rows_onehot = jax.nn.one_hot(g_in_slab, gps, dtype=F32)
    w_re = jnp.einsum("gs,gpc->gscp", rows_onehot, bb_re).reshape(n_groups, LANES, n_state)
    w_im = jnp.einsum("gs,gpc->gscp", rows_onehot, bb_im).reshape(n_groups, LANES, n_state)
    w_re = w_re.reshape(n_tiles, gpt, LANES, n_state).transpose(0, 2, 1, 3).reshape(n_tiles, LANES, LANES)
    w_im = w_im.reshape(n_tiles, gpt, LANES, n_state).transpose(0, 2, 1, 3).reshape(n_tiles, LANES, LANES)
    wb = jnp.concatenate([w_re, w_im], axis=-1).astype(BF16)

    n_flat = n_groups * n_state
    out_w = 2 * LANES
    g_per_out = out_w // S5_GROUP
    n_out_tiles = n_groups // g_per_out
    oh = jax.nn.one_hot(jnp.arange(n_groups) % g_per_out, g_per_out, dtype=F32)
    wcr = jnp.einsum("gs,gcp->gpsc", oh, c_re.astype(F32)).reshape(n_out_tiles, g_per_out * n_state, out_w)
    wci = jnp.einsum("gs,gcp->gpsc", oh, -c_im.astype(F32)).reshape(n_out_tiles, g_per_out * n_state, out_w)

    def powers(k):
        return ((mag ** k) * jnp.cos(lam_im * dt * k)).reshape(1, n_flat), \
               ((mag ** k) * jnp.sin(lam_im * dt * k)).reshape(1, n_flat)

    return (wb, a_re.reshape(1, n_flat), a_im.reshape(1, n_flat), [powers(float(k)) for k in seg_lens],
            wcr.astype(BF16), wci.astype(BF16))


def _layer(x_p, x_s, st, lp):
    (w_in, b_gates, conv_w, conv_b, norm_w, lam_re, lam_im, log_dt, b_re, b_im, c_re, c_im,
     d_skip, w_glu, b_glu, w_out, ln1_g, ln1_b, w_r1, b_r1, w_r2, b_r2, w_gate, w_up, w_down,
     ln2_g, ln2_b, alpha) = lp
    state_c, state_n, state_m, state_conv, state_re, state_im = st
    bp, tp, d = x_p.shape
    bs, ts, _ = x_s.shape
    dm = norm_w.shape[0]
    head_dim = dm // M_HEADS
    ds5 = d_skip.shape[0]
    n_groups, n_state = lam_re.shape
    n_flat = n_groups * n_state

    o0 = 2 * dm
    o1 = o0 + 2 * dm
    wqk = w_in[:, :o0].astype(BF16)
    wv = w_in[:, o0:o0 + dm].astype(BF16)
    wo = w_in[:, o0 + dm:o1].astype(BF16)
    wg = jnp.pad(w_in[:, o1:o1 + 2 * M_HEADS], ((0, 0), (0, LANES - 2 * M_HEADS))).astype(BF16)
    wu = w_in[:, o1 + 2 * M_HEADS:].astype(BF16)
    bg = jnp.pad(b_gates.astype(F32), (0, LANES - 2 * M_HEADS)).reshape(1, LANES)
    w_out_b = w_out.astype(BF16)
    wr = jnp.concatenate([w_r1, jnp.transpose(w_r2, (1, 0, 2)).reshape(d, -1)], axis=1).astype(F32)
    n_logits = wr.shape[1]
    wr = jnp.pad(wr, ((0, 0), (0, LANES - n_logits)))
    wrh = wr.astype(BF16)
    wrl = (wr - wrh.astype(F32)).astype(BF16)
    br = jnp.pad(jnp.concatenate([b_r1, b_r2.reshape(-1)]).astype(F32), (0, LANES - n_logits))
    br = br.reshape(LANES, 1)

    blk_p = min(tp, PERM_ROWS)
    blk_s = min(ts, PERM_ROWS)
    seg_p = blk_p // N_STREAMS
    seg_s = blk_s // N_STREAMS
    wb, a_re, a_im, (as_p, as_s), wcr, wci = _s5_tables(lam_re, lam_im, log_dt, b_re, b_im, c_re, c_im,
                                                         (seg_p, seg_s))

    outs = []
    shared = None
    row_offset = 0
    for x, blk, seg, a_seg, zero_state in ((x_p, blk_p, seg_p, as_p, True), (x_s, blk_s, seg_s, as_s, False)):
        b, t, _ = x.shape
        x2 = x.reshape(b * t, d)
        perm_np = _stream_perm(blk, seg)[:blk, :blk]
        perm = jnp.asarray(perm_np, BF16)
        permt = jnp.asarray(perm_np.T, BF16)

        if zero_state:
            c0 = jnp.zeros((b, M_HEADS, head_dim, head_dim), F32)
            n0 = jnp.zeros((b, M_HEADS, head_dim), F32)
            m0 = jnp.zeros((b, M_HEADS), F32)
            conv0 = jnp.zeros((b, CONV_W - 1, 2 * dm), F32)
            re0 = jnp.zeros((b, n_groups, n_state), F32)
            im0 = jnp.zeros((b, n_groups, n_state), F32)
        else:
            c0, n0, m0, conv0, re0, im0 = (state_c.astype(F32), state_n.astype(F32), state_m.astype(F32),
                                           state_conv.astype(F32), state_re.astype(F32), state_im.astype(F32))
        n0p = jnp.pad(n0, ((0, 0), (0, SUBLANES - M_HEADS), (0, 0)))
        m0p = jnp.broadcast_to(jnp.pad(m0, ((0, 0), (0, SUBLANES - M_HEADS)))[:, :, None], (b, SUBLANES, LANES))
        tail0 = jnp.pad(conv0, ((0, 0), (SUBLANES - (CONV_W - 1), 0), (0, 0)))
        chunk = min(t, LANES)
        u, hm, c_new, n_new, m_new, qk_tail = _mixer_in(
            x2, perm, wqk, wv, wo, wu, wg, c0, n0p, m0p, tail0, conv_w.astype(F32),
            conv_b.astype(F32).reshape(1, -1), bg, norm_w.astype(F32).reshape(1, -1), batch=b, seq=t, chunk=chunk)
        hs, re_new, im_new = _s5(u, re0.reshape(b, 1, n_flat), im0.reshape(b, 1, n_flat), wb, a_re, a_im,
                                 a_seg[0], a_seg[1], wcr, wci, d_skip.astype(F32).reshape(1, -1),
                                 w_glu.astype(BF16), b_glu.astype(F32).reshape(1, -1), permt,
                                 batch=b, seq=t, block=blk)
        shared = _outproj(x2, hm, hs, w_out_b, ln1_g.astype(F32).reshape(1, -1),
                          ln1_b.astype(F32).reshape(1, -1), wrh, wrl, br, shared, alpha=alpha,
                          n_total=bp * tp + bs * ts, row_offset=row_offset)
        row_offset += b * t
        conv_new = qk_tail[:, SUBLANES - (CONV_W - 1):]
        states = (c_new, n_new[:, :M_HEADS], m_new[:, :M_HEADS, 0], conv_new,
                  re_new.reshape(b, n_groups, n_state), im_new.reshape(b, n_groups, n_state))
        outs.append(states)

    st_p, st_s = outs
    x1_all, route_all = shared
    yp, ys = _moe(x1_all, route_all, w_gate.astype(BF16), w_up.astype(BF16), w_down.astype(BF16),
                  ln2_g.astype(F32).reshape(1, -1), ln2_b.astype(F32).reshape(1, -1), alpha=alpha,
                  n_prompt=bp * tp)
    return yp.reshape(bp, tp, d), ys.reshape(bs, ts, d), st_p, st_s


def kernel(x_prompt, x_sample, state_mlstm_C, state_mlstm_n, state_mlstm_m, state_conv, state_s5_re, state_s5_im, w_in, b_gates, conv_w, conv_b, mlstm_norm_w, s5_lam_re, s5_lam_im, s5_log_dt, s5_b_re, s5_b_im, s5_c_re, s5_c_im, s5_d, w_glu, b_glu, w_out, ln1_g, ln1_b, w_r1, b_r1, w_r2, b_r2, w_gate, w_up, w_down, ln2_g, ln2_b):
    depth = w_in.shape[0]
    alpha = (2 * depth) ** 0.25
    yp, ys = x_prompt, x_sample
    sts_p, sts_s = [], []
    for l in range(depth):
        lp = (w_in[l], b_gates[l], conv_w[l], conv_b[l], mlstm_norm_w[l], s5_lam_re[l], s5_lam_im[l],
              s5_log_dt[l], s5_b_re[l], s5_b_im[l], s5_c_re[l], s5_c_im[l], s5_d[l], w_glu[l], b_glu[l],
              w_out[l], ln1_g[l], ln1_b[l], w_r1[l], b_r1[l], w_r2[l], b_r2[l], w_gate[l], w_up[l],
              w_down[l], ln2_g[l], ln2_b[l], alpha)
        st = (state_mlstm_C[l], state_mlstm_n[l], state_mlstm_m[l], state_conv[l], state_s5_re[l],
              state_s5_im[l])
        yp, ys, sp, ss = _layer(yp, ys, st, lp)
        sts_p.append(sp)
        sts_s.append(ss)
    stack = lambda sts, i: jnp.stack([s[i] for s in sts])
    return (yp, ys) + tuple(stack(sts_p, i) for i in range(6)) + tuple(stack(sts_s, i) for i in range(6))
```

```python
import functools
import math

import jax
import jax.numpy as jnp
import numpy as np
from jax import lax
from jax.experimental import pallas as pl
from jax.experimental.pallas import tpu as pltpu

F32 = jnp.float32
BF16 = jnp.bfloat16

LANES = 128
SUBLANES = 8
VMEM_LIMIT = 56 * 1024 * 1024

M_HEADS = 4
CONV_W = 4
S5_GROUP = 16
S5_STATE = 64
N_EXPERT_GROUPS = 4
EXPERTS_PER_GROUP = 4
N_PAIRS = 6
N_BINS = N_EXPERT_GROUPS * N_PAIRS
LN_EPS = 1e-5
NEG = -1e30

IN_TILE = 512
PERM_ROWS = 256
OUT_ROWS = 256
MOE_TILE = 256
ISSUE_UNROLL = 8
N_STREAMS = SUBLANES
SCAN_LANES = 512


def _dot(a, b):
    return jnp.dot(a, b, preferred_element_type=F32)


def _dot_nt(a, b):
    return lax.dot_general(a, b, (((1,), (1,)), ((), ())), preferred_element_type=F32)


def _resident(shape):
    nd = len(shape)
    return pl.BlockSpec(shape, lambda *_: (0,) * nd, pipeline_mode=pl.Buffered(1))


def _params(*sem):
    return pltpu.CompilerParams(dimension_semantics=sem, vmem_limit_bytes=VMEM_LIMIT)


def _token_pitch(d):
    return -(-(d // LANES + 1) // SUBLANES) * SUBLANES


def _stream_perm(block, seg):
    n_seg = block // seg
    r = np.arange(block)
    src = (r % n_seg) * seg + r // n_seg
    p = np.zeros((block, block), np.float32)
    p[r, src] = 1.0
    reps = PERM_ROWS // block
    return np.kron(np.eye(reps, dtype=np.float32), p)


def _cumsum_rows(x):
    n = x.shape[0]
    row = lax.broadcasted_iota(jnp.int32, x.shape, 0)
    s = 1
    while s < n:
        x = x + jnp.where(row >= s, pltpu.roll(x, s, 0), 0.0)
        s *= 2
    return x


def _transpose_gate_cols(x):
    sel = (lax.broadcasted_iota(jnp.int32, (SUBLANES, LANES), 0)
           == lax.broadcasted_iota(jnp.int32, (SUBLANES, LANES), 1)).astype(BF16)
    hi = x.astype(BF16)
    r1 = x - hi.astype(F32)
    mid = r1.astype(BF16)
    lo = (r1 - mid.astype(F32)).astype(BF16)
    return _dot_nt(sel, hi) + _dot_nt(sel, mid) + _dot_nt(sel, lo)


def _mixer_in_kernel(x_ref, perm_ref, wqk_ref, wv_ref, wo_ref, wu_ref, wg_ref,
                     c0_ref, n0_ref, m0_ref, tail0_ref, cw_ref, cb_ref, bg_ref, nw_ref,
                     u_ref, h_ref, c_out, n_out, m_out, tail_out,
                     qk_s, v_s, og_s, g_s, ext_ref, c_ref, n_ref, m_ref, *, chunk, head_dim):
    ci = pl.program_id(1)

    @pl.when(ci == 0)
    def _():
        c_ref[...] = c0_ref[0]
        n_ref[...] = n0_ref[0]
        m_ref[...] = m0_ref[0]
        ext_ref[0:SUBLANES, :] = tail0_ref[0]

    tile_rows = x_ref.shape[0]
    group = perm_ref.shape[0]
    for r0 in range(0, tile_rows, group):
        rows = slice(r0, r0 + group)
        xb = x_ref[rows, :].astype(BF16)
        qk_s[rows, :] = _dot(xb, wqk_ref[...])
        v_s[rows, :] = _dot(xb, wv_ref[...]).astype(BF16)
        og_s[rows, :] = _dot(xb, wo_ref[...])
        g_s[rows, :] = _dot(xb, wg_ref[...])
        u_ref[rows, :] = _dot(_dot(perm_ref[...], xb).astype(BF16), wu_ref[...])

    for c0 in range(0, tile_rows, chunk):
        _mlstm_chunk(slice(c0, c0 + chunk), qk_s, v_s, og_s, g_s, cw_ref, cb_ref, bg_ref, nw_ref,
                     h_ref, ext_ref, c_ref, n_ref, m_ref, head_dim=head_dim)

    @pl.when(ci == pl.num_programs(1) - 1)
    def _():
        c_out[0] = c_ref[...]
        n_out[0] = n_ref[...]
        m_out[0] = m_ref[...]
        tail_out[0] = ext_ref[0:SUBLANES, :]


def _mlstm_chunk(rows, qk_ref, v_ref, og_ref, g_ref, cw_ref, cb_ref, bg_ref, nw_ref,
                 h_ref, ext_ref, c_ref, n_ref, m_ref, *, head_dim):
    dm = M_HEADS * head_dim
    L = rows.stop - rows.start

    ext_ref[SUBLANES:SUBLANES + L, :] = qk_ref[rows, :]
    full = ext_ref[...]
    acc = cb_ref[...] + full[SUBLANES:, :] * cw_ref[CONV_W - 1:CONV_W, :]
    for back in range(1, CONV_W):
        tap = cw_ref[CONV_W - 1 - back:CONV_W - back, :]
        acc = acc + pltpu.roll(full, back, 0)[SUBLANES:, :] * tap
    tail = ext_ref[L:L + SUBLANES, :]
    ext_ref[0:SUBLANES, :] = tail
    qk = acc * jax.nn.sigmoid(acc)

    gates = g_ref[rows, :] + bg_ref[...]
    fpre = pltpu.roll(gates, LANES - M_HEADS, 1)
    logf = jnp.minimum(fpre, 0.0) - jnp.log1p(jnp.exp(-jnp.abs(fpre)))
    bcum = _cumsum_rows(logf)
    rrow = _transpose_gate_cols(gates - bcum)
    tri = (lax.broadcasted_iota(jnp.int32, (L, L), 0) >= lax.broadcasted_iota(jnp.int32, (L, L), 1))

    for h in range(M_HEADS):
        lo, hi = h * head_dim, (h + 1) * head_dim
        q = qk[:, lo:hi]
        k = qk[:, dm + lo:dm + hi] * (head_dim ** -0.5)
        v = v_ref[rows, lo:hi]
        qb = q.astype(BF16)
        kb = k.astype(BF16)
        b_col = bcum[:, h:h + 1]
        ig_col = gates[:, h:h + 1]
        m_prev = m_ref[h:h + 1, 0:1]
        c_prev = c_ref[h]
        n_prev = n_ref[h:h + 1, :]

        dmat = jnp.where(tri, b_col + rrow[h:h + 1, :], NEG)
        inter = b_col + m_prev
        m_t = jnp.maximum(inter, jnp.max(dmat, axis=-1, keepdims=True))
        w_intra = jnp.exp(dmat - m_t)
        w_inter = jnp.exp(inter - m_t)
        s = _dot_nt(qb, kb) * w_intra
        num = w_inter * _dot(qb, c_prev.astype(BF16)) + _dot(s.astype(BF16), v)
        den = w_inter * jnp.sum(q * n_prev, axis=-1, keepdims=True) + jnp.sum(s, axis=-1, keepdims=True)
        hh = num / jnp.maximum(jnp.abs(den), jnp.exp(-m_t))

        m_new = m_t[L - 1:L, :]
        b_last = b_col[L - 1:L, :]
        w_s = jnp.exp(b_last - b_col + ig_col - m_new)
        decay = jnp.exp(b_last + m_prev - m_new)
        kw = k * w_s
        c_ref[h] = decay * c_prev + _dot(kw.T.astype(BF16), v)
        n_ref[h:h + 1, :] = decay * n_prev + jnp.sum(kw, axis=0, keepdims=True)
        m_ref[h:h + 1, :] = jnp.broadcast_to(m_new, (1, LANES))

        mu = jnp.mean(hh, axis=-1, keepdims=True)
        hc = hh - mu
        var = jnp.mean(hc * hc, axis=-1, keepdims=True)
        hn = hc * lax.rsqrt(var + LN_EPS) * nw_ref[:, lo:hi]
        h_ref[rows, lo:hi] = (jax.nn.sigmoid(og_ref[rows, lo:hi]) * hn).astype(BF16)


def _mixer_in(x2, perm, wqk, wv, wo, wu, wg, c0, n0p, m0p, tail0, conv_w, conv_b, bg, norm_w,
              *, batch, seq, chunk):
    n, d = x2.shape
    dm = wv.shape[1]
    ds5 = wu.shape[1]
    head_dim = dm // M_HEADS
    tile = min(seq, IN_TILE)
    nt = seq // tile
    rows = lambda w: pl.BlockSpec((tile, w), lambda b, c: (b * nt + c, 0))
    per_b = lambda *s: pl.BlockSpec((1,) + s, lambda b, c: (b,) + (0,) * len(s))
    weights = [perm, wqk, wv, wo, wu, wg]
    small = [conv_w, conv_b, bg, norm_w]
    return pl.pallas_call(
        functools.partial(_mixer_in_kernel, chunk=chunk, head_dim=head_dim),
        grid=(batch, nt),
        in_specs=[rows(d)] + [_resident(w.shape) for w in weights]
                 + [per_b(M_HEADS, head_dim, head_dim), per_b(SUBLANES, head_dim), per_b(SUBLANES, LANES),
                    per_b(SUBLANES, 2 * dm)] + [_resident(a.shape) for a in small],
        out_specs=[rows(ds5), rows(dm), per_b(M_HEADS, head_dim, head_dim), per_b(SUBLANES, head_dim),
                   per_b(SUBLANES, LANES), per_b(SUBLANES, 2 * dm)],
        out_shape=[jax.ShapeDtypeStruct((n, ds5), F32), jax.ShapeDtypeStruct((n, dm), BF16),
                   jax.ShapeDtypeStruct((batch, M_HEADS, head_dim, head_dim), F32),
                   jax.ShapeDtypeStruct((batch, SUBLANES, head_dim), F32),
                   jax.ShapeDtypeStruct((batch, SUBLANES, LANES), F32),
                   jax.ShapeDtypeStruct((batch, SUBLANES, 2 * dm), F32)],
        scratch_shapes=[pltpu.VMEM((tile, 2 * dm), F32), pltpu.VMEM((tile, dm), BF16),
                        pltpu.VMEM((tile, dm), F32), pltpu.VMEM((tile, LANES), F32),
                        pltpu.VMEM((chunk + SUBLANES, 2 * dm), F32),
                        pltpu.VMEM((M_HEADS, head_dim, head_dim), F32),
                        pltpu.VMEM((SUBLANES, head_dim), F32),
                        pltpu.VMEM((SUBLANES, LANES), F32)],
        compiler_params=_params("arbitrary", "arbitrary"),
        name="mixer_in",
    )(x2, *weights, c0, n0p, m0p, tail0, *small)


def _s5_kernel(u_ref, sre0_ref, sim0_ref, wb_ref, are_ref, aim_ref, asre_ref, asim_ref,
               wcr_ref, wci_ref, dskip_ref, wglu_ref, bglu_ref, permt_ref,
               hs_ref, sre_out, sim_out,
               xr_ref, xi_ref, hr_ref, hi_ref, *, seg):
    bi = pl.program_id(1)

    @pl.when(bi == 0)
    def _():
        hr_ref[...] = sre0_ref[0]
        hi_ref[...] = sim0_ref[0]

    _s5_scan(u_ref, wb_ref, are_ref, aim_ref, asre_ref, asim_ref, xr_ref, xi_ref, hr_ref, hi_ref, seg=seg)
    hs_ref[...] = _s5_output(u_ref, wcr_ref, wci_ref, dskip_ref, wglu_ref, bglu_ref, permt_ref, xr_ref, xi_ref)

    @pl.when(bi == pl.num_programs(1) - 1)
    def _():
        sre_out[0] = hr_ref[...]
        sim_out[0] = hi_ref[...]


def _s5_scan(u_ref, wb_ref, are_ref, aim_ref, asre_ref, asim_ref, xr_ref, xi_ref, hr_ref, hi_ref, *, seg):
    n_state = xr_ref.shape[1]

    ub = u_ref[...].astype(BF16)
    tiles_per_slab = LANES // (2 * S5_GROUP)
    row = lax.broadcasted_iota(jnp.int32, (N_STREAMS, SCAN_LANES), 0)
    for sl in range(n_state // SCAN_LANES):
        cols = slice(sl * SCAN_LANES, (sl + 1) * SCAN_LANES)
        for j in range(sl * (SCAN_LANES // LANES), (sl + 1) * (SCAN_LANES // LANES)):
            cs = (j // tiles_per_slab) * LANES
            xj = _dot(ub[:, cs:cs + LANES], wb_ref[j])
            xr_ref[:, j * LANES:(j + 1) * LANES] = xj[:, :LANES]
            xi_ref[:, j * LANES:(j + 1) * LANES] = xj[:, LANES:]
        ar = jnp.broadcast_to(are_ref[:, cols], (N_STREAMS, SCAN_LANES))
        ai = jnp.broadcast_to(aim_ref[:, cols], (N_STREAMS, SCAN_LANES))

        er = xr_ref[0:N_STREAMS, cols]
        ei = xi_ref[0:N_STREAMS, cols]
        for i in range(1, seg):
            rows = slice(i * N_STREAMS, (i + 1) * N_STREAMS)
            nr = ar * er - ai * ei + xr_ref[rows, cols]
            ni = ar * ei + ai * er + xi_ref[rows, cols]
            xr_ref[rows, cols] = nr
            xi_ref[rows, cols] = ni
            er, ei = nr, ni

        asr = asre_ref[:, cols]
        asi = asim_ref[:, cols]
        cr = hr_ref[:, cols]
        ci = hi_ref[:, cols]
        cmr = jnp.zeros((N_STREAMS, SCAN_LANES), F32)
        cmi = jnp.zeros((N_STREAMS, SCAN_LANES), F32)
        for k in range(N_STREAMS):
            cmr = jnp.where(row == k, cr, cmr)
            cmi = jnp.where(row == k, ci, cmi)
            nr = asr * cr - asi * ci + er[k:k + 1, :]
            ni = asr * ci + asi * cr + ei[k:k + 1, :]
            cr, ci = nr, ni
        hr_ref[:, cols] = cr
        hi_ref[:, cols] = ci

        dr, di = cmr, cmi
        for i in range(seg):
            rows = slice(i * N_STREAMS, (i + 1) * N_STREAMS)
            dr, di = ar * dr - ai * di, ar * di + ai * dr
            xr_ref[rows, cols] += dr
            xi_ref[rows, cols] += di


def _s5_output(u_ref, wcr_ref, wci_ref, dskip_ref, wglu_ref, bglu_ref, permt_ref, xr_ref, xi_ref):
    n_state = xr_ref.shape[1]
    n_out_tiles = wcr_ref.shape[0]
    kw = n_state // n_out_tiles
    ys = []
    for qt in range(n_out_tiles):
        hr = xr_ref[:, qt * kw:(qt + 1) * kw].astype(BF16)
        hi = xi_ref[:, qt * kw:(qt + 1) * kw].astype(BF16)
        ys.append(_dot(hr, wcr_ref[qt]) + _dot(hi, wci_ref[qt]))
    y = jnp.concatenate(ys, axis=-1) + dskip_ref[...] * u_ref[...]
    gl = 0.5 * y * (1.0 + jnp.tanh(math.sqrt(2.0 / math.pi) * (y + 0.044715 * (y * y * y))))
    z = _dot(gl.astype(BF16), wglu_ref[...]) + bglu_ref[...]
    out = gl * jax.nn.sigmoid(z)
    return _dot(permt_ref[...], out.astype(BF16)).astype(BF16)


def _s5_out_kernel(u_ref, sre0_ref, sim0_ref, wb_ref, are_ref, aim_ref, asre_ref, asim_ref,
                   wcr_ref, wci_ref, dskip_ref, wglu_ref, bglu_ref, permt_ref,
                   x_ref, hm_ref, wo_ref, g1_ref, b1_ref, wrh_ref, wrl_ref, br_ref,
                   x1_ref, route_ref, sre_out, sim_out,
                   xr_ref, xi_ref, hr_ref, hi_ref, hs_s, *, seg, alpha, n_batch, n_blocks):
    b = pl.program_id(0)
    c = pl.program_id(1)

    @pl.when(b < n_batch)
    def _():
        @pl.when(jnp.logical_and(b == 0, c == 0))
        def _():
            hs_s[...] = jnp.zeros_like(hs_s)

        @pl.when(c == 0)
        def _():
            hr_ref[...] = sre0_ref[0]
            hi_ref[...] = sim0_ref[0]

        _s5_scan(u_ref, wb_ref, are_ref, aim_ref, asre_ref, asim_ref, xr_ref, xi_ref, hr_ref, hi_ref, seg=seg)
        mix = _dot(jnp.concatenate([hm_ref[...], hs_s[...]], axis=1), wo_ref[...])
        hs_s[...] = _s5_output(u_ref, wcr_ref, wci_ref, dskip_ref, wglu_ref, bglu_ref, permt_ref, xr_ref, xi_ref)
        _outproj_rows(0, mix, x_ref, g1_ref, b1_ref, wrh_ref, wrl_ref, br_ref, x1_ref, route_ref, alpha=alpha)

        @pl.when(c == n_blocks - 1)
        def _():
            sre_out[0] = hr_ref[...]
            sim_out[0] = hi_ref[...]

    @pl.when(b >= n_batch)
    def _():
        x1_ref[...] = jnp.zeros_like(x1_ref)
        route_ref[...] = jnp.zeros_like(route_ref)


def _s5_out(u, sre0, sim0, s5_params, x2, hm, out_params, *, batch, seq, block, alpha, n_total):
    n, ds5 = u.shape
    d = x2.shape[1]
    dm = hm.shape[1]
    n_state = s5_params[1].shape[1]
    nb = seq // block
    n_tail = (n_total - n) // block
    pitch = _token_pitch(d)
    last_b = batch - 1
    cur = lambda b, c: (jnp.minimum(b, last_b) * nb + jnp.minimum(c, nb - 1), 0)
    prev = lambda b, c: (jnp.minimum(b, last_b) * nb + jnp.clip(c - 1, 0, nb - 1), 0)

    def out_block(b, c):
        return jnp.where(b < batch, b * nb + jnp.clip(c - 1, 0, nb - 1), batch * nb + jnp.minimum(c, max(n_tail, 1) - 1))

    per_b = pl.BlockSpec((1, 1, n_state), lambda b, c: (jnp.minimum(b, last_b), 0, 0))
    return pl.pallas_call(
        functools.partial(_s5_out_kernel, seg=block // N_STREAMS, alpha=alpha, n_batch=batch, n_blocks=nb),
        grid=(batch + (1 if n_tail else 0), nb + 1),
        in_specs=[pl.BlockSpec((block, ds5), cur), per_b, per_b] + [_resident(a.shape) for a in s5_params]
                 + [pl.BlockSpec((block, d), prev), pl.BlockSpec((block, dm), prev)]
                 + [_resident(a.shape) for a in out_params],
        out_specs=[pl.BlockSpec((block * pitch, LANES), lambda b, c: (out_block(b, c), 0)),
                   pl.BlockSpec((SUBLANES, block), lambda b, c: (0, out_block(b, c))),
                   per_b, per_b],
        out_shape=[jax.ShapeDtypeStruct((n_total * pitch, LANES), F32),
                   jax.ShapeDtypeStruct((SUBLANES, n_total), F32),
                   jax.ShapeDtypeStruct((batch, 1, n_state), F32),
                   jax.ShapeDtypeStruct((batch, 1, n_state), F32)],
        scratch_shapes=[pltpu.VMEM((block, n_state), F32), pltpu.VMEM((block, n_state), F32),
                        pltpu.VMEM((1, n_state), F32), pltpu.VMEM((1, n_state), F32),
                        pltpu.VMEM((block, ds5), BF16)],
        compiler_params=_params("arbitrary", "arbitrary"),
        name="s5_outproj",
    )(u, sre0, sim0, *s5_params, x2, hm, *out_params)


def _s5(u, sre0, sim0, wb, a_re, a_im, as_re, as_im, wcr, wci, dskip, wglu, bglu, permt,
        *, batch, seq, block):
    n, ds5 = u.shape
    n_state = a_re.shape[1]
    nb = seq // block
    rows = lambda w: pl.BlockSpec((block, w), lambda b, c: (b * nb + c, 0))
    per_b = pl.BlockSpec((1, 1, n_state), lambda b, c: (b, 0, 0))
    const = lambda a: pl.BlockSpec(a.shape, lambda b, c: (0,) * a.ndim)
    return pl.pallas_call(
        functools.partial(_s5_kernel, seg=block // N_STREAMS),
        grid=(batch, nb),
        in_specs=[rows(ds5), per_b, per_b, const(wb), const(a_re), const(a_im), const(as_re),
                  const(as_im), const(wcr), const(wci), const(dskip), const(wglu), const(bglu),
                  const(permt)],
        out_specs=[rows(ds5), per_b, per_b],
        out_shape=[jax.ShapeDtypeStruct((n, ds5), BF16),
                   jax.ShapeDtypeStruct((batch, 1, n_state), F32),
                   jax.ShapeDtypeStruct((batch, 1, n_state), F32)],
        scratch_shapes=[pltpu.VMEM((block, n_state), F32), pltpu.VMEM((block, n_state), F32),
                        pltpu.VMEM((1, n_state), F32), pltpu.VMEM((1, n_state), F32)],
        compiler_params=_params("arbitrary", "arbitrary"),
        name="s5",
    )(u, sre0, sim0, wb, a_re, a_im, as_re, as_im, wcr, wci, dskip, wglu, bglu, permt)


def _layer_norm(x, g, b):
    mu = jnp.mean(x, axis=-1, keepdims=True)
    xc = x - mu
    var = jnp.mean(xc * xc, axis=-1, keepdims=True)
    return xc * lax.rsqrt(var + LN_EPS) * g + b


def _first_max(cols):
    best = cols[0]
    for c in cols[1:]:
        best = jnp.maximum(best, c)
    flags = []
    taken = None
    for c in cols:
        hit = c == best
        if taken is not None:
            hit = jnp.logical_and(hit, jnp.logical_not(taken))
            taken = jnp.logical_or(taken, hit)
        else:
            taken = hit
        flags.append(hit)
    return best, flags


def _outproj_kernel(*refs, alpha, n_valid_blocks):
    x1_ref, route_ref = refs[-2:]
    i = pl.program_id(0)

    @pl.when(i < n_valid_blocks)
    def _():
        _outproj_tile(*refs, alpha=alpha)

    @pl.when(i >= n_valid_blocks)
    def _():
        x1_ref[...] = jnp.zeros_like(x1_ref)
        route_ref[...] = jnp.zeros_like(route_ref)


def _outproj_tile(x_ref, hm_ref, hs_ref, wo_ref, g1_ref, b1_ref, wrh_ref, wrl_ref, br_ref,
                  *rest, alpha):
    x1_ref, route_ref = rest[-2:]
    starts = range(0, x_ref.shape[0], OUT_ROWS)
    mixes = [_dot(jnp.concatenate([hm_ref[r0:r0 + OUT_ROWS, :], hs_ref[r0:r0 + OUT_ROWS, :]], axis=1), wo_ref[...])
             for r0 in starts]
    for r0, mix in zip(starts, mixes):
        _outproj_rows(r0, mix, x_ref, g1_ref, b1_ref, wrh_ref, wrl_ref, br_ref, x1_ref, route_ref, alpha=alpha)


def _outproj_rows(r0, mix, x_ref, g1_ref, b1_ref, wrh_ref, wrl_ref, br_ref, x1_ref, route_ref, *, alpha):
    tm = OUT_ROWS
    d = x_ref.shape[1]
    pitch = x1_ref.shape[0] // x_ref.shape[0]
    rows = slice(r0, r0 + tm)

    def store_token_tile(j, v):
        for g in range(tm // SUBLANES):
            first = (r0 + g * SUBLANES) * pitch + j
            x1_ref[pl.ds(first, SUBLANES, stride=pitch), :] = v[g * SUBLANES:(g + 1) * SUBLANES, :]

    x1 = _layer_norm(alpha * x_ref[rows, :] + mix, g1_ref[...], b1_ref[...])
    for j in range(d // LANES):
        store_token_tile(j, x1[:, j * LANES:(j + 1) * LANES])

    xh = x1.astype(BF16)
    xl = (x1 - xh.astype(F32)).astype(BF16)
    logits = (_dot(xh, wrh_ref[...]) + _dot(xl, wrh_ref[...]) + _dot(xh, wrl_ref[...])).T + br_ref[...]

    l1 = [logits[g:g + 1, :] for g in range(N_EXPERT_GROUPS)]
    m1, gsel = _first_max(l1)
    denom = l1[0] * 0.0
    for c in l1:
        denom = denom + jnp.exp(c - m1)
    p_g = 1.0 / denom
    gid = l1[0] * 0.0
    l2 = []
    for e in range(EXPERTS_PER_GROUP):
        acc = l1[0] * 0.0
        for g in range(N_EXPERT_GROUPS):
            c0 = N_EXPERT_GROUPS + g * EXPERTS_PER_GROUP + e
            acc = acc + jnp.where(gsel[g], logits[c0:c0 + 1, :], 0.0)
        l2.append(acc)
    for g in range(N_EXPERT_GROUPS):
        gid = gid + jnp.where(gsel[g], float(g), 0.0)

    v1, f1 = _first_max(l2)
    masked = [jnp.where(f1[e], -jnp.inf, l2[e]) for e in range(EXPERTS_PER_GROUP)]
    v2, f2 = _first_max(masked)
    e21 = jnp.exp(v2 - v1)
    w_first = 1.0 / (1.0 + e21)
    w_second = e21 / (1.0 + e21)
    pair_id = l1[0] * 0.0
    gate_a = l1[0] * 0.0
    gate_b = l1[0] * 0.0
    pid = 0
    for a in range(EXPERTS_PER_GROUP):
        for b in range(a + 1, EXPERTS_PER_GROUP):
            ab = jnp.logical_and(f1[a], f2[b])
            ba = jnp.logical_and(f1[b], f2[a])
            pair_id = pair_id + jnp.where(jnp.logical_or(ab, ba), float(pid), 0.0)
            gate_a = gate_a + jnp.where(ab, w_first, 0.0) + jnp.where(ba, w_second, 0.0)
            gate_b = gate_b + jnp.where(ab, w_second, 0.0) + jnp.where(ba, w_first, 0.0)
            pid += 1
    bin_id = gid * float(N_PAIRS) + pair_id
    row = lax.broadcasted_iota(jnp.int32, (LANES, tm), 0)
    route_t = jnp.where(row == 0, bin_id,
                        jnp.where(row == 1, p_g * gate_a, jnp.where(row == 2, p_g * gate_b, 0.0)))
    route_ref[:, rows] = route_t[0:SUBLANES, :]
    store_token_tile(d // LANES, route_t.T)
    for j in range(d // LANES + 1, pitch):
        store_token_tile(j, jnp.zeros((tm, LANES), F32))


def _outproj(x2, hm, hs, wo, g1, b1, wrh, wrl, br_col, shared, *, alpha, n_total, row_offset):
    n, d = x2.shape
    dm = hm.shape[1]
    groups = 2 if all(e % (2 * OUT_ROWS) == 0 for e in (n, row_offset, n_total - row_offset)) else 1
    tile = groups * OUT_ROWS
    blk0 = row_offset // tile
    n_blocks = n // tile
    pitch = _token_pitch(d)
    grid = n_blocks if shared is not None else n_total // tile - blk0
    rows = lambda w: pl.BlockSpec((tile, w), lambda i: (jnp.minimum(i, n_blocks - 1), 0))
    kern = functools.partial(_outproj_kernel, alpha=alpha, n_valid_blocks=n_blocks)
    in_specs = [rows(d), rows(dm), rows(hs.shape[1]), _resident(wo.shape),
                _resident(g1.shape), _resident(b1.shape), _resident(wrh.shape), _resident(wrl.shape),
                _resident(br_col.shape)]
    args = [x2, hm, hs, wo, g1, b1, wrh, wrl, br_col]
    aliases = {}
    if shared is not None:
        in_specs += [pl.BlockSpec(memory_space=pl.ANY), pl.BlockSpec(memory_space=pl.ANY)]
        aliases = {len(args): 0, len(args) + 1: 1}
        args = args + list(shared)
    return pl.pallas_call(
        kern,
        grid=(grid,),
        in_specs=in_specs,
        out_specs=[pl.BlockSpec((tile * pitch, LANES), lambda i: (i + blk0, 0)),
                   pl.BlockSpec((SUBLANES, tile), lambda i: (0, i + blk0))],
        out_shape=[jax.ShapeDtypeStruct((n_total * pitch, LANES), F32),
                   jax.ShapeDtypeStruct((SUBLANES, n_total), F32)],
        input_output_aliases=aliases,
        compiler_params=_params("arbitrary"),
        name="outproj",
    )(*args)


def _moe_kernel(ea_ref, eb_ref, nv_ref, npr_ref,
                idx_ref, idxn_ref, idxp_ref, wga_ref, wua_ref, wda_ref, wgb_ref, wub_ref, wdb_ref,
                g2_ref, b2_ref, x1_hbm,
                yp_hbm, ys_hbm,
                xbuf, obuf, xb_ref, gsem, ssem, *, alpha, n_prompt):
    t = pl.program_id(0)
    nt = pl.num_programs(0)
    slot = t % 2
    tm, d = obuf.shape[1:]
    pitch = xbuf.shape[2]
    n_model_tiles = d // LANES

    def gather_copy(tok, r, s):
        src = x1_hbm.at[pl.ds(pl.multiple_of(tok * pitch, SUBLANES), pitch), :]
        return pltpu.make_async_copy(src, xbuf.at[s, r // SUBLANES, :, r % SUBLANES, :], gsem.at[s])

    def tile_lanes(j):
        return xbuf[slot, :, j].reshape(tm, LANES)

    def scatter_copy(tok, r, s, to_prompt):
        dst = yp_hbm.at[pl.ds(tok, 1)] if to_prompt else ys_hbm.at[pl.ds(tok - n_prompt, 1)]
        return pltpu.make_async_copy(obuf.at[s, pl.ds(r, 1)], dst, ssem.at[s])

    def for_rows(lo, hi, fn):
        n_groups = (hi - lo) // ISSUE_UNROLL

        def group(gi, c):
            base = lo + gi * ISSUE_UNROLL
            for j in range(ISSUE_UNROLL):
                fn(base + j)
            return c

        def single(r, c):
            fn(r)
            return c

        lax.fori_loop(0, n_groups, group, 0)
        lax.fori_loop(lo + n_groups * ISSUE_UNROLL, hi, single, 0)

    def scatter_tile(ids, s, n_prompt_rows, n_rows):
        for_rows(0, n_prompt_rows, lambda r: scatter_copy(ids[0, 0, r], r, s, True).start())
        for_rows(n_prompt_rows, n_rows, lambda r: scatter_copy(ids[0, 0, r], r, s, False).start())

    def wait_rows(src, dst, sem, count):
        bit = tm
        while bit >= 1:
            @pl.when((count & bit) != 0)
            def _(bit=bit):
                pltpu.make_async_copy(src.at[pl.ds(0, bit)], dst.at[pl.ds(0, bit)], sem).wait()
            bit //= 2

    n_rows = nv_ref[t]
    mixed = npr_ref[t] != n_rows
    prev = jnp.maximum(t - 1, 0)
    n_prev = jnp.where(jnp.logical_and(t >= 1, npr_ref[prev] == nv_ref[prev]), nv_ref[prev], 0)

    def wait_gather(s):
        pltpu.make_async_copy(xbuf.at[1 - s], xbuf.at[s], gsem.at[s]).wait()

    @pl.when(t == 0)
    def _():
        for_rows(0, tm, lambda r: gather_copy(idx_ref[0, 0, r], r, 0).start())

    @pl.when(jnp.logical_or(t == 0, nv_ref[prev] > 0))
    def _():
        wait_gather(slot)

    @pl.when(t >= 2)
    def _():
        wait_rows(obuf.at[slot], yp_hbm, ssem.at[slot], nv_ref[t - 2])

    @pl.when(n_rows > 0)
    def _():
        for j in range(n_model_tiles):
            xb_ref[:, j * LANES:(j + 1) * LANES] = tile_lanes(j).astype(BF16)
        n_stages = 8

        def issue_neighbours(stage):
            for r in range(stage * tm // n_stages, (stage + 1) * tm // n_stages):
                gather_copy(idxn_ref[0, 0, r], r, 1 - slot).start(priority=r % 2)

                @pl.when(r < n_prev)
                def _(r=r):
                    scatter_copy(idxp_ref[0, 0, r], r, 1 - slot, True).start(priority=r % 2)

        routing = tile_lanes(n_model_tiles)
        gate_a = routing[:, 1:2]
        gate_b = routing[:, 2:3]
        xb = xb_ref[...]
        issue_neighbours(0)
        ha = _dot(xb, wga_ref[0])
        issue_neighbours(1)
        ha = (ha * jax.nn.sigmoid(ha)) * _dot(xb, wua_ref[0])
        issue_neighbours(2)
        ya = _dot(ha.astype(BF16), wda_ref[0])
        issue_neighbours(3)
        hb = _dot(xb, wgb_ref[0])
        issue_neighbours(4)
        hb = (hb * jax.nn.sigmoid(hb)) * _dot(xb, wub_ref[0])
        issue_neighbours(5)
        yb = _dot(hb.astype(BF16), wdb_ref[0])
        issue_neighbours(6)
        moe = gate_a * ya + gate_b * yb
        issue_neighbours(7)
        x1 = jnp.concatenate([tile_lanes(j) for j in range(n_model_tiles)], axis=1)
        obuf[slot] = _layer_norm(alpha * x1 + moe, g2_ref[...], b2_ref[...])

    @pl.when(n_rows == 0)
    def _():
        scatter_tile(idxp_ref, 1 - slot, n_prev, n_prev)

    now = jnp.logical_or(mixed, t == nt - 1)
    scatter_tile(idx_ref, slot, jnp.where(now, npr_ref[t], 0), jnp.where(now, n_rows, 0))

    @pl.when(t == nt - 1)
    def _():
        @pl.when(n_rows > 0)
        def _():
            wait_gather(1 - slot)

        wait_rows(obuf.at[slot], yp_hbm, ssem.at[slot], nv_ref[t])

        @pl.when(t >= 1)
        def _():
            wait_rows(obuf.at[1 - slot], yp_hbm, ssem.at[1 - slot], nv_ref[t - 1])


def _moe(x1_all, route_all, wg, wu, wd, g2, b2, *, alpha, n_prompt):
    n = route_all.shape[1]
    d = wg.shape[1]
    pitch = x1_all.shape[0] // n
    n_p = n_prompt
    n_s = n - n_p
    tm = MOE_TILE
    nt = n // tm + N_BINS if n % tm == 0 else (n + N_BINS * (tm - 1)) // tm + 1
    bins = route_all[0].astype(jnp.int32)

    onehot = (bins[:, None] == jnp.arange(N_BINS, dtype=jnp.int32)[None, :]).astype(jnp.int32)
    csum = jnp.cumsum(onehot, axis=0)
    rank = jnp.sum(csum * onehot, axis=1) - 1
    cnt = csum[n - 1]
    cnt_p = csum[n_p - 1]
    tiles_b = (cnt + tm - 1) // tm
    tile_end = jnp.cumsum(tiles_b)
    tile_start = tile_end - tiles_b
    slot_of_token = jnp.sum(onehot * tile_start[None, :], axis=1) * tm + rank
    idx = jnp.zeros((nt * tm,), jnp.int32).at[slot_of_token].set(
        jnp.arange(n, dtype=jnp.int32), unique_indices=True).reshape(nt, tm)
    tid = jnp.arange(nt, dtype=jnp.int32)
    used = tid < tile_end[N_BINS - 1]
    tbin = jnp.minimum(jnp.sum((tid[:, None] >= tile_end[None, :]).astype(jnp.int32), axis=1), N_BINS - 1)
    last_bin = jnp.max(jnp.where(cnt > 0, jnp.arange(N_BINS, dtype=jnp.int32), 0))
    tbin = jnp.where(used, tbin, last_bin)
    tile_onehot = (tbin[:, None] == jnp.arange(N_BINS, dtype=jnp.int32)[None, :]).astype(jnp.int32)
    lookup = lambda table: jnp.sum(tile_onehot * table[None, :], axis=1)
    kk = tid - lookup(tile_start)
    nvalid = jnp.where(used, jnp.clip(lookup(cnt) - kk * tm, 0, tm), 0).astype(jnp.int32)
    nprompt = jnp.where(used, jnp.clip(lookup(cnt_p) - kk * tm, 0, nvalid), 0).astype(jnp.int32)
    pairs = [(a, b) for a in range(EXPERTS_PER_GROUP) for b in range(a + 1, EXPERTS_PER_GROUP)]
    bin_group = np.arange(N_BINS) // N_PAIRS
    ea = lookup(jnp.asarray(bin_group * EXPERTS_PER_GROUP + np.array([p[0] for p in pairs] * N_EXPERT_GROUPS),
                            jnp.int32))
    eb = lookup(jnp.asarray(bin_group * EXPERTS_PER_GROUP + np.array([p[1] for p in pairs] * N_EXPERT_GROUPS),
                            jnp.int32))
    idx3 = idx.reshape(nt, 1, tm)

    dff = wg.shape[2]
    wspec_in = lambda sel: pl.BlockSpec((1, d, dff), lambda t, ea, eb, nv, npr: (sel(ea, eb)[t], 0, 0))
    wspec_out = lambda sel: pl.BlockSpec((1, dff, d), lambda t, ea, eb, nv, npr: (sel(ea, eb)[t], 0, 0))
    first = lambda a, b: a
    second = lambda a, b: b
    const2 = lambda a: pl.BlockSpec(a.shape, lambda t, *_: (0,) * a.ndim)
    smem_rows = lambda fn: pl.BlockSpec((1, 1, tm), fn, memory_space=pltpu.SMEM)
    grid_spec = pltpu.PrefetchScalarGridSpec(
        num_scalar_prefetch=4,
        grid=(nt,),
        in_specs=[smem_rows(lambda t, *_: (t, 0, 0)),
                  smem_rows(lambda t, *_: (jnp.minimum(t + 1, nt - 1), 0, 0)),
                  smem_rows(lambda t, *_: (jnp.maximum(t - 1, 0), 0, 0)),
                  wspec_in(first), wspec_in(first), wspec_out(first),
                  wspec_in(second), wspec_in(second), wspec_out(second),
                  const2(g2), const2(b2),
                  pl.BlockSpec(memory_space=pl.ANY)],
        out_specs=[pl.BlockSpec(memory_space=pl.ANY), pl.BlockSpec(memory_space=pl.ANY)],
        scratch_shapes=[pltpu.VMEM((2, tm // SUBLANES, pitch, SUBLANES, LANES), F32),
                        pltpu.VMEM((2, tm, d), F32),
                        pltpu.VMEM((tm, d), BF16),
                        pltpu.SemaphoreType.DMA((2,)), pltpu.SemaphoreType.DMA((2,))],
    )
    return pl.pallas_call(
        functools.partial(_moe_kernel, alpha=alpha, n_prompt=n_p),
        grid_spec=grid_spec,
        out_shape=[jax.ShapeDtypeStruct((n_p, d), F32), jax.ShapeDtypeStruct((n_s, d), F32)],
        compiler_params=_params("arbitrary"),
        name="moe",
    )(ea, eb, nvalid, nprompt, idx3, idx3, idx3, wg, wu, wd, wg, wu, wd, g2, b2, x1_all)


def _s5_tables(lam_re, lam_im, log_dt, b_re, b_im, c_re, c_im, seg_lens):
    n_groups, n_state = lam_re.shape
    dt = jnp.exp(log_dt.astype(F32))[:, None]
    lam_re = lam_re.astype(F32)
    lam_im = lam_im.astype(F32)
    mag = jnp.exp(lam_re * dt)
    a_re = mag * jnp.cos(lam_im * dt)
    a_im = mag * jnp.sin(lam_im * dt)
    e_re = a_re - 1.0
    e_im = a_im
    lam_sq = lam_re * lam_re + lam_im * lam_im
    coef_re = (e_re * lam_re + e_im * lam_im) / lam_sq
    coef_im = (e_im * lam_re - e_re * lam_im) / lam_sq
    b_re = b_re.astype(F32)
    b_im = b_im.astype(F32)
    bb_re = coef_re[..., None] * b_re - coef_im[..., None] * b_im
    bb_im = coef_re[..., None] * b_im + coef_im[..., None] * b_re

    gpt = LANES // n_state
    n_tiles = n_groups // gpt
    gps = LANES // S5_GROUP
    g_in_slab = (jnp.arange(n_groups) % gps)
    rows_onehot = jax.nn.one_hot(g_in_slab, gps, dtype=F32)
    w_re = jnp.einsum("gs,gpc->gscp", rows_onehot, bb_re).reshape(n_groups, LANES, n_state)
    w_im = jnp.einsum("gs,gpc->gscp", rows_onehot, bb_im).reshape(n_groups, LANES, n_state)
    w_re = w_re.reshape(n_tiles, gpt, LANES, n_state).transpose(0, 2, 1, 3).reshape(n_tiles, LANES, LANES)
    w_im = w_im.reshape(n_tiles, gpt, LANES, n_state).transpose(0, 2, 1, 3).reshape(n_tiles, LANES, LANES)
    wb = jnp.concatenate([w_re, w_im], axis=-1).astype(BF16)

    n_flat = n_groups * n_state
    out_w = 2 * LANES
    g_per_out = out_w // S5_GROUP
    n_out_tiles = n_groups // g_per_out
    oh = jax.nn.one_hot(jnp.arange(n_groups) % g_per_out, g_per_out, dtype=F32)
    wcr = jnp.einsum("gs,gcp->gpsc", oh, c_re.astype(F32)).reshape(n_out_tiles, g_per_out * n_state, out_w)
    wci = jnp.einsum("gs,gcp->gpsc", oh, -c_im.astype(F32)).reshape(n_out_tiles, g_per_out * n_state, out_w)

    def powers(k):
        return ((mag ** k) * jnp.cos(lam_im * dt * k)).reshape(1, n_flat), \
               ((mag ** k) * jnp.sin(lam_im * dt * k)).reshape(1, n_flat)

    return (wb, a_re.reshape(1, n_flat), a_im.reshape(1, n_flat), [powers(float(k)) for k in seg_lens],
            wcr.astype(BF16), wci.astype(BF16))


def _layer(x_p, x_s, st, lp):
    (w_in, b_gates, conv_w, conv_b, norm_w, lam_re, lam_im, log_dt, b_re, b_im, c_re, c_im,
     d_skip, w_glu, b_glu, w_out, ln1_g, ln1_b, w_r1, b_r1, w_r2, b_r2, w_gate, w_up, w_down,
     ln2_g, ln2_b, alpha) = lp
    state_c, state_n, state_m, state_conv, state_re, state_im = st
    bp, tp, d = x_p.shape
    bs, ts, _ = x_s.shape
    dm = norm_w.shape[0]
    head_dim = dm // M_HEADS
    ds5 = d_skip.shape[0]
    n_groups, n_state = lam_re.shape
    n_flat = n_groups * n_state

    o0 = 2 * dm
    o1 = o0 + 2 * dm
    wqk = w_in[:, :o0].astype(BF16)
    wv = w_in[:, o0:o0 + dm].astype(BF16)
    wo = w_in[:, o0 + dm:o1].astype(BF16)
    wg = jnp.pad(w_in[:, o1:o1 + 2 * M_HEADS], ((0, 0), (0, LANES - 2 * M_HEADS))).astype(BF16)
    wu = w_in[:, o1 + 2 * M_HEADS:].astype(BF16)
    bg = jnp.pad(b_gates.astype(F32), (0, LANES - 2 * M_HEADS)).reshape(1, LANES)
    w_out_b = w_out.astype(BF16)
    wr = jnp.concatenate([w_r1, jnp.transpose(w_r2, (1, 0, 2)).reshape(d, -1)], axis=1).astype(F32)
    n_logits = wr.shape[1]
    wr = jnp.pad(wr, ((0, 0), (0, LANES - n_logits)))
    wrh = wr.astype(BF16)
    wrl = (wr - wrh.astype(F32)).astype(BF16)
    br = jnp.pad(jnp.concatenate([b_r1, b_r2.reshape(-1)]).astype(F32), (0, LANES - n_logits))
    br = br.reshape(LANES, 1)

    blk_p = min(tp, PERM_ROWS)
    blk_s = min(ts, PERM_ROWS)
    seg_p = blk_p // N_STREAMS
    seg_s = blk_s // N_STREAMS
    wb, a_re, a_im, (as_p, as_s), wcr, wci = _s5_tables(lam_re, lam_im, log_dt, b_re, b_im, c_re, c_im,
                                                         (seg_p, seg_s))

    outs = []
    shared = None
    row_offset = 0
    for x, blk, seg, a_seg, zero_state in ((x_p, blk_p, seg_p, as_p, True), (x_s, blk_s, seg_s, as_s, False)):
        b, t, _ = x.shape
        x2 = x.reshape(b * t, d)
        perm_np = _stream_perm(blk, seg)[:blk, :blk]
        perm = jnp.asarray(perm_np, BF16)
        permt = jnp.asarray(perm_np.T, BF16)

        if zero_state:
            c0 = jnp.zeros((b, M_HEADS, head_dim, head_dim), F32)
            n0 = jnp.zeros((b, M_HEADS, head_dim), F32)
            m0 = jnp.zeros((b, M_HEADS), F32)
            conv0 = jnp.zeros((b, CONV_W - 1, 2 * dm), F32)
            re0 = jnp.zeros((b, n_groups, n_state), F32)
            im0 = jnp.zeros((b, n_groups, n_state), F32)
        else:
            c0, n0, m0, conv0, re0, im0 = (state_c.astype(F32), state_n.astype(F32), state_m.astype(F32),
                                           state_conv.astype(F32), state_re.astype(F32), state_im.astype(F32))
        n0p = jnp.pad(n0, ((0, 0), (0, SUBLANES - M_HEADS), (0, 0)))
        m0p = jnp.broadcast_to(jnp.pad(m0, ((0, 0), (0, SUBLANES - M_HEADS)))[:, :, None], (b, SUBLANES, LANES))
        tail0 = jnp.pad(conv0, ((0, 0), (SUBLANES - (CONV_W - 1), 0), (0, 0)))
        chunk = min(t, LANES)
        u, hm, c_new, n_new, m_new, qk_tail = _mixer_in(
            x2, perm, wqk, wv, wo, wu, wg, c0, n0p, m0p, tail0, conv_w.astype(F32),
            conv_b.astype(F32).reshape(1, -1), bg, norm_w.astype(F32).reshape(1, -1), batch=b, seq=t, chunk=chunk)
        s5_params = (wb, a_re, a_im, a_seg[0], a_seg[1], wcr, wci, d_skip.astype(F32).reshape(1, -1),
                     w_glu.astype(BF16), b_glu.astype(F32).reshape(1, -1), permt)
        out_params = (w_out_b, ln1_g.astype(F32).reshape(1, -1), ln1_b.astype(F32).reshape(1, -1), wrh, wrl, br)
        n_total = bp * tp + bs * ts
        re0 = re0.reshape(b, 1, n_flat)
        im0 = im0.reshape(b, 1, n_flat)
        if shared is None and blk == OUT_ROWS and (n_total - b * t) % blk == 0:
            *shared, re_new, im_new = _s5_out(u, re0, im0, s5_params, x2, hm, out_params, batch=b, seq=t,
                                              block=blk, alpha=alpha, n_total=n_total)
        else:
            hs, re_new, im_new = _s5(u, re0, im0, *s5_params, batch=b, seq=t, block=blk)
            shared = _outproj(x2, hm, hs, *out_params, shared, alpha=alpha, n_total=n_total,
                              row_offset=row_offset)
        row_offset += b * t
        conv_new = qk_tail[:, SUBLANES - (CONV_W - 1):]
        states = (c_new, n_new[:, :M_HEADS], m_new[:, :M_HEADS, 0], conv_new,
                  re_new.reshape(b, n_groups, n_state), im_new.reshape(b, n_groups, n_state))
        outs.append(states)

    st_p, st_s = outs
    x1_all, route_all = shared
    yp, ys = _moe(x1_all, route_all, w_gate.astype(BF16), w_up.astype(BF16), w_down.astype(BF16),
                  ln2_g.astype(F32).reshape(1, -1), ln2_b.astype(F32).reshape(1, -1), alpha=alpha,
                  n_prompt=bp * tp)
    return yp.reshape(bp, tp, d), ys.reshape(bs, ts, d), st_p, st_s


def kernel(x_prompt, x_sample, state_mlstm_C, state_mlstm_n, state_mlstm_m, state_conv, state_s5_re, state_s5_im, w_in, b_gates, conv_w, conv_b, mlstm_norm_w, s5_lam_re, s5_lam_im, s5_log_dt, s5_b_re, s5_b_im, s5_c_re, s5_c_im, s5_d, w_glu, b_glu, w_out, ln1_g, ln1_b, w_r1, b_r1, w_r2, b_r2, w_gate, w_up, w_down, ln2_g, ln2_b):
    depth = w_in.shape[0]
    alpha = (2 * depth) ** 0.25
    yp, ys = x_prompt, x_sample
    sts_p, sts_s = [], []
    for l in range(depth):
        lp = (w_in[l], b_gates[l], conv_w[l], conv_b[l], mlstm_norm_w[l], s5_lam_re[l], s5_lam_im[l],
              s5_log_dt[l], s5_b_re[l], s5_b_im[l], s5_c_re[l], s5_c_im[l], s5_d[l], w_glu[l], b_glu[l],
              w_out[l], ln1_g[l], ln1_b[l], w_r1[l], b_r1[l], w_r2[l], b_r2[l], w_gate[l], w_up[l],
              w_down[l], ln2_g[l], ln2_b[l], alpha)
        st = (state_mlstm_C[l], state_mlstm_n[l], state_mlstm_m[l], state_conv[l], state_s5_re[l],
              state_s5_im[l])
        yp, ys, sp, ss = _layer(yp, ys, st, lp)
        sts_p.append(sp)
        sts_s.append(ss)
    stack = lambda sts, i: jnp.stack([s[i] for s in sts])
    return (yp, ys) + tuple(stack(sts_p, i) for i in range(6)) + tuple(stack(sts_s, i) for i in range(6))
```

```python
import functools
import math

import jax
import jax.numpy as jnp
import numpy as np
from jax import lax
from jax.experimental import pallas as pl
from jax.experimental.pallas import tpu as pltpu

F32 = jnp.float32
BF16 = jnp.bfloat16

LANES = 128
SUBLANES = 8
VMEM_LIMIT = 56 * 1024 * 1024

M_HEADS = 4
CONV_W = 4
S5_GROUP = 16
S5_STATE = 64
N_EXPERT_GROUPS = 4
EXPERTS_PER_GROUP = 4
N_PAIRS = 6
N_BINS = N_EXPERT_GROUPS * N_PAIRS
LN_EPS = 1e-5
NEG = -1e30

IN_TILE = 512
PERM_ROWS = 256
OUT_ROWS = 256
MOE_TILE = 256
ISSUE_UNROLL = 8
N_STREAMS = SUBLANES
SCAN_LANES = 512


def _dot(a, b):
    return jnp.dot(a, b, preferred_element_type=F32)


def _dot_nt(a, b):
    return lax.dot_general(a, b, (((1,), (1,)), ((), ())), preferred_element_type=F32)


def _resident(shape):
    nd = len(shape)
    return pl.BlockSpec(shape, lambda *_: (0,) * nd, pipeline_mode=pl.Buffered(1))


def _params(*sem):
    return pltpu.CompilerParams(dimension_semantics=sem, vmem_limit_bytes=VMEM_LIMIT)


def _token_pitch(d):
    return -(-(d // LANES + 1) // SUBLANES) * SUBLANES


def _stream_perm(block, seg):
    n_seg = block // seg
    r = np.arange(block)
    src = (r % n_seg) * seg + r // n_seg
    p = np.zeros((block, block), np.float32)
    p[r, src] = 1.0
    reps = PERM_ROWS // block
    return np.kron(np.eye(reps, dtype=np.float32), p)


def _cumsum_rows(x):
    n = x.shape[0]
    row = lax.broadcasted_iota(jnp.int32, x.shape, 0)
    s = 1
    while s < n:
        x = x + jnp.where(row >= s, pltpu.roll(x, s, 0), 0.0)
        s *= 2
    return x


def _transpose_gate_cols(x):
    sel = (lax.broadcasted_iota(jnp.int32, (SUBLANES, LANES), 0)
           == lax.broadcasted_iota(jnp.int32, (SUBLANES, LANES), 1)).astype(BF16)
    hi = x.astype(BF16)
    r1 = x - hi.astype(F32)
    mid = r1.astype(BF16)
    lo = (r1 - mid.astype(F32)).astype(BF16)
    return _dot_nt(sel, hi) + _dot_nt(sel, mid) + _dot_nt(sel, lo)


def _mixer_in_kernel(x_ref, perm_ref, wqk_ref, wv_ref, wo_ref, wu_ref, wg_ref,
                     c0_ref, n0_ref, m0_ref, tail0_ref, cw_ref, cb_ref, bg_ref, nw_ref,
                     u_ref, h_ref, c_out, n_out, m_out, tail_out,
                     qk_s, v_s, og_s, g_s, ext_ref, c_ref, n_ref, m_ref, *, chunk, head_dim):
    ci = pl.program_id(1)
    n_seq = c_ref.shape[0]

    @pl.when(ci == 0)
    def _():
        c_ref[...] = c0_ref[...]
        n_ref[...] = n0_ref[...]
        m_ref[...] = m0_ref[...]
        ext_ref[:, 0:SUBLANES, :] = tail0_ref[...]

    tile_rows = x_ref.shape[0]
    seq_rows = tile_rows // n_seq
    group = perm_ref.shape[0]
    for r0 in range(0, tile_rows, group):
        rows = slice(r0, r0 + group)
        xb = x_ref[rows, :].astype(BF16)
        qk_s[rows, :] = _dot(xb, wqk_ref[...])
        v_s[rows, :] = _dot(xb, wv_ref[...]).astype(BF16)
        og_s[rows, :] = _dot(xb, wo_ref[...])
        g_s[rows, :] = _dot(xb, wg_ref[...])
        u_ref[rows, :] = _dot(_dot(perm_ref[...], xb).astype(BF16), wu_ref[...])

    for c0 in range(0, tile_rows, chunk):
        s = c0 // seq_rows
        _mlstm_chunk(slice(c0, c0 + chunk), qk_s, v_s, og_s, g_s, cw_ref, cb_ref, bg_ref, nw_ref,
                     h_ref, ext_ref.at[s], c_ref.at[s], n_ref.at[s], m_ref.at[s], head_dim=head_dim)

    @pl.when(ci == pl.num_programs(1) - 1)
    def _():
        c_out[...] = c_ref[...]
        n_out[...] = n_ref[...]
        m_out[...] = m_ref[...]
        tail_out[...] = ext_ref[:, 0:SUBLANES, :]


def _mlstm_chunk(rows, qk_ref, v_ref, og_ref, g_ref, cw_ref, cb_ref, bg_ref, nw_ref,
                 h_ref, ext_ref, c_ref, n_ref, m_ref, *, head_dim):
    dm = M_HEADS * head_dim
    L = rows.stop - rows.start

    ext_ref[SUBLANES:SUBLANES + L, :] = qk_ref[rows, :]
    full = ext_ref[...]
    acc = cb_ref[...] + full[SUBLANES:, :] * cw_ref[CONV_W - 1:CONV_W, :]
    for back in range(1, CONV_W):
        tap = cw_ref[CONV_W - 1 - back:CONV_W - back, :]
        acc = acc + pltpu.roll(full, back, 0)[SUBLANES:, :] * tap
    tail = ext_ref[L:L + SUBLANES, :]
    ext_ref[0:SUBLANES, :] = tail
    qk = acc * jax.nn.sigmoid(acc)

    gates = g_ref[rows, :] + bg_ref[...]
    fpre = pltpu.roll(gates, LANES - M_HEADS, 1)
    logf = jnp.minimum(fpre, 0.0) - jnp.log1p(jnp.exp(-jnp.abs(fpre)))
    bcum = _cumsum_rows(logf)
    rrow = _transpose_gate_cols(gates - bcum)
    tri = (lax.broadcasted_iota(jnp.int32, (L, L), 0) >= lax.broadcasted_iota(jnp.int32, (L, L), 1))

    for h in range(M_HEADS):
        lo, hi = h * head_dim, (h + 1) * head_dim
        q = qk[:, lo:hi]
        k = qk[:, dm + lo:dm + hi] * (head_dim ** -0.5)
        v = v_ref[rows, lo:hi]
        qb = q.astype(BF16)
        kb = k.astype(BF16)
        b_col = bcum[:, h:h + 1]
        ig_col = gates[:, h:h + 1]
        m_prev = m_ref[h:h + 1, 0:1]
        c_prev = c_ref[h]
        n_prev = n_ref[h:h + 1, :]

        dmat = jnp.where(tri, b_col + rrow[h:h + 1, :], NEG)
        inter = b_col + m_prev
        m_t = jnp.maximum(inter, jnp.max(dmat, axis=-1, keepdims=True))
        w_intra = jnp.exp(dmat - m_t)
        w_inter = jnp.exp(inter - m_t)
        s = _dot_nt(qb, kb) * w_intra
        num = w_inter * _dot(qb, c_prev.astype(BF16)) + _dot(s.astype(BF16), v)
        den = w_inter * jnp.sum(q * n_prev, axis=-1, keepdims=True) + jnp.sum(s, axis=-1, keepdims=True)
        hh = num / jnp.maximum(jnp.abs(den), jnp.exp(-m_t))

        m_new = m_t[L - 1:L, :]
        b_last = b_col[L - 1:L, :]
        w_s = jnp.exp(b_last - b_col + ig_col - m_new)
        decay = jnp.exp(b_last + m_prev - m_new)
        kw = k * w_s
        c_ref[h] = decay * c_prev + _dot(kw.T.astype(BF16), v)
        n_ref[h:h + 1, :] = decay * n_prev + jnp.sum(kw, axis=0, keepdims=True)
        m_ref[h:h + 1, :] = jnp.broadcast_to(m_new, (1, LANES))

        mu = jnp.mean(hh, axis=-1, keepdims=True)
        hc = hh - mu
        var = jnp.mean(hc * hc, axis=-1, keepdims=True)
        hn = hc * lax.rsqrt(var + LN_EPS) * nw_ref[:, lo:hi]
        h_ref[rows, lo:hi] = (jax.nn.sigmoid(og_ref[rows, lo:hi]) * hn).astype(BF16)


def _mixer_in(x2, perm, wqk, wv, wo, wu, wg, c0, n0p, m0p, tail0, conv_w, conv_b, bg, norm_w,
              *, batch, seq, chunk):
    n, d = x2.shape
    dm = wv.shape[1]
    ds5 = wu.shape[1]
    head_dim = dm // M_HEADS
    n_seq = PERM_ROWS // seq if seq < PERM_ROWS and batch % (PERM_ROWS // seq) == 0 else 1
    tile = min(seq, IN_TILE) * n_seq
    nt = seq * n_seq // tile
    rows = lambda w: pl.BlockSpec((tile, w), lambda b, c: (b * nt + c, 0))
    per_b = lambda *s: pl.BlockSpec((n_seq,) + s, lambda b, c: (b,) + (0,) * len(s))
    group = min(tile, PERM_ROWS)
    weights = [perm[:group, :group], wqk, wv, wo, wu, wg]
    small = [conv_w, conv_b, bg, norm_w]
    return pl.pallas_call(
        functools.partial(_mixer_in_kernel, chunk=chunk, head_dim=head_dim),
        grid=(batch // n_seq, nt),
        in_specs=[rows(d)] + [_resident(w.shape) for w in weights]
                 + [per_b(M_HEADS, head_dim, head_dim), per_b(SUBLANES, head_dim), per_b(SUBLANES, LANES),
                    per_b(SUBLANES, 2 * dm)] + [_resident(a.shape) for a in small],
        out_specs=[rows(ds5), rows(dm), per_b(M_HEADS, head_dim, head_dim), per_b(SUBLANES, head_dim),
                   per_b(SUBLANES, LANES), per_b(SUBLANES, 2 * dm)],
        out_shape=[jax.ShapeDtypeStruct((n, ds5), F32), jax.ShapeDtypeStruct((n, dm), BF16),
                   jax.ShapeDtypeStruct((batch, M_HEADS, head_dim, head_dim), F32),
                   jax.ShapeDtypeStruct((batch, SUBLANES, head_dim), F32),
                   jax.ShapeDtypeStruct((batch, SUBLANES, LANES), F32),
                   jax.ShapeDtypeStruct((batch, SUBLANES, 2 * dm), F32)],
        scratch_shapes=[pltpu.VMEM((tile, 2 * dm), F32), pltpu.VMEM((tile, dm), BF16),
                        pltpu.VMEM((tile, dm), F32), pltpu.VMEM((tile, LANES), F32),
                        pltpu.VMEM((n_seq, chunk + SUBLANES, 2 * dm), F32),
                        pltpu.VMEM((n_seq, M_HEADS, head_dim, head_dim), F32),
                        pltpu.VMEM((n_seq, SUBLANES, head_dim), F32),
                        pltpu.VMEM((n_seq, SUBLANES, LANES), F32)],
        compiler_params=_params("arbitrary", "arbitrary"),
        name="mixer_in",
    )(x2, *weights, c0, n0p, m0p, tail0, *small)


def _s5_kernel(u_ref, sre0_ref, sim0_ref, wb_ref, are_ref, aim_ref, asre_ref, asim_ref,
               wcr_ref, wci_ref, dskip_ref, wglu_ref, bglu_ref, permt_ref,
               hs_ref, sre_out, sim_out,
               xr_ref, xi_ref, hr_ref, hi_ref, *, seg):
    bi = pl.program_id(1)

    @pl.when(bi == 0)
    def _():
        hr_ref[...] = sre0_ref[0]
        hi_ref[...] = sim0_ref[0]

    _s5_scan(u_ref, wb_ref, are_ref, aim_ref, asre_ref, asim_ref, xr_ref, xi_ref, hr_ref, hi_ref, seg=seg)
    hs_ref[...] = _s5_output(u_ref, wcr_ref, wci_ref, dskip_ref, wglu_ref, bglu_ref, permt_ref, xr_ref, xi_ref)

    @pl.when(bi == pl.num_programs(1) - 1)
    def _():
        sre_out[0] = hr_ref[...]
        sim_out[0] = hi_ref[...]


def _s5_scan(u_ref, wb_ref, are_ref, aim_ref, asre_ref, asim_ref, xr_ref, xi_ref, hr_ref, hi_ref, *, seg):
    n_state = xr_ref.shape[1]

    ub = u_ref[...].astype(BF16)
    tiles_per_slab = LANES // (2 * S5_GROUP)
    row = lax.broadcasted_iota(jnp.int32, (N_STREAMS, SCAN_LANES), 0)
    for sl in range(n_state // SCAN_LANES):
        cols = slice(sl * SCAN_LANES, (sl + 1) * SCAN_LANES)
        for j in range(sl * (SCAN_LANES // LANES), (sl + 1) * (SCAN_LANES // LANES)):
            cs = (j // tiles_per_slab) * LANES
            xj = _dot(ub[:, cs:cs + LANES], wb_ref[j])
            xr_ref[:, j * LANES:(j + 1) * LANES] = xj[:, :LANES]
            xi_ref[:, j * LANES:(j + 1) * LANES] = xj[:, LANES:]
        ar = jnp.broadcast_to(are_ref[:, cols], (N_STREAMS, SCAN_LANES))
        ai = jnp.broadcast_to(aim_ref[:, cols], (N_STREAMS, SCAN_LANES))

        er = xr_ref[0:N_STREAMS, cols]
        ei = xi_ref[0:N_STREAMS, cols]
        for i in range(1, seg):
            rows = slice(i * N_STREAMS, (i + 1) * N_STREAMS)
            nr = ar * er - ai * ei + xr_ref[rows, cols]
            ni = ar * ei + ai * er + xi_ref[rows, cols]
            xr_ref[rows, cols] = nr
            xi_ref[rows, cols] = ni
            er, ei = nr, ni

        asr = asre_ref[:, cols]
        asi = asim_ref[:, cols]
        cr = hr_ref[:, cols]
        ci = hi_ref[:, cols]
        cmr = jnp.zeros((N_STREAMS, SCAN_LANES), F32)
        cmi = jnp.zeros((N_STREAMS, SCAN_LANES), F32)
        for k in range(N_STREAMS):
            cmr = jnp.where(row == k, cr, cmr)
            cmi = jnp.where(row == k, ci, cmi)
            nr = asr * cr - asi * ci + er[k:k + 1, :]
            ni = asr * ci + asi * cr + ei[k:k + 1, :]
            cr, ci = nr, ni
        hr_ref[:, cols] = cr
        hi_ref[:, cols] = ci

        dr, di = cmr, cmi
        for i in range(seg):
            rows = slice(i * N_STREAMS, (i + 1) * N_STREAMS)
            dr, di = ar * dr - ai * di, ar * di + ai * dr
            xr_ref[rows, cols] += dr
            xi_ref[rows, cols] += di


def _s5_output(u_ref, wcr_ref, wci_ref, dskip_ref, wglu_ref, bglu_ref, permt_ref, xr_ref, xi_ref):
    n_state = xr_ref.shape[1]
    n_out_tiles = wcr_ref.shape[0]
    kw = n_state // n_out_tiles
    ys = []
    for qt in range(n_out_tiles):
        hr = xr_ref[:, qt * kw:(qt + 1) * kw].astype(BF16)
        hi = xi_ref[:, qt * kw:(qt + 1) * kw].astype(BF16)
        ys.append(_dot(hr, wcr_ref[qt]) + _dot(hi, wci_ref[qt]))
    y = jnp.concatenate(ys, axis=-1) + dskip_ref[...] * u_ref[...]
    gl = 0.5 * y * (1.0 + jnp.tanh(math.sqrt(2.0 / math.pi) * (y + 0.044715 * (y * y * y))))
    z = _dot(gl.astype(BF16), wglu_ref[...]) + bglu_ref[...]
    out = gl * jax.nn.sigmoid(z)
    return _dot(permt_ref[...], out.astype(BF16)).astype(BF16)


def _s5(u, sre0, sim0, wb, a_re, a_im, as_re, as_im, wcr, wci, dskip, wglu, bglu, permt,
        *, batch, seq, block):
    n, ds5 = u.shape
    n_state = a_re.shape[1]
    nb = seq // block
    rows = lambda w: pl.BlockSpec((block, w), lambda b, c: (b * nb + c, 0))
    per_b = pl.BlockSpec((1, 1, n_state), lambda b, c: (b, 0, 0))
    const = lambda a: pl.BlockSpec(a.shape, lambda b, c: (0,) * a.ndim)
    return pl.pallas_call(
        functools.partial(_s5_kernel, seg=block // N_STREAMS),
        grid=(batch, nb),
        in_specs=[rows(ds5), per_b, per_b, const(wb), const(a_re), const(a_im), const(as_re),
                  const(as_im), const(wcr), const(wci), const(dskip), const(wglu), const(bglu),
                  const(permt)],
        out_specs=[rows(ds5), per_b, per_b],
        out_shape=[jax.ShapeDtypeStruct((n, ds5), BF16),
                   jax.ShapeDtypeStruct((batch, 1, n_state), F32),
                   jax.ShapeDtypeStruct((batch, 1, n_state), F32)],
        scratch_shapes=[pltpu.VMEM((block, n_state), F32), pltpu.VMEM((block, n_state), F32),
                        pltpu.VMEM((1, n_state), F32), pltpu.VMEM((1, n_state), F32)],
        compiler_params=_params("arbitrary", "arbitrary"),
        name="s5",
    )(u, sre0, sim0, wb, a_re, a_im, as_re, as_im, wcr, wci, dskip, wglu, bglu, permt)


def _layer_norm(x, g, b):
    mu = jnp.mean(x, axis=-1, keepdims=True)
    xc = x - mu
    var = jnp.mean(xc * xc, axis=-1, keepdims=True)
    return xc * lax.rsqrt(var + LN_EPS) * g + b


def _first_max(cols):
    best = cols[0]
    for c in cols[1:]:
        best = jnp.maximum(best, c)
    flags = []
    taken = None
    for c in cols:
        hit = c == best
        if taken is not None:
            hit = jnp.logical_and(hit, jnp.logical_not(taken))
            taken = jnp.logical_or(taken, hit)
        else:
            taken = hit
        flags.append(hit)
    return best, flags


def _outproj_kernel(*refs, alpha, n_valid_blocks):
    x1_ref, route_ref = refs[-2:]
    i = pl.program_id(0)

    @pl.when(i < n_valid_blocks)
    def _():
        _outproj_tile(*refs, alpha=alpha)

    @pl.when(i >= n_valid_blocks)
    def _():
        x1_ref[...] = jnp.zeros_like(x1_ref)
        route_ref[...] = jnp.zeros_like(route_ref)


def _outproj_tile(x_ref, hm_ref, hs_ref, wo_ref, g1_ref, b1_ref, wrh_ref, wrl_ref, br_ref,
                  *rest, alpha):
    x1_ref, route_ref = rest[-2:]
    starts = range(0, x_ref.shape[0], OUT_ROWS)
    mixes = [_dot(jnp.concatenate([hm_ref[r0:r0 + OUT_ROWS, :], hs_ref[r0:r0 + OUT_ROWS, :]], axis=1), wo_ref[...])
             for r0 in starts]
    for r0, mix in zip(starts, mixes):
        _outproj_rows(r0, mix, x_ref, g1_ref, b1_ref, wrh_ref, wrl_ref, br_ref, x1_ref, route_ref, alpha=alpha)


def _outproj_rows(r0, mix, x_ref, g1_ref, b1_ref, wrh_ref, wrl_ref, br_ref, x1_ref, route_ref, *, alpha):
    tm = OUT_ROWS
    d = x_ref.shape[1]
    pitch = x1_ref.shape[0] // x_ref.shape[0]
    rows = slice(r0, r0 + tm)

    def store_token_tile(j, v):
        for g in range(tm // SUBLANES):
            first = (r0 + g * SUBLANES) * pitch + j
            x1_ref[pl.ds(first, SUBLANES, stride=pitch), :] = v[g * SUBLANES:(g + 1) * SUBLANES, :]

    x1 = _layer_norm(alpha * x_ref[rows, :] + mix, g1_ref[...], b1_ref[...])
    for j in range(d // LANES):
        store_token_tile(j, x1[:, j * LANES:(j + 1) * LANES])

    xh = x1.astype(BF16)
    xl = (x1 - xh.astype(F32)).astype(BF16)
    logits = (_dot(xh, wrh_ref[...]) + _dot(xl, wrh_ref[...]) + _dot(xh, wrl_ref[...])).T + br_ref[...]

    l1 = [logits[g:g + 1, :] for g in range(N_EXPERT_GROUPS)]
    m1, gsel = _first_max(l1)
    denom = l1[0] * 0.0
    for c in l1:
        denom = denom + jnp.exp(c - m1)
    p_g = 1.0 / denom
    gid = l1[0] * 0.0
    l2 = []
    for e in range(EXPERTS_PER_GROUP):
        acc = l1[0] * 0.0
        for g in range(N_EXPERT_GROUPS):
            c0 = N_EXPERT_GROUPS + g * EXPERTS_PER_GROUP + e
            acc = acc + jnp.where(gsel[g], logits[c0:c0 + 1, :], 0.0)
        l2.append(acc)
    for g in range(N_EXPERT_GROUPS):
        gid = gid + jnp.where(gsel[g], float(g), 0.0)

    v1, f1 = _first_max(l2)
    masked = [jnp.where(f1[e], -jnp.inf, l2[e]) for e in range(EXPERTS_PER_GROUP)]
    v2, f2 = _first_max(masked)
    e21 = jnp.exp(v2 - v1)
    w_first = 1.0 / (1.0 + e21)
    w_second = e21 / (1.0 + e21)
    pair_id = l1[0] * 0.0
    gate_a = l1[0] * 0.0
    gate_b = l1[0] * 0.0
    pid = 0
    for a in range(EXPERTS_PER_GROUP):
        for b in range(a + 1, EXPERTS_PER_GROUP):
            ab = jnp.logical_and(f1[a], f2[b])
            ba = jnp.logical_and(f1[b], f2[a])
            pair_id = pair_id + jnp.where(jnp.logical_or(ab, ba), float(pid), 0.0)
            gate_a = gate_a + jnp.where(ab, w_first, 0.0) + jnp.where(ba, w_second, 0.0)
            gate_b = gate_b + jnp.where(ab, w_second, 0.0) + jnp.where(ba, w_first, 0.0)
            pid += 1
    bin_id = gid * float(N_PAIRS) + pair_id
    row = lax.broadcasted_iota(jnp.int32, (LANES, tm), 0)
    route_t = jnp.where(row == 0, bin_id,
                        jnp.where(row == 1, p_g * gate_a, jnp.where(row == 2, p_g * gate_b, 0.0)))
    route_ref[:, rows] = route_t[0:SUBLANES, :]
    store_token_tile(d // LANES, route_t.T)
    for j in range(d // LANES + 1, pitch):
        store_token_tile(j, jnp.zeros((tm, LANES), F32))


def _outproj(x2, hm, hs, wo, g1, b1, wrh, wrl, br_col, shared, *, alpha, n_total, row_offset):
    n, d = x2.shape
    dm = hm.shape[1]
    groups = 2 if all(e % (2 * OUT_ROWS) == 0 for e in (n, row_offset, n_total - row_offset)) else 1
    tile = groups * OUT_ROWS
    blk0 = row_offset // tile
    n_blocks = n // tile
    pitch = _token_pitch(d)
    grid = n_blocks if shared is not None else n_total // tile - blk0
    rows = lambda w: pl.BlockSpec((tile, w), lambda i: (jnp.minimum(i, n_blocks - 1), 0))
    kern = functools.partial(_outproj_kernel, alpha=alpha, n_valid_blocks=n_blocks)
    in_specs = [rows(d), rows(dm), rows(hs.shape[1]), _resident(wo.shape),
                _resident(g1.shape), _resident(b1.shape), _resident(wrh.shape), _resident(wrl.shape),
                _resident(br_col.shape)]
    args = [x2, hm, hs, wo, g1, b1, wrh, wrl, br_col]
    aliases = {}
    if shared is not None:
        in_specs += [pl.BlockSpec(memory_space=pl.ANY), pl.BlockSpec(memory_space=pl.ANY)]
        aliases = {len(args): 0, len(args) + 1: 1}
        args = args + list(shared)
    return pl.pallas_call(
        kern,
        grid=(grid,),
        in_specs=in_specs,
        out_specs=[pl.BlockSpec((tile * pitch, LANES), lambda i: (i + blk0, 0)),
                   pl.BlockSpec((SUBLANES, tile), lambda i: (0, i + blk0))],
        out_shape=[jax.ShapeDtypeStruct((n_total * pitch, LANES), F32),
                   jax.ShapeDtypeStruct((SUBLANES, n_total), F32)],
        input_output_aliases=aliases,
        compiler_params=_params("arbitrary"),
        name="outproj",
    )(*args)


def _moe_kernel(ea_ref, eb_ref, nv_ref, npr_ref,
                idx_ref, idxn_ref, idxp_ref, wga_ref, wua_ref, wda_ref, wgb_ref, wub_ref, wdb_ref,
                g2_ref, b2_ref, x1_hbm,
                yp_hbm, ys_hbm,
                xbuf, obuf, xb_ref, gsem, ssem, *, alpha, n_prompt):
    t = pl.program_id(0)
    nt = pl.num_programs(0)
    slot = t % 2
    tm, d = obuf.shape[1:]
    pitch = xbuf.shape[2]
    n_model_tiles = d // LANES

    def gather_copy(tok, r, s):
        src = x1_hbm.at[pl.ds(pl.multiple_of(tok * pitch, SUBLANES), pitch), :]
        return pltpu.make_async_copy(src, xbuf.at[s, r // SUBLANES, :, r % SUBLANES, :], gsem.at[s])

    def tile_lanes(j):
        return xbuf[slot, :, j].reshape(tm, LANES)

    def scatter_copy(tok, r, s, to_prompt):
        dst = yp_hbm.at[pl.ds(tok, 1)] if to_prompt else ys_hbm.at[pl.ds(tok - n_prompt, 1)]
        return pltpu.make_async_copy(obuf.at[s, pl.ds(r, 1)], dst, ssem.at[s])

    def for_rows(lo, hi, fn):
        n_groups = (hi - lo) // ISSUE_UNROLL

        def group(gi, c):
            base = lo + gi * ISSUE_UNROLL
            for j in range(ISSUE_UNROLL):
                fn(base + j)
            return c

        def single(r, c):
            fn(r)
            return c

        lax.fori_loop(0, n_groups, group, 0)
        lax.fori_loop(lo + n_groups * ISSUE_UNROLL, hi, single, 0)

    def scatter_tile(ids, s, n_prompt_rows, n_rows):
        for_rows(0, n_prompt_rows, lambda r: scatter_copy(ids[0, 0, r], r, s, True).start())
        for_rows(n_prompt_rows, n_rows, lambda r: scatter_copy(ids[0, 0, r], r, s, False).start())

    def wait_rows(src, dst, sem, count):
        bit = tm
        while bit >= 1:
            @pl.when((count & bit) != 0)
            def _(bit=bit):
                pltpu.make_async_copy(src.at[pl.ds(0, bit)], dst.at[pl.ds(0, bit)], sem).wait()
            bit //= 2

    n_rows = nv_ref[t]
    mixed = npr_ref[t] != n_rows
    prev = jnp.maximum(t - 1, 0)
    n_prev = jnp.where(jnp.logical_and(t >= 1, npr_ref[prev] == nv_ref[prev]), nv_ref[prev], 0)

    def wait_gather(s):
        pltpu.make_async_copy(xbuf.at[1 - s], xbuf.at[s], gsem.at[s]).wait()

    @pl.when(t == 0)
    def _():
        for_rows(0, tm, lambda r: gather_copy(idx_ref[0, 0, r], r, 0).start())

    @pl.when(jnp.logical_or(t == 0, nv_ref[prev] > 0))
    def _():
        wait_gather(slot)

    @pl.when(t >= 2)
    def _():
        wait_rows(obuf.at[slot], yp_hbm, ssem.at[slot], nv_ref[t - 2])

    @pl.when(n_rows > 0)
    def _():
        for j in range(n_model_tiles):
            xb_ref[:, j * LANES:(j + 1) * LANES] = tile_lanes(j).astype(BF16)
        n_stages = 8

        def issue_neighbours(stage):
            for r in range(stage * tm // n_stages, (stage + 1) * tm // n_stages):
                gather_copy(idxn_ref[0, 0, r], r, 1 - slot).start(priority=r % 2)

                @pl.when(r < n_prev)
                def _(r=r):
                    scatter_copy(idxp_ref[0, 0, r], r, 1 - slot, True).start(priority=r % 2)

        routing = tile_lanes(n_model_tiles)
        gate_a = routing[:, 1:2]
        gate_b = routing[:, 2:3]
        xb = xb_ref[...]
        issue_neighbours(0)
        ha = _dot(xb, wga_ref[0])
        issue_neighbours(1)
        ha = (ha * jax.nn.sigmoid(ha)) * _dot(xb, wua_ref[0])
        issue_neighbours(2)
        ya = _dot(ha.astype(BF16), wda_ref[0])
        issue_neighbours(3)
        hb = _dot(xb, wgb_ref[0])
        issue_neighbours(4)
        hb = (hb * jax.nn.sigmoid(hb)) * _dot(xb, wub_ref[0])
        issue_neighbours(5)
        yb = _dot(hb.astype(BF16), wdb_ref[0])
        issue_neighbours(6)
        moe = gate_a * ya + gate_b * yb
        issue_neighbours(7)
        x1 = jnp.concatenate([tile_lanes(j) for j in range(n_model_tiles)], axis=1)
        obuf[slot] = _layer_norm(alpha * x1 + moe, g2_ref[...], b2_ref[...])

    @pl.when(n_rows == 0)
    def _():
        scatter_tile(idxp_ref, 1 - slot, n_prev, n_prev)

    now = jnp.logical_or(mixed, t == nt - 1)
    scatter_tile(idx_ref, slot, jnp.where(now, npr_ref[t], 0), jnp.where(now, n_rows, 0))

    @pl.when(t == nt - 1)
    def _():
        @pl.when(n_rows > 0)
        def _():
            wait_gather(1 - slot)

        wait_rows(obuf.at[slot], yp_hbm, ssem.at[slot], nv_ref[t])

        @pl.when(t >= 1)
        def _():
            wait_rows(obuf.at[1 - slot], yp_hbm, ssem.at[1 - slot], nv_ref[t - 1])


def _moe(x1_all, route_all, wg, wu, wd, g2, b2, *, alpha, n_prompt):
    n = route_all.shape[1]
    d = wg.shape[1]
    pitch = x1_all.shape[0] // n
    n_p = n_prompt
    n_s = n - n_p
    tm = MOE_TILE
    nt = n // tm + N_BINS if n % tm == 0 else (n + N_BINS * (tm - 1)) // tm + 1
    bins = route_all[0].astype(jnp.int32)

    onehot = (bins[:, None] == jnp.arange(N_BINS, dtype=jnp.int32)[None, :]).astype(jnp.int32)
    csum = jnp.cumsum(onehot, axis=0)
    rank = jnp.sum(csum * onehot, axis=1) - 1
    cnt = csum[n - 1]
    cnt_p = csum[n_p - 1]
    tiles_b = (cnt + tm - 1) // tm
    tile_end = jnp.cumsum(tiles_b)
    tile_start = tile_end - tiles_b
    slot_of_token = jnp.sum(onehot * tile_start[None, :], axis=1) * tm + rank
    idx = jnp.zeros((nt * tm,), jnp.int32).at[slot_of_token].set(
        jnp.arange(n, dtype=jnp.int32), unique_indices=True).reshape(nt, tm)
    tid = jnp.arange(nt, dtype=jnp.int32)
    used = tid < tile_end[N_BINS - 1]
    tbin = jnp.minimum(jnp.sum((tid[:, None] >= tile_end[None, :]).astype(jnp.int32), axis=1), N_BINS - 1)
    last_bin = jnp.max(jnp.where(cnt > 0, jnp.arange(N_BINS, dtype=jnp.int32), 0))
    tbin = jnp.where(used, tbin, last_bin)
    tile_onehot = (tbin[:, None] == jnp.arange(N_BINS, dtype=jnp.int32)[None, :]).astype(jnp.int32)
    lookup = lambda table: jnp.sum(tile_onehot * table[None, :], axis=1)
    kk = tid - lookup(tile_start)
    nvalid = jnp.where(used, jnp.clip(lookup(cnt) - kk * tm, 0, tm), 0).astype(jnp.int32)
    nprompt = jnp.where(used, jnp.clip(lookup(cnt_p) - kk * tm, 0, nvalid), 0).astype(jnp.int32)
    pairs = [(a, b) for a in range(EXPERTS_PER_GROUP) for b in range(a + 1, EXPERTS_PER_GROUP)]
    bin_group = np.arange(N_BINS) // N_PAIRS
    ea = lookup(jnp.asarray(bin_group * EXPERTS_PER_GROUP + np.array([p[0] for p in pairs] * N_EXPERT_GROUPS),
                            jnp.int32))
    eb = lookup(jnp.asarray(bin_group * EXPERTS_PER_GROUP + np.array([p[1] for p in pairs] * N_EXPERT_GROUPS),
                            jnp.int32))
    idx3 = idx.reshape(nt, 1, tm)

    dff = wg.shape[2]
    wspec_in = lambda sel: pl.BlockSpec((1, d, dff), lambda t, ea, eb, nv, npr: (sel(ea, eb)[t], 0, 0))
    wspec_out = lambda sel: pl.BlockSpec((1, dff, d), lambda t, ea, eb, nv, npr: (sel(ea, eb)[t], 0, 0))
    first = lambda a, b: a
    second = lambda a, b: b
    const2 = lambda a: pl.BlockSpec(a.shape, lambda t, *_: (0,) * a.ndim)
    smem_rows = lambda fn: pl.BlockSpec((1, 1, tm), fn, memory_space=pltpu.SMEM)
    grid_spec = pltpu.PrefetchScalarGridSpec(
        num_scalar_prefetch=4,
        grid=(nt,),
        in_specs=[smem_rows(lambda t, *_: (t, 0, 0)),
                  smem_rows(lambda t, *_: (jnp.minimum(t + 1, nt - 1), 0, 0)),
                  smem_rows(lambda t, *_: (jnp.maximum(t - 1, 0), 0, 0)),
                  wspec_in(first), wspec_in(first), wspec_out(first),
                  wspec_in(second), wspec_in(second), wspec_out(second),
                  const2(g2), const2(b2),
                  pl.BlockSpec(memory_space=pl.ANY)],
        out_specs=[pl.BlockSpec(memory_space=pl.ANY), pl.BlockSpec(memory_space=pl.ANY)],
        scratch_shapes=[pltpu.VMEM((2, tm // SUBLANES, pitch, SUBLANES, LANES), F32),
                        pltpu.VMEM((2, tm, d), F32),
                        pltpu.VMEM((tm, d), BF16),
                        pltpu.SemaphoreType.DMA((2,)), pltpu.SemaphoreType.DMA((2,))],
    )
    return pl.pallas_call(
        functools.partial(_moe_kernel, alpha=alpha, n_prompt=n_p),
        grid_spec=grid_spec,
        out_shape=[jax.ShapeDtypeStruct((n_p, d), F32), jax.ShapeDtypeStruct((n_s, d), F32)],
        compiler_params=_params("arbitrary"),
        name="moe",
    )(ea, eb, nvalid, nprompt, idx3, idx3, idx3, wg, wu, wd, wg, wu, wd, g2, b2, x1_all)


def _s5_tables(lam_re, lam_im, log_dt, b_re, b_im, c_re, c_im, seg_lens):
    n_groups, n_state = lam_re.shape
    dt = jnp.exp(log_dt.astype(F32))[:, None]
    lam_re = lam_re.astype(F32)
    lam_im = lam_im.astype(F32)
    mag = jnp.exp(lam_re * dt)
    a_re = mag * jnp.cos(lam_im * dt)
    a_im = mag * jnp.sin(lam_im * dt)
    e_re = a_re - 1.0
    e_im = a_im
    lam_sq = lam_re * lam_re + lam_im * lam_im
    coef_re = (e_re * lam_re + e_im * lam_im) / lam_sq
    coef_im = (e_im * lam_re - e_re * lam_im) / lam_sq
    b_re = b_re.astype(F32)
    b_im = b_im.astype(F32)
    bb_re = coef_re[..., None] * b_re - coef_im[..., None] * b_im
    bb_im = coef_re[..., None] * b_im + coef_im[..., None] * b_re

    gpt = LANES // n_state
    n_tiles = n_groups // gpt
    gps = LANES // S5_GROUP
    g_in_slab = (jnp.arange(n_groups) % gps)
    rows_onehot = jax.nn.one_hot(g_in_slab, gps, dtype=F32)
    w_re = jnp.einsum("gs,gpc->gscp", rows_onehot, bb_re).reshape(n_groups, LANES, n_state)
    w_im = jnp.einsum("gs,gpc->gscp", rows_onehot, bb_im).reshape(n_groups, LANES, n_state)
    w_re = w_re.reshape(n_tiles, gpt, LANES, n_state).transpose(0, 2, 1, 3).reshape(n_tiles, LANES, LANES)
    w_im = w_im.reshape(n_tiles, gpt, LANES, n_state).transpose(0, 2, 1, 3).reshape(n_tiles, LANES, LANES)
    wb = jnp.concatenate([w_re, w_im], axis=-1).astype(BF16)

    n_flat = n_groups * n_state
    out_w = 2 * LANES
    g_per_out = out_w // S5_GROUP
    n_out_tiles = n_groups // g_per_out
    oh = jax.nn.one_hot(jnp.arange(n_groups) % g_per_out, g_per_out, dtype=F32)
    wcr = jnp.einsum("gs,gcp->gpsc", oh, c_re.astype(F32)).reshape(n_out_tiles, g_per_out * n_state, out_w)
    wci = jnp.einsum("gs,gcp->gpsc", oh, -c_im.astype(F32)).reshape(n_out_tiles, g_per_out * n_state, out_w)

    def powers(k):
        return ((mag ** k) * jnp.cos(lam_im * dt * k)).reshape(1, n_flat), \
               ((mag ** k) * jnp.sin(lam_im * dt * k)).reshape(1, n_flat)

    return (wb, a_re.reshape(1, n_flat), a_im.reshape(1, n_flat), [powers(float(k)) for k in seg_lens],
            wcr.astype(BF16), wci.astype(BF16))


def _layer(x_p, x_s, st, lp):
    (w_in, b_gates, conv_w, conv_b, norm_w, lam_re, lam_im, log_dt, b_re, b_im, c_re, c_im,
     d_skip, w_glu, b_glu, w_out, ln1_g, ln1_b, w_r1, b_r1, w_r2, b_r2, w_gate, w_up, w_down,
     ln2_g, ln2_b, alpha) = lp
    state_c, state_n, state_m, state_conv, state_re, state_im = st
    bp, tp, d = x_p.shape
    bs, ts, _ = x_s.shape
    dm = norm_w.shape[0]
    head_dim = dm // M_HEADS
    ds5 = d_skip.shape[0]
    n_groups, n_state = lam_re.shape
    n_flat = n_groups * n_state

    o0 = 2 * dm
    o1 = o0 + 2 * dm
    wqk = w_in[:, :o0].astype(BF16)
    wv = w_in[:, o0:o0 + dm].astype(BF16)
    wo = w_in[:, o0 + dm:o1].astype(BF16)
    wg = jnp.pad(w_in[:, o1:o1 + 2 * M_HEADS], ((0, 0), (0, LANES - 2 * M_HEADS))).astype(BF16)
    wu = w_in[:, o1 + 2 * M_HEADS:].astype(BF16)
    bg = jnp.pad(b_gates.astype(F32), (0, LANES - 2 * M_HEADS)).reshape(1, LANES)
    w_out_b = w_out.astype(BF16)
    wr = jnp.concatenate([w_r1, jnp.transpose(w_r2, (1, 0, 2)).reshape(d, -1)], axis=1).astype(F32)
    n_logits = wr.shape[1]
    wr = jnp.pad(wr, ((0, 0), (0, LANES - n_logits)))
    wrh = wr.astype(BF16)
    wrl = (wr - wrh.astype(F32)).astype(BF16)
    br = jnp.pad(jnp.concatenate([b_r1, b_r2.reshape(-1)]).astype(F32), (0, LANES - n_logits))
    br = br.reshape(LANES, 1)

    blk_p = min(tp, PERM_ROWS)
    blk_s = min(ts, PERM_ROWS)
    seg_p = blk_p // N_STREAMS
    seg_s = blk_s // N_STREAMS
    wb, a_re, a_im, (as_p, as_s), wcr, wci = _s5_tables(lam_re, lam_im, log_dt, b_re, b_im, c_re, c_im,
                                                         (seg_p, seg_s))

    outs = []
    shared = None
    row_offset = 0
    for x, blk, seg, a_seg, zero_state in ((x_p, blk_p, seg_p, as_p, True), (x_s, blk_s, seg_s, as_s, False)):
        b, t, _ = x.shape
        x2 = x.reshape(b * t, d)
        perm_np = _stream_perm(blk, seg)
        perm = jnp.asarray(perm_np, BF16)
        permt = jnp.asarray(perm_np[:blk, :blk].T, BF16)

        if zero_state:
            c0 = jnp.zeros((b, M_HEADS, head_dim, head_dim), F32)
            n0 = jnp.zeros((b, M_HEADS, head_dim), F32)
            m0 = jnp.zeros((b, M_HEADS), F32)
            conv0 = jnp.zeros((b, CONV_W - 1, 2 * dm), F32)
            re0 = jnp.zeros((b, n_groups, n_state), F32)
            im0 = jnp.zeros((b, n_groups, n_state), F32)
        else:
            c0, n0, m0, conv0, re0, im0 = (state_c.astype(F32), state_n.astype(F32), state_m.astype(F32),
                                           state_conv.astype(F32), state_re.astype(F32), state_im.astype(F32))
        n0p = jnp.pad(n0, ((0, 0), (0, SUBLANES - M_HEADS), (0, 0)))
        m0p = jnp.broadcast_to(jnp.pad(m0, ((0, 0), (0, SUBLANES - M_HEADS)))[:, :, None], (b, SUBLANES, LANES))
        tail0 = jnp.pad(conv0, ((0, 0), (SUBLANES - (CONV_W - 1), 0), (0, 0)))
        chunk = min(t, LANES)
        u, hm, c_new, n_new, m_new, qk_tail = _mixer_in(
            x2, perm, wqk, wv, wo, wu, wg, c0, n0p, m0p, tail0, conv_w.astype(F32),
            conv_b.astype(F32).reshape(1, -1), bg, norm_w.astype(F32).reshape(1, -1), batch=b, seq=t, chunk=chunk)
        s5_params = (wb, a_re, a_im, a_seg[0], a_seg[1], wcr, wci, d_skip.astype(F32).reshape(1, -1),
                     w_glu.astype(BF16), b_glu.astype(F32).reshape(1, -1), permt)
        out_params = (w_out_b, ln1_g.astype(F32).reshape(1, -1), ln1_b.astype(F32).reshape(1, -1), wrh, wrl, br)
        n_total = bp * tp + bs * ts
        re0 = re0.reshape(b, 1, n_flat)
        im0 = im0.reshape(b, 1, n_flat)
        hs, re_new, im_new = _s5(u, re0, im0, *s5_params, batch=b, seq=t, block=blk)
        shared = _outproj(x2, hm, hs, *out_params, shared, alpha=alpha, n_total=n_total, row_offset=row_offset)
        row_offset += b * t
        conv_new = qk_tail[:, SUBLANES - (CONV_W - 1):]
        states = (c_new, n_new[:, :M_HEADS], m_new[:, :M_HEADS, 0], conv_new,
                  re_new.reshape(b, n_groups, n_state), im_new.reshape(b, n_groups, n_state))
        outs.append(states)

    st_p, st_s = outs
    x1_all, route_all = shared
    yp, ys = _moe(x1_all, route_all, w_gate.astype(BF16), w_up.astype(BF16), w_down.astype(BF16),
                  ln2_g.astype(F32).reshape(1, -1), ln2_b.astype(F32).reshape(1, -1), alpha=alpha,
                  n_prompt=bp * tp)
    return yp.reshape(bp, tp, d), ys.reshape(bs, ts, d), st_p, st_s


def kernel(x_prompt, x_sample, state_mlstm_C, state_mlstm_n, state_mlstm_m, state_conv, state_s5_re, state_s5_im, w_in, b_gates, conv_w, conv_b, mlstm_norm_w, s5_lam_re, s5_lam_im, s5_log_dt, s5_b_re, s5_b_im, s5_c_re, s5_c_im, s5_d, w_glu, b_glu, w_out, ln1_g, ln1_b, w_r1, b_r1, w_r2, b_r2, w_gate, w_up, w_down, ln2_g, ln2_b):
    depth = w_in.shape[0]
    alpha = (2 * depth) ** 0.25
    yp, ys = x_prompt, x_sample
    sts_p, sts_s = [], []
    for l in range(depth):
        lp = (w_in[l], b_gates[l], conv_w[l], conv_b[l], mlstm_norm_w[l], s5_lam_re[l], s5_lam_im[l],
              s5_log_dt[l], s5_b_re[l], s5_b_im[l], s5_c_re[l], s5_c_im[l], s5_d[l], w_glu[l], b_glu[l],
              w_out[l], ln1_g[l], ln1_b[l], w_r1[l], b_r1[l], w_r2[l], b_r2[l], w_gate[l], w_up[l],
              w_down[l], ln2_g[l], ln2_b[l], alpha)
        st = (state_mlstm_C[l], state_mlstm_n[l], state_mlstm_m[l], state_conv[l], state_s5_re[l],
              state_s5_im[l])
        yp, ys, sp, ss = _layer(yp, ys, st, lp)
        sts_p.append(sp)
        sts_s.append(ss)
    stack = lambda sts, i: jnp.stack([s[i] for s in sts])
    return (yp, ys) + tuple(stack(sts_p, i) for i in range(6)) + tuple(stack(sts_s, i) for i in range(6))
```

```python
import functools
import math

import jax
import jax.numpy as jnp
import numpy as np
from jax import lax
from jax.experimental import pallas as pl
from jax.experimental.pallas import tpu as pltpu

F32 = jnp.float32
BF16 = jnp.bfloat16

LANES = 128
SUBLANES = 8
VMEM_LIMIT = 56 * 1024 * 1024

M_HEADS = 4
CONV_W = 4
S5_GROUP = 16
N_EXPERT_GROUPS = 4
EXPERTS_PER_GROUP = 4
N_PAIRS = 6
N_BINS = N_EXPERT_GROUPS * N_PAIRS
LN_EPS = 1e-5
NEG = -1e30

IN_TILE = 512
PERM_ROWS = 256
OUT_ROWS = 256
MOE_TILE = 256
ISSUE_UNROLL = 8
N_STREAMS = SUBLANES
SCAN_LANES = 512


def _dot(a, b):
    return jnp.dot(a, b, preferred_element_type=F32)


def _dot_nt(a, b):
    return lax.dot_general(a, b, (((1,), (1,)), ((), ())), preferred_element_type=F32)


def _resident(shape):
    nd = len(shape)
    return pl.BlockSpec(shape, lambda *_: (0,) * nd, pipeline_mode=pl.Buffered(1))


def _params(*sem):
    return pltpu.CompilerParams(dimension_semantics=sem, vmem_limit_bytes=VMEM_LIMIT)


def _token_pitch(d):
    return -(-(d // LANES + 1) // SUBLANES) * SUBLANES


def _stream_perm(block, seg):
    n_seg = block // seg
    r = np.arange(block)
    src = (r % n_seg) * seg + r // n_seg
    p = np.zeros((block, block), np.float32)
    p[r, src] = 1.0
    reps = PERM_ROWS // block
    return np.kron(np.eye(reps, dtype=np.float32), p)


def _cumsum_rows(x):
    n = x.shape[0]
    row = lax.broadcasted_iota(jnp.int32, x.shape, 0)
    s = 1
    while s < n:
        x = x + jnp.where(row >= s, pltpu.roll(x, s, 0), 0.0)
        s *= 2
    return x


def _transpose_gate_cols(x):
    sel = (lax.broadcasted_iota(jnp.int32, (SUBLANES, LANES), 0)
           == lax.broadcasted_iota(jnp.int32, (SUBLANES, LANES), 1)).astype(BF16)
    hi = x.astype(BF16)
    r1 = x - hi.astype(F32)
    mid = r1.astype(BF16)
    lo = (r1 - mid.astype(F32)).astype(BF16)
    return _dot_nt(sel, hi) + _dot_nt(sel, mid) + _dot_nt(sel, lo)


def _mixer_in_kernel(x_ref, perm_ref, wqk_ref, wv_ref, wo_ref, wu_ref, wg_ref,
                     c0_ref, n0_ref, m0_ref, tail0_ref, cw_ref, cb_ref, bg_ref, nw_ref,
                     u_ref, h_ref, c_out, n_out, m_out, tail_out,
                     qk_s, v_s, og_s, g_s, ext_ref, c_ref, n_ref, m_ref, *, chunk, head_dim):
    ci = pl.program_id(1)
    n_seq = c_ref.shape[0]

    @pl.when(ci == 0)
    def _():
        c_ref[...] = c0_ref[...]
        n_ref[...] = n0_ref[...]
        m_ref[...] = m0_ref[...]
        ext_ref[:, 0:SUBLANES, :] = tail0_ref[...]

    tile_rows = x_ref.shape[0]
    seq_rows = tile_rows // n_seq
    group = perm_ref.shape[0]
    def project(r0):
        rows = slice(r0, r0 + group)
        xb = x_ref[rows, :].astype(BF16)
        qk_s[rows, :] = _dot(xb, wqk_ref[...])
        yield
        g_s[rows, :] = _dot(xb, wg_ref[...])
        v_s[rows, :] = _dot(xb, wv_ref[...]).astype(BF16)
        yield
        og_s[rows, :] = _dot(xb, wo_ref[...])
        yield
        u_ref[rows, :] = _dot(_dot(perm_ref[...], xb).astype(BF16), wu_ref[...])
        yield

    def chunks(c_lo, c_hi):
        for c0 in range(c_lo, c_hi, chunk):
            s = c0 // seq_rows
            yield from _mlstm_chunk(slice(c0, c0 + chunk), qk_s, v_s, og_s, g_s, cw_ref, cb_ref, bg_ref, nw_ref,
                                    h_ref, ext_ref.at[s], c_ref.at[s], n_ref.at[s], m_ref.at[s], head_dim=head_dim)

    def emit(*weighted):
        live = list(weighted)
        while live:
            for item in list(live):
                turns, gen = item
                for _ in range(turns):
                    try:
                        next(gen)
                    except StopIteration:
                        live.remove(item)
                        break

    emit((1, project(0)))
    for r0 in range(group, tile_rows, group):
        n_chunk_stages = (group // chunk) * (M_HEADS + 1)
        emit((1, project(r0)), (-(-n_chunk_stages // 4), chunks(r0 - group, r0)))
    emit((1, chunks(tile_rows - group, tile_rows)))

    @pl.when(ci == pl.num_programs(1) - 1)
    def _():
        c_out[...] = c_ref[...]
        n_out[...] = n_ref[...]
        m_out[...] = m_ref[...]
        tail_out[...] = ext_ref[:, 0:SUBLANES, :]


def _mlstm_chunk(rows, qk_ref, v_ref, og_ref, g_ref, cw_ref, cb_ref, bg_ref, nw_ref,
                 h_ref, ext_ref, c_ref, n_ref, m_ref, *, head_dim):
    dm = M_HEADS * head_dim
    L = rows.stop - rows.start

    ext_ref[SUBLANES:SUBLANES + L, :] = qk_ref[rows, :]
    full = ext_ref[...]
    acc = cb_ref[...] + full[SUBLANES:, :] * cw_ref[CONV_W - 1:CONV_W, :]
    for back in range(1, CONV_W):
        tap = cw_ref[CONV_W - 1 - back:CONV_W - back, :]
        acc = acc + pltpu.roll(full, back, 0)[SUBLANES:, :] * tap
    tail = ext_ref[L:L + SUBLANES, :]
    ext_ref[0:SUBLANES, :] = tail
    qk = acc * jax.nn.sigmoid(acc)

    gates = g_ref[rows, :] + bg_ref[...]
    fpre = pltpu.roll(gates, LANES - M_HEADS, 1)
    logf = jnp.minimum(fpre, 0.0) - jnp.log1p(jnp.exp(-jnp.abs(fpre)))
    bcum = _cumsum_rows(logf)
    rrow = _transpose_gate_cols(gates - bcum)
    tri = (lax.broadcasted_iota(jnp.int32, (L, L), 0) >= lax.broadcasted_iota(jnp.int32, (L, L), 1))
    yield

    for h in range(M_HEADS):
        lo, hi = h * head_dim, (h + 1) * head_dim
        q = qk[:, lo:hi]
        k = qk[:, dm + lo:dm + hi] * (head_dim ** -0.5)
        v = v_ref[rows, lo:hi]
        qb = q.astype(BF16)
        kb = k.astype(BF16)
        b_col = bcum[:, h:h + 1]
        ig_col = gates[:, h:h + 1]
        m_prev = m_ref[h:h + 1, 0:1]
        c_prev = c_ref[h]
        n_prev = n_ref[h:h + 1, :]

        dmat = jnp.where(tri, b_col + rrow[h:h + 1, :], NEG)
        inter = b_col + m_prev
        m_t = jnp.maximum(inter, jnp.max(dmat, axis=-1, keepdims=True))
        w_intra = jnp.exp(dmat - m_t)
        w_inter = jnp.exp(inter - m_t)
        s = _dot_nt(qb, kb) * w_intra
        num = w_inter * _dot(qb, c_prev.astype(BF16)) + _dot(s.astype(BF16), v)
        den = w_inter * jnp.sum(q * n_prev, axis=-1, keepdims=True) + jnp.sum(s, axis=-1, keepdims=True)
        hh = num / jnp.maximum(jnp.abs(den), jnp.exp(-m_t))

        m_new = m_t[L - 1:L, :]
        b_last = b_col[L - 1:L, :]
        w_s = jnp.exp(b_last - b_col + ig_col - m_new)
        decay = jnp.exp(b_last + m_prev - m_new)
        kw = k * w_s
        c_ref[h] = decay * c_prev + _dot(kw.T.astype(BF16), v)
        n_ref[h:h + 1, :] = decay * n_prev + jnp.sum(kw, axis=0, keepdims=True)
        m_ref[h:h + 1, :] = jnp.broadcast_to(m_new, (1, LANES))

        mu = jnp.mean(hh, axis=-1, keepdims=True)
        hc = hh - mu
        var = jnp.mean(hc * hc, axis=-1, keepdims=True)
        hn = hc * lax.rsqrt(var + LN_EPS) * nw_ref[:, lo:hi]
        h_ref[rows, lo:hi] = (jax.nn.sigmoid(og_ref[rows, lo:hi]) * hn).astype(BF16)
        yield


def _mixer_in(x2, perm, wqk, wv, wo, wu, wg, c0, n0p, m0p, tail0, conv_w, conv_b, bg, norm_w,
              *, batch, seq, chunk):
    n, d = x2.shape
    dm = wv.shape[1]
    ds5 = wu.shape[1]
    head_dim = dm // M_HEADS
    n_seq = PERM_ROWS // seq if seq < PERM_ROWS and batch % (PERM_ROWS // seq) == 0 else 1
    tile = min(seq, IN_TILE) * n_seq
    nt = seq * n_seq // tile
    rows = lambda w: pl.BlockSpec((tile, w), lambda b, c: (b * nt + c, 0))
    per_b = lambda *s: pl.BlockSpec((n_seq,) + s, lambda b, c: (b,) + (0,) * len(s))
    group = min(tile, PERM_ROWS)
    weights = [perm[:group, :group], wqk, wv, wo, wu, wg]
    small = [conv_w, conv_b, bg, norm_w]
    return pl.pallas_call(
        functools.partial(_mixer_in_kernel, chunk=chunk, head_dim=head_dim),
        grid=(batch // n_seq, nt),
        in_specs=[rows(d)] + [_resident(w.shape) for w in weights]
                 + [per_b(M_HEADS, head_dim, head_dim), per_b(SUBLANES, head_dim), per_b(SUBLANES, LANES),
                    per_b(SUBLANES, 2 * dm)] + [_resident(a.shape) for a in small],
        out_specs=[rows(ds5), rows(dm), per_b(M_HEADS, head_dim, head_dim), per_b(SUBLANES, head_dim),
                   per_b(SUBLANES, LANES), per_b(SUBLANES, 2 * dm)],
        out_shape=[jax.ShapeDtypeStruct((n, ds5), F32), jax.ShapeDtypeStruct((n, dm), BF16),
                   jax.ShapeDtypeStruct((batch, M_HEADS, head_dim, head_dim), F32),
                   jax.ShapeDtypeStruct((batch, SUBLANES, head_dim), F32),
                   jax.ShapeDtypeStruct((batch, SUBLANES, LANES), F32),
                   jax.ShapeDtypeStruct((batch, SUBLANES, 2 * dm), F32)],
        scratch_shapes=[pltpu.VMEM((tile, 2 * dm), F32), pltpu.VMEM((tile, dm), BF16),
                        pltpu.VMEM((tile, dm), F32), pltpu.VMEM((tile, LANES), F32),
                        pltpu.VMEM((n_seq, chunk + SUBLANES, 2 * dm), F32),
                        pltpu.VMEM((n_seq, M_HEADS, head_dim, head_dim), F32),
                        pltpu.VMEM((n_seq, SUBLANES, head_dim), F32),
                        pltpu.VMEM((n_seq, SUBLANES, LANES), F32)],
        compiler_params=_params("arbitrary", "arbitrary"),
        name="mixer_in",
    )(x2, *weights, c0, n0p, m0p, tail0, *small)


def _s5_kernel(u_ref, sre0_ref, sim0_ref, wb_ref, are_ref, aim_ref, asre_ref, asim_ref,
               wcr_ref, wci_ref, dskip_ref, wglu_ref, bglu_ref, permt_ref,
               hs_ref, sre_out, sim_out,
               xr_ref, xi_ref, hr_ref, hi_ref, *, seg):
    bi = pl.program_id(1)

    @pl.when(bi == 0)
    def _():
        hr_ref[...] = sre0_ref[0]
        hi_ref[...] = sim0_ref[0]

    _s5_scan(u_ref, wb_ref, are_ref, aim_ref, asre_ref, asim_ref, xr_ref, xi_ref, hr_ref, hi_ref, seg=seg)
    hs_ref[...] = _s5_output(u_ref, wcr_ref, wci_ref, dskip_ref, wglu_ref, bglu_ref, permt_ref, xr_ref, xi_ref)

    @pl.when(bi == pl.num_programs(1) - 1)
    def _():
        sre_out[0] = hr_ref[...]
        sim_out[0] = hi_ref[...]


def _s5_scan(u_ref, wb_ref, are_ref, aim_ref, asre_ref, asim_ref, xr_ref, xi_ref, hr_ref, hi_ref, *, seg):
    n_state = xr_ref.shape[1]

    ub = u_ref[...].astype(BF16)
    tiles_per_slab = LANES // (2 * S5_GROUP)
    row = lax.broadcasted_iota(jnp.int32, (N_STREAMS, SCAN_LANES), 0)
    for sl in range(n_state // SCAN_LANES):
        cols = slice(sl * SCAN_LANES, (sl + 1) * SCAN_LANES)
        for j in range(sl * (SCAN_LANES // LANES), (sl + 1) * (SCAN_LANES // LANES)):
            cs = (j // tiles_per_slab) * LANES
            xj = _dot(ub[:, cs:cs + LANES], wb_ref[j])
            xr_ref[:, j * LANES:(j + 1) * LANES] = xj[:, :LANES]
            xi_ref[:, j * LANES:(j + 1) * LANES] = xj[:, LANES:]
        ar = jnp.broadcast_to(are_ref[:, cols], (N_STREAMS, SCAN_LANES))
        ai = jnp.broadcast_to(aim_ref[:, cols], (N_STREAMS, SCAN_LANES))

        er = xr_ref[0:N_STREAMS, cols]
        ei = xi_ref[0:N_STREAMS, cols]
        for i in range(1, seg):
            rows = slice(i * N_STREAMS, (i + 1) * N_STREAMS)
            nr = ar * er - ai * ei + xr_ref[rows, cols]
            ni = ar * ei + ai * er + xi_ref[rows, cols]
            xr_ref[rows, cols] = nr
            xi_ref[rows, cols] = ni
            er, ei = nr, ni

        asr = asre_ref[:, cols]
        asi = asim_ref[:, cols]
        cr = hr_ref[:, cols]
        ci = hi_ref[:, cols]
        cmr = jnp.zeros((N_STREAMS, SCAN_LANES), F32)
        cmi = jnp.zeros((N_STREAMS, SCAN_LANES), F32)
        for k in range(N_STREAMS):
            cmr = jnp.where(row == k, cr, cmr)
            cmi = jnp.where(row == k, ci, cmi)
            nr = asr * cr - asi * ci + er[k:k + 1, :]
            ni = asr * ci + asi * cr + ei[k:k + 1, :]
            cr, ci = nr, ni
        hr_ref[:, cols] = cr
        hi_ref[:, cols] = ci

        dr, di = cmr, cmi
        for i in range(seg):
            rows = slice(i * N_STREAMS, (i + 1) * N_STREAMS)
            dr, di = ar * dr - ai * di, ar * di + ai * dr
            xr_ref[rows, cols] += dr
            xi_ref[rows, cols] += di


def _s5_output(u_ref, wcr_ref, wci_ref, dskip_ref, wglu_ref, bglu_ref, permt_ref, xr_ref, xi_ref):
    n_state = xr_ref.shape[1]
    n_out_tiles = wcr_ref.shape[0]
    kw = n_state // n_out_tiles
    ys = []
    for qt in range(n_out_tiles):
        hr = xr_ref[:, qt * kw:(qt + 1) * kw].astype(BF16)
        hi = xi_ref[:, qt * kw:(qt + 1) * kw].astype(BF16)
        ys.append(_dot(hr, wcr_ref[qt]) + _dot(hi, wci_ref[qt]))
    y = jnp.concatenate(ys, axis=-1) + dskip_ref[...] * u_ref[...]
    gl = 0.5 * y * (1.0 + jnp.tanh(math.sqrt(2.0 / math.pi) * (y + 0.044715 * (y * y * y))))
    z = _dot(gl.astype(BF16), wglu_ref[...]) + bglu_ref[...]
    out = gl * jax.nn.sigmoid(z)
    return _dot(permt_ref[...], out.astype(BF16)).astype(BF16)


def _s5(u, sre0, sim0, wb, a_re, a_im, as_re, as_im, wcr, wci, dskip, wglu, bglu, permt,
        *, batch, seq, block):
    n, ds5 = u.shape
    n_state = a_re.shape[1]
    nb = seq // block
    rows = lambda w: pl.BlockSpec((block, w), lambda b, c: (b * nb + c, 0))
    per_b = pl.BlockSpec((1, 1, n_state), lambda b, c: (b, 0, 0))
    const = lambda a: pl.BlockSpec(a.shape, lambda b, c: (0,) * a.ndim)
    return pl.pallas_call(
        functools.partial(_s5_kernel, seg=block // N_STREAMS),
        grid=(batch, nb),
        in_specs=[rows(ds5), per_b, per_b, const(wb), const(a_re), const(a_im), const(as_re),
                  const(as_im), const(wcr), const(wci), const(dskip), const(wglu), const(bglu),
                  const(permt)],
        out_specs=[rows(ds5), per_b, per_b],
        out_shape=[jax.ShapeDtypeStruct((n, ds5), BF16),
                   jax.ShapeDtypeStruct((batch, 1, n_state), F32),
                   jax.ShapeDtypeStruct((batch, 1, n_state), F32)],
        scratch_shapes=[pltpu.VMEM((block, n_state), F32), pltpu.VMEM((block, n_state), F32),
                        pltpu.VMEM((1, n_state), F32), pltpu.VMEM((1, n_state), F32)],
        compiler_params=_params("arbitrary", "arbitrary"),
        name="s5",
    )(u, sre0, sim0, wb, a_re, a_im, as_re, as_im, wcr, wci, dskip, wglu, bglu, permt)


def _layer_norm(x, g, b):
    mu = jnp.mean(x, axis=-1, keepdims=True)
    xc = x - mu
    var = jnp.mean(xc * xc, axis=-1, keepdims=True)
    return xc * lax.rsqrt(var + LN_EPS) * g + b


def _first_max(cols):
    best = cols[0]
    for c in cols[1:]:
        best = jnp.maximum(best, c)
    flags = []
    taken = None
    for c in cols:
        hit = c == best
        if taken is not None:
            hit = jnp.logical_and(hit, jnp.logical_not(taken))
            taken = jnp.logical_or(taken, hit)
        else:
            taken = hit
        flags.append(hit)
    return best, flags


def _outproj_kernel(*refs, alpha, n_valid_blocks):
    x1_ref, route_ref = refs[-2:]
    i = pl.program_id(0)

    @pl.when(i < n_valid_blocks)
    def _():
        _outproj_tile(*refs, alpha=alpha)

    @pl.when(i >= n_valid_blocks)
    def _():
        x1_ref[...] = jnp.zeros_like(x1_ref)
        route_ref[...] = jnp.zeros_like(route_ref)


def _outproj_tile(x_ref, hm_ref, hs_ref, wo_ref, g1_ref, b1_ref, wrh_ref, wrl_ref, br_ref,
                  *rest, alpha):
    x1_ref, route_ref = rest[-2:]
    starts = range(0, x_ref.shape[0], OUT_ROWS)
    mixes = [_dot(jnp.concatenate([hm_ref[r0:r0 + OUT_ROWS, :], hs_ref[r0:r0 + OUT_ROWS, :]], axis=1), wo_ref[...])
             for r0 in starts]
    for r0, mix in zip(starts, mixes):
        _outproj_rows(r0, mix, x_ref, g1_ref, b1_ref, wrh_ref, wrl_ref, br_ref, x1_ref, route_ref, alpha=alpha)


def _outproj_rows(r0, mix, x_ref, g1_ref, b1_ref, wrh_ref, wrl_ref, br_ref, x1_ref, route_ref, *, alpha):
    tm = OUT_ROWS
    d = x_ref.shape[1]
    pitch = x1_ref.shape[0] // x_ref.shape[0]
    rows = slice(r0, r0 + tm)

    def store_token_tile(j, v):
        for g in range(tm // SUBLANES):
            first = (r0 + g * SUBLANES) * pitch + j
            x1_ref[pl.ds(first, SUBLANES, stride=pitch), :] = v[g * SUBLANES:(g + 1) * SUBLANES, :]

    x1 = _layer_norm(alpha * x_ref[rows, :] + mix, g1_ref[...], b1_ref[...])
    for j in range(d // LANES):
        store_token_tile(j, x1[:, j * LANES:(j + 1) * LANES])

    xh = x1.astype(BF16)
    xl = (x1 - xh.astype(F32)).astype(BF16)
    logits = (_dot(xh, wrh_ref[...]) + _dot(xl, wrh_ref[...]) + _dot(xh, wrl_ref[...])).T + br_ref[...]

    l1 = [logits[g:g + 1, :] for g in range(N_EXPERT_GROUPS)]
    m1, gsel = _first_max(l1)
    denom = l1[0] * 0.0
    for c in l1:
        denom = denom + jnp.exp(c - m1)
    p_g = 1.0 / denom
    gid = l1[0] * 0.0
    l2 = []
    for e in range(EXPERTS_PER_GROUP):
        acc = l1[0] * 0.0
        for g in range(N_EXPERT_GROUPS):
            c0 = N_EXPERT_GROUPS + g * EXPERTS_PER_GROUP + e
            acc = acc + jnp.where(gsel[g], logits[c0:c0 + 1, :], 0.0)
        l2.append(acc)
    for g in range(N_EXPERT_GROUPS):
        gid = gid + jnp.where(gsel[g], float(g), 0.0)

    v1, f1 = _first_max(l2)
    masked = [jnp.where(f1[e], -jnp.inf, l2[e]) for e in range(EXPERTS_PER_GROUP)]
    v2, f2 = _first_max(masked)
    e21 = jnp.exp(v2 - v1)
    w_first = 1.0 / (1.0 + e21)
    w_second = e21 / (1.0 + e21)
    pair_id = l1[0] * 0.0
    gate_a = l1[0] * 0.0
    gate_b = l1[0] * 0.0
    pid = 0
    for a in range(EXPERTS_PER_GROUP):
        for b in range(a + 1, EXPERTS_PER_GROUP):
            ab = jnp.logical_and(f1[a], f2[b])
            ba = jnp.logical_and(f1[b], f2[a])
            pair_id = pair_id + jnp.where(jnp.logical_or(ab, ba), float(pid), 0.0)
            gate_a = gate_a + jnp.where(ab, w_first, 0.0) + jnp.where(ba, w_second, 0.0)
            gate_b = gate_b + jnp.where(ab, w_second, 0.0) + jnp.where(ba, w_first, 0.0)
            pid += 1
    bin_id = gid * float(N_PAIRS) + pair_id
    row = lax.broadcasted_iota(jnp.int32, (LANES, tm), 0)
    route_t = jnp.where(row == 0, bin_id,
                        jnp.where(row == 1, p_g * gate_a, jnp.where(row == 2, p_g * gate_b, 0.0)))
    route_ref[:, rows] = route_t[0:SUBLANES, :]
    store_token_tile(d // LANES, route_t.T)
    for j in range(d // LANES + 1, pitch):
        store_token_tile(j, jnp.zeros((tm, LANES), F32))


def _outproj(x2, hm, hs, wo, g1, b1, wrh, wrl, br_col, shared, *, alpha, n_total, row_offset):
    n, d = x2.shape
    dm = hm.shape[1]
    groups = 2 if all(e % (2 * OUT_ROWS) == 0 for e in (n, row_offset, n_total - row_offset)) else 1
    tile = groups * OUT_ROWS
    blk0 = row_offset // tile
    n_blocks = n // tile
    pitch = _token_pitch(d)
    grid = n_blocks if shared is not None else n_total // tile - blk0
    rows = lambda w: pl.BlockSpec((tile, w), lambda i: (jnp.minimum(i, n_blocks - 1), 0))
    kern = functools.partial(_outproj_kernel, alpha=alpha, n_valid_blocks=n_blocks)
    in_specs = [rows(d), rows(dm), rows(hs.shape[1]), _resident(wo.shape),
                _resident(g1.shape), _resident(b1.shape), _resident(wrh.shape), _resident(wrl.shape),
                _resident(br_col.shape)]
    args = [x2, hm, hs, wo, g1, b1, wrh, wrl, br_col]
    aliases = {}
    if shared is not None:
        in_specs += [pl.BlockSpec(memory_space=pl.ANY), pl.BlockSpec(memory_space=pl.ANY)]
        aliases = {len(args): 0, len(args) + 1: 1}
        args = args + list(shared)
    return pl.pallas_call(
        kern,
        grid=(grid,),
        in_specs=in_specs,
        out_specs=[pl.BlockSpec((tile * pitch, LANES), lambda i: (i + blk0, 0)),
                   pl.BlockSpec((SUBLANES, tile), lambda i: (0, i + blk0))],
        out_shape=[jax.ShapeDtypeStruct((n_total * pitch, LANES), F32),
                   jax.ShapeDtypeStruct((SUBLANES, n_total), F32)],
        input_output_aliases=aliases,
        compiler_params=_params("arbitrary"),
        name="outproj",
    )(*args)


def _moe_kernel(ea_ref, eb_ref, nv_ref, npr_ref,
                idx_ref, idxn_ref, idxp_ref, wga_ref, wua_ref, wda_ref, wgb_ref, wub_ref, wdb_ref,
                g2_ref, b2_ref, x1_hbm,
                yp_hbm, ys_hbm,
                xbuf, obuf, xb_ref, gsem, ssem, *, alpha, n_prompt):
    t = pl.program_id(0)
    nt = pl.num_programs(0)
    slot = t % 2
    tm, d = obuf.shape[1:]
    pitch = xbuf.shape[2]
    n_model_tiles = d // LANES

    def gather_copy(tok, r, s):
        src = x1_hbm.at[pl.ds(pl.multiple_of(tok * pitch, SUBLANES), pitch), :]
        return pltpu.make_async_copy(src, xbuf.at[s, r // SUBLANES, :, r % SUBLANES, :], gsem.at[s])

    def tile_lanes(j):
        return xbuf[slot, :, j].reshape(tm, LANES)

    def scatter_copy(tok, r, s, to_prompt):
        dst = yp_hbm.at[pl.ds(tok, 1)] if to_prompt else ys_hbm.at[pl.ds(tok - n_prompt, 1)]
        return pltpu.make_async_copy(obuf.at[s, pl.ds(r, 1)], dst, ssem.at[s])

    def for_rows(lo, hi, fn):
        n_groups = (hi - lo) // ISSUE_UNROLL

        def group(gi, c):
            base = lo + gi * ISSUE_UNROLL
            for j in range(ISSUE_UNROLL):
                fn(base + j)
            return c

        def single(r, c):
            fn(r)
            return c

        lax.fori_loop(0, n_groups, group, 0)
        lax.fori_loop(lo + n_groups * ISSUE_UNROLL, hi, single, 0)

    def scatter_tile(ids, s, n_prompt_rows, n_rows):
        for_rows(0, n_prompt_rows, lambda r: scatter_copy(ids[0, 0, r], r, s, True).start())
        for_rows(n_prompt_rows, n_rows, lambda r: scatter_copy(ids[0, 0, r], r, s, False).start())

    def wait_rows(src, dst, sem, count):
        bit = tm
        while bit >= 1:
            @pl.when((count & bit) != 0)
            def _(bit=bit):
                pltpu.make_async_copy(src.at[pl.ds(0, bit)], dst.at[pl.ds(0, bit)], sem).wait()
            bit //= 2

    n_rows = nv_ref[t]
    mixed = npr_ref[t] != n_rows
    prev = jnp.maximum(t - 1, 0)
    n_prev = jnp.where(jnp.logical_and(t >= 1, npr_ref[prev] == nv_ref[prev]), nv_ref[prev], 0)

    def wait_gather(s):
        pltpu.make_async_copy(xbuf.at[1 - s], xbuf.at[s], gsem.at[s]).wait()

    @pl.when(t == 0)
    def _():
        for_rows(0, tm, lambda r: gather_copy(idx_ref[0, 0, r], r, 0).start())

    @pl.when(jnp.logical_or(t == 0, nv_ref[prev] > 0))
    def _():
        wait_gather(slot)

    @pl.when(t >= 2)
    def _():
        wait_rows(obuf.at[slot], yp_hbm, ssem.at[slot], nv_ref[t - 2])

    @pl.when(n_rows > 0)
    def _():
        for j in range(n_model_tiles):
            xb_ref[:, j * LANES:(j + 1) * LANES] = tile_lanes(j).astype(BF16)
        n_stages = 8

        def issue_neighbours(stage):
            for r in range(stage * tm // n_stages, (stage + 1) * tm // n_stages):
                gather_copy(idxn_ref[0, 0, r], r, 1 - slot).start(priority=r % 2)

                @pl.when(r < n_prev)
                def _(r=r):
                    scatter_copy(idxp_ref[0, 0, r], r, 1 - slot, True).start(priority=r % 2)

        routing = tile_lanes(n_model_tiles)
        gate_a = routing[:, 1:2]
        gate_b = routing[:, 2:3]
        xb = xb_ref[...]
        issue_neighbours(0)
        ha = _dot(xb, wga_ref[0])
        issue_neighbours(1)
        ha = (ha * jax.nn.sigmoid(ha)) * _dot(xb, wua_ref[0])
        issue_neighbours(2)
        ya = _dot(ha.astype(BF16), wda_ref[0])
        issue_neighbours(3)
        hb = _dot(xb, wgb_ref[0])
        issue_neighbours(4)
        hb = (hb * jax.nn.sigmoid(hb)) * _dot(xb, wub_ref[0])
        issue_neighbours(5)
        yb = _dot(hb.astype(BF16), wdb_ref[0])
        issue_neighbours(6)
        moe = gate_a * ya + gate_b * yb
        issue_neighbours(7)
        x1 = jnp.concatenate([tile_lanes(j) for j in range(n_model_tiles)], axis=1)
        obuf[slot] = _layer_norm(alpha * x1 + moe, g2_ref[...], b2_ref[...])

    @pl.when(n_rows == 0)
    def _():
        scatter_tile(idxp_ref, 1 - slot, n_prev, n_prev)

    now = jnp.logical_or(mixed, t == nt - 1)
    scatter_tile(idx_ref, slot, jnp.where(now, npr_ref[t], 0), jnp.where(now, n_rows, 0))

    @pl.when(t == nt - 1)
    def _():
        @pl.when(n_rows > 0)
        def _():
            wait_gather(1 - slot)

        wait_rows(obuf.at[slot], yp_hbm, ssem.at[slot], nv_ref[t])

        @pl.when(t >= 1)
        def _():
            wait_rows(obuf.at[1 - slot], yp_hbm, ssem.at[1 - slot], nv_ref[t - 1])


def _moe(x1_all, route_all, wg, wu, wd, g2, b2, *, alpha, n_prompt):
    n = route_all.shape[1]
    d = wg.shape[1]
    pitch = x1_all.shape[0] // n
    n_p = n_prompt
    n_s = n - n_p
    tm = MOE_TILE
    nt = n // tm + N_BINS if n % tm == 0 else (n + N_BINS * (tm - 1)) // tm + 1
    bins = route_all[0].astype(jnp.int32)

    onehot = (bins[:, None] == jnp.arange(N_BINS, dtype=jnp.int32)[None, :]).astype(jnp.int32)
    csum = jnp.cumsum(onehot, axis=0)
    rank = jnp.sum(csum * onehot, axis=1) - 1
    cnt = csum[n - 1]
    cnt_p = csum[n_p - 1]
    tiles_b = (cnt + tm - 1) // tm
    tile_end = jnp.cumsum(tiles_b)
    tile_start = tile_end - tiles_b
    slot_of_token = jnp.sum(onehot * tile_start[None, :], axis=1) * tm + rank
    idx = jnp.zeros((nt * tm,), jnp.int32).at[slot_of_token].set(
        jnp.arange(n, dtype=jnp.int32), unique_indices=True).reshape(nt, tm)
    tid = jnp.arange(nt, dtype=jnp.int32)
    used = tid < tile_end[N_BINS - 1]
    tbin = jnp.minimum(jnp.sum((tid[:, None] >= tile_end[None, :]).astype(jnp.int32), axis=1), N_BINS - 1)
    last_bin = jnp.max(jnp.where(cnt > 0, jnp.arange(N_BINS, dtype=jnp.int32), 0))
    tbin = jnp.where(used, tbin, last_bin)
    tile_onehot = (tbin[:, None] == jnp.arange(N_BINS, dtype=jnp.int32)[None, :]).astype(jnp.int32)
    lookup = lambda table: jnp.sum(tile_onehot * table[None, :], axis=1)
    kk = tid - lookup(tile_start)
    nvalid = jnp.where(used, jnp.clip(lookup(cnt) - kk * tm, 0, tm), 0).astype(jnp.int32)
    nprompt = jnp.where(used, jnp.clip(lookup(cnt_p) - kk * tm, 0, nvalid), 0).astype(jnp.int32)
    pairs = [(a, b) for a in range(EXPERTS_PER_GROUP) for b in range(a + 1, EXPERTS_PER_GROUP)]
    bin_group = np.arange(N_BINS) // N_PAIRS
    ea = lookup(jnp.asarray(bin_group * EXPERTS_PER_GROUP + np.array([p[0] for p in pairs] * N_EXPERT_GROUPS),
                            jnp.int32))
    eb = lookup(jnp.asarray(bin_group * EXPERTS_PER_GROUP + np.array([p[1] for p in pairs] * N_EXPERT_GROUPS),
                            jnp.int32))
    idx3 = idx.reshape(nt, 1, tm)

    dff = wg.shape[2]
    wspec_in = lambda sel: pl.BlockSpec((1, d, dff), lambda t, ea, eb, nv, npr: (sel(ea, eb)[t], 0, 0))
    wspec_out = lambda sel: pl.BlockSpec((1, dff, d), lambda t, ea, eb, nv, npr: (sel(ea, eb)[t], 0, 0))
    first = lambda a, b: a
    second = lambda a, b: b
    const2 = lambda a: pl.BlockSpec(a.shape, lambda t, *_: (0,) * a.ndim)
    smem_rows = lambda fn: pl.BlockSpec((1, 1, tm), fn, memory_space=pltpu.SMEM)
    grid_spec = pltpu.PrefetchScalarGridSpec(
        num_scalar_prefetch=4,
        grid=(nt,),
        in_specs=[smem_rows(lambda t, *_: (t, 0, 0)),
                  smem_rows(lambda t, *_: (jnp.minimum(t + 1, nt - 1), 0, 0)),
                  smem_rows(lambda t, *_: (jnp.maximum(t - 1, 0), 0, 0)),
                  wspec_in(first), wspec_in(first), wspec_out(first),
                  wspec_in(second), wspec_in(second), wspec_out(second),
                  const2(g2), const2(b2),
                  pl.BlockSpec(memory_space=pl.ANY)],
        out_specs=[pl.BlockSpec(memory_space=pl.ANY), pl.BlockSpec(memory_space=pl.ANY)],
        scratch_shapes=[pltpu.VMEM((2, tm // SUBLANES, pitch, SUBLANES, LANES), F32),
                        pltpu.VMEM((2, tm, d), F32),
                        pltpu.VMEM((tm, d), BF16),
                        pltpu.SemaphoreType.DMA((2,)), pltpu.SemaphoreType.DMA((2,))],
    )
    return pl.pallas_call(
        functools.partial(_moe_kernel, alpha=alpha, n_prompt=n_p),
        grid_spec=grid_spec,
        out_shape=[jax.ShapeDtypeStruct((n_p, d), F32), jax.ShapeDtypeStruct((n_s, d), F32)],
        compiler_params=_params("arbitrary"),
        name="moe",
    )(ea, eb, nvalid, nprompt, idx3, idx3, idx3, wg, wu, wd, wg, wu, wd, g2, b2, x1_all)


def _s5_tables(lam_re, lam_im, log_dt, b_re, b_im, c_re, c_im, seg_lens):
    n_groups, n_state = lam_re.shape
    dt = jnp.exp(log_dt.astype(F32))[:, None]
    lam_re = lam_re.astype(F32)
    lam_im = lam_im.astype(F32)
    mag = jnp.exp(lam_re * dt)
    a_re = mag * jnp.cos(lam_im * dt)
    a_im = mag * jnp.sin(lam_im * dt)
    e_re = a_re - 1.0
    e_im = a_im
    lam_sq = lam_re * lam_re + lam_im * lam_im
    coef_re = (e_re * lam_re + e_im * lam_im) / lam_sq
    coef_im = (e_im * lam_re - e_re * lam_im) / lam_sq
    b_re = b_re.astype(F32)
    b_im = b_im.astype(F32)
    bb_re = coef_re[..., None] * b_re - coef_im[..., None] * b_im
    bb_im = coef_re[..., None] * b_im + coef_im[..., None] * b_re

    gpt = LANES // n_state
    n_tiles = n_groups // gpt
    gps = LANES // S5_GROUP
    g_in_slab = (jnp.arange(n_groups) % gps)
    rows_onehot = jax.nn.one_hot(g_in_slab, gps, dtype=F32)
    w_re = jnp.einsum("gs,gpc->gscp", rows_onehot, bb_re).reshape(n_groups, LANES, n_state)
    w_im = jnp.einsum("gs,gpc->gscp", rows_onehot, bb_im).reshape(n_groups, LANES, n_state)
    w_re = w_re.reshape(n_tiles, gpt, LANES, n_state).transpose(0, 2, 1, 3).reshape(n_tiles, LANES, LANES)
    w_im = w_im.reshape(n_tiles, gpt, LANES, n_state).transpose(0, 2, 1, 3).reshape(n_tiles, LANES, LANES)
    wb = jnp.concatenate([w_re, w_im], axis=-1).astype(BF16)

    n_flat = n_groups * n_state
    out_w = 2 * LANES
    g_per_out = out_w // S5_GROUP
    n_out_tiles = n_groups // g_per_out
    oh = jax.nn.one_hot(jnp.arange(n_groups) % g_per_out, g_per_out, dtype=F32)
    wcr = jnp.einsum("gs,gcp->gpsc", oh, c_re.astype(F32)).reshape(n_out_tiles, g_per_out * n_state, out_w)
    wci = jnp.einsum("gs,gcp->gpsc", oh, -c_im.astype(F32)).reshape(n_out_tiles, g_per_out * n_state, out_w)

    def powers(k):
        return ((mag ** k) * jnp.cos(lam_im * dt * k)).reshape(1, n_flat), \
               ((mag ** k) * jnp.sin(lam_im * dt * k)).reshape(1, n_flat)

    return (wb, a_re.reshape(1, n_flat), a_im.reshape(1, n_flat), [powers(float(k)) for k in seg_lens],
            wcr.astype(BF16), wci.astype(BF16))


def _layer(x_p, x_s, st, lp):
    (w_in, b_gates, conv_w, conv_b, norm_w, lam_re, lam_im, log_dt, b_re, b_im, c_re, c_im,
     d_skip, w_glu, b_glu, w_out, ln1_g, ln1_b, w_r1, b_r1, w_r2, b_r2, w_gate, w_up, w_down,
     ln2_g, ln2_b, alpha) = lp
    state_c, state_n, state_m, state_conv, state_re, state_im = st
    bp, tp, d = x_p.shape
    bs, ts, _ = x_s.shape
    dm = norm_w.shape[0]
    head_dim = dm // M_HEADS
    ds5 = d_skip.shape[0]
    n_groups, n_state = lam_re.shape
    n_flat = n_groups * n_state

    o0 = 2 * dm
    o1 = o0 + 2 * dm
    wqk = w_in[:, :o0].astype(BF16)
    wv = w_in[:, o0:o0 + dm].astype(BF16)
    wo = w_in[:, o0 + dm:o1].astype(BF16)
    wg = jnp.pad(w_in[:, o1:o1 + 2 * M_HEADS], ((0, 0), (0, LANES - 2 * M_HEADS))).astype(BF16)
    wu = w_in[:, o1 + 2 * M_HEADS:].astype(BF16)
    bg = jnp.pad(b_gates.astype(F32), (0, LANES - 2 * M_HEADS)).reshape(1, LANES)
    w_out_b = w_out.astype(BF16)
    wr = jnp.concatenate([w_r1, jnp.transpose(w_r2, (1, 0, 2)).reshape(d, -1)], axis=1).astype(F32)
    n_logits = wr.shape[1]
    wr = jnp.pad(wr, ((0, 0), (0, LANES - n_logits)))
    wrh = wr.astype(BF16)
    wrl = (wr - wrh.astype(F32)).astype(BF16)
    br = jnp.pad(jnp.concatenate([b_r1, b_r2.reshape(-1)]).astype(F32), (0, LANES - n_logits))
    br = br.reshape(LANES, 1)

    blk_p = min(tp, PERM_ROWS)
    blk_s = min(ts, PERM_ROWS)
    seg_p = blk_p // N_STREAMS
    seg_s = blk_s // N_STREAMS
    wb, a_re, a_im, (as_p, as_s), wcr, wci = _s5_tables(lam_re, lam_im, log_dt, b_re, b_im, c_re, c_im,
                                                         (seg_p, seg_s))

    outs = []
    shared = None
    row_offset = 0
    for x, blk, seg, a_seg, zero_state in ((x_p, blk_p, seg_p, as_p, True), (x_s, blk_s, seg_s, as_s, False)):
        b, t, _ = x.shape
        x2 = x.reshape(b * t, d)
        perm_np = _stream_perm(blk, seg)
        perm = jnp.asarray(perm_np, BF16)
        permt = jnp.asarray(perm_np[:blk, :blk].T, BF16)

        if zero_state:
            c0 = jnp.zeros((b, M_HEADS, head_dim, head_dim), F32)
            n0 = jnp.zeros((b, M_HEADS, head_dim), F32)
            m0 = jnp.zeros((b, M_HEADS), F32)
            conv0 = jnp.zeros((b, CONV_W - 1, 2 * dm), F32)
            re0 = jnp.zeros((b, n_groups, n_state), F32)
            im0 = jnp.zeros((b, n_groups, n_state), F32)
        else:
            c0, n0, m0, conv0, re0, im0 = (state_c.astype(F32), state_n.astype(F32), state_m.astype(F32),
                                           state_conv.astype(F32), state_re.astype(F32), state_im.astype(F32))
        n0p = jnp.pad(n0, ((0, 0), (0, SUBLANES - M_HEADS), (0, 0)))
        m0p = jnp.broadcast_to(jnp.pad(m0, ((0, 0), (0, SUBLANES - M_HEADS)))[:, :, None], (b, SUBLANES, LANES))
        tail0 = jnp.pad(conv0, ((0, 0), (SUBLANES - (CONV_W - 1), 0), (0, 0)))
        chunk = min(t, LANES)
        u, hm, c_new, n_new, m_new, qk_tail = _mixer_in(
            x2, perm, wqk, wv, wo, wu, wg, c0, n0p, m0p, tail0, conv_w.astype(F32),
            conv_b.astype(F32).reshape(1, -1), bg, norm_w.astype(F32).reshape(1, -1), batch=b, seq=t, chunk=chunk)
        s5_params = (wb, a_re, a_im, a_seg[0], a_seg[1], wcr, wci, d_skip.astype(F32).reshape(1, -1),
                     w_glu.astype(BF16), b_glu.astype(F32).reshape(1, -1), permt)
        out_params = (w_out_b, ln1_g.astype(F32).reshape(1, -1), ln1_b.astype(F32).reshape(1, -1), wrh, wrl, br)
        n_total = bp * tp + bs * ts
        re0 = re0.reshape(b, 1, n_flat)
        im0 = im0.reshape(b, 1, n_flat)
        hs, re_new, im_new = _s5(u, re0, im0, *s5_params, batch=b, seq=t, block=blk)
        shared = _outproj(x2, hm, hs, *out_params, shared, alpha=alpha, n_total=n_total, row_offset=row_offset)
        row_offset += b * t
        conv_new = qk_tail[:, SUBLANES - (CONV_W - 1):]
        states = (c_new, n_new[:, :M_HEADS], m_new[:, :M_HEADS, 0], conv_new,
                  re_new.reshape(b, n_groups, n_state), im_new.reshape(b, n_groups, n_state))
        outs.append(states)

    st_p, st_s = outs
    x1_all, route_all = shared
    yp, ys = _moe(x1_all, route_all, w_gate.astype(BF16), w_up.astype(BF16), w_down.astype(BF16),
                  ln2_g.astype(F32).reshape(1, -1), ln2_b.astype(F32).reshape(1, -1), alpha=alpha,
                  n_prompt=bp * tp)
    return yp.reshape(bp, tp, d), ys.reshape(bs, ts, d), st_p, st_s


def kernel(x_prompt, x_sample, state_mlstm_C, state_mlstm_n, state_mlstm_m, state_conv, state_s5_re, state_s5_im, w_in, b_gates, conv_w, conv_b, mlstm_norm_w, s5_lam_re, s5_lam_im, s5_log_dt, s5_b_re, s5_b_im, s5_c_re, s5_c_im, s5_d, w_glu, b_glu, w_out, ln1_g, ln1_b, w_r1, b_r1, w_r2, b_r2, w_gate, w_up, w_down, ln2_g, ln2_b):
    depth = w_in.shape[0]
    alpha = (2 * depth) ** 0.25
    yp, ys = x_prompt, x_sample
    sts_p, sts_s = [], []
    for l in range(depth):
        lp = (w_in[l], b_gates[l], conv_w[l], conv_b[l], mlstm_norm_w[l], s5_lam_re[l], s5_lam_im[l],
              s5_log_dt[l], s5_b_re[l], s5_b_im[l], s5_c_re[l], s5_c_im[l], s5_d[l], w_glu[l], b_glu[l],
              w_out[l], ln1_g[l], ln1_b[l], w_r1[l], b_r1[l], w_r2[l], b_r2[l], w_gate[l], w_up[l],
              w_down[l], ln2_g[l], ln2_b[l], alpha)
        st = (state_mlstm_C[l], state_mlstm_n[l], state_mlstm_m[l], state_conv[l], state_s5_re[l],
              state_s5_im[l])
        yp, ys, sp, ss = _layer(yp, ys, st, lp)
        sts_p.append(sp)
        sts_s.append(ss)
    stack = lambda sts, i: jnp.stack([s[i] for s in sts])
    return (yp, ys) + tuple(stack(sts_p, i) for i in range(6)) + tuple(stack(sts_s, i) for i in range(6))
```
